```python
import math
import jax, jax.numpy as jnp
from jax import lax
import numpy as np

D_MODEL = 4096
BATCH = 4
SEQ = 2048
DEPTH = 1
DEC_BATCH = 128
DEC_SEQ = 1
PAST_LEN = 16384
PAGE_SIZE = 128

D_SC = D_MODEL
SC_WIDTH = 3
D_INNER = 2 * D_MODEL
M_HEADDIM = 64
M_HEADS = D_INNER // M_HEADDIM
M_GROUPS = 8
M_STATE = 128
M_CONV = 4
CONV_DIM = D_INNER + 2 * M_GROUPS * M_STATE
SSD_CHUNK = 128
PEER_HEADS = 8
PEER_NKEYS = 128
PEER_TOPK = 16
PEER_QDIM = 256
PEER_QHALF = PEER_QDIM // 2
N_EXPERTS = PEER_NKEYS * PEER_NKEYS
PEER_BLOCK = 64
EPS = 1e-6
IN_PROJ_DIM = 3 * D_SC + D_INNER + CONV_DIM + M_HEADS + 2 * D_MODEL
SPLITS = (D_SC, 2 * D_SC, 3 * D_SC, 3 * D_SC + D_INNER, 3 * D_SC + D_INNER + CONV_DIM,
          3 * D_SC + D_INNER + CONV_DIM + M_HEADS, 3 * D_SC + D_INNER + CONV_DIM + M_HEADS + D_MODEL)

kernel_name = "hybrid_shortconv_ssd_peer_step"


def rmsnorm(x, w):
    xf = x.astype(jnp.float32)
    y = xf * lax.rsqrt(jnp.mean(xf * xf, axis=-1, keepdims=True) + EPS)
    return (y * w.astype(jnp.float32)).astype(x.dtype)


def causal_dwconv(x, buf, w):
    K = w.shape[0]
    L = x.shape[1]
    xp = jnp.concatenate([buf.astype(x.dtype), x], axis=1)
    y = xp[:, 0:L] * w[0]
    for k in range(1, K):
        y = y + xp[:, k:k + L] * w[k]
    return y, xp[:, -(K - 1):]


def ssd_chunked(x, dt, a, bm, cm, h0):
    b, l, H, P = x.shape
    G, N = bm.shape[2], bm.shape[3]
    R = H // G
    chunk = min(SSD_CHUNK, l)
    nc = -(-l // chunk)
    pad = nc * chunk - l
    f32 = jnp.float32
    xdt = x.astype(f32) * dt[..., None]
    da = dt * a
    padw = lambda t: jnp.pad(t, [(0, 0), (0, pad)] + [(0, 0)] * (t.ndim - 2))
    xdt = padw(xdt).reshape(b, nc, chunk, G, R, P).transpose(1, 0, 2, 3, 4, 5)
    da = padw(da).reshape(b, nc, chunk, G, R).transpose(1, 0, 2, 3, 4)
    bc_all = padw(bm.astype(f32)).reshape(b, nc, chunk, G, N).transpose(1, 0, 2, 3, 4)
    cc_all = padw(cm.astype(f32)).reshape(b, nc, chunk, G, N).transpose(1, 0, 2, 3, 4)
    state0 = h0.astype(f32).reshape(b, G, R, P, N)
    mask = jnp.tril(jnp.ones((chunk, chunk), dtype=bool))[None, :, :, None, None]

    def step(state, inp):
        xc, ac, bc, cc = inp
        acum = jnp.cumsum(ac, axis=1)
        seg = acum[:, :, None] - acum[:, None, :]
        lmat = jnp.exp(jnp.where(mask, seg, -jnp.inf))
        cb = jnp.einsum('blgn,bsgn->blsg', cc, bc)
        y_diag = jnp.einsum('blsgr,bsgrp->blgrp', cb[..., None] * lmat, xc)
        y_off = jnp.einsum('blgn,bgrpn,blgr->blgrp', cc, state, jnp.exp(acum))
        decay_to_end = jnp.exp(acum[:, -1:] - acum)
        new_state = state * jnp.exp(acum[:, -1])[..., None, None] + \
            jnp.einsum('bsgn,bsgr,bsgrp->bgrpn', bc, decay_to_end, xc)
        return new_state, y_diag + y_off

    final, ys = lax.scan(step, state0, (xdt, da, bc_all, cc_all))
    y = ys.transpose(1, 0, 2, 3, 4, 5).reshape(b, nc * chunk, H, P)[:, :l]
    return y, final.reshape(b, H, P, N)


def peer_ffn(h, wq, keys, u, v):
    Bt, L, D = h.shape
    t = h.reshape(-1, D)
    T = t.shape[0]
    f32 = jnp.float32
    q = (t @ wq).reshape(T, PEER_HEADS, 2, PEER_QHALF).astype(f32)
    s1 = jnp.einsum('thd,hkd->thk', q[:, :, 0], keys[0].astype(f32))
    s2 = jnp.einsum('thd,hkd->thk', q[:, :, 1], keys[1].astype(f32))
    v1, i1 = lax.top_k(s1, PEER_TOPK)
    v2, i2 = lax.top_k(s2, PEER_TOPK)
    cand = (v1[..., :, None] + v2[..., None, :]).reshape(T, PEER_HEADS, PEER_TOPK * PEER_TOPK)
    cidx = (i1[..., :, None] * PEER_NKEYS + i2[..., None, :]).reshape(T, PEER_HEADS, PEER_TOPK * PEER_TOPK)
    sv, sp = lax.top_k(cand, PEER_TOPK)
    eidx = jnp.take_along_axis(cidx, sp, axis=-1)
    gates = jax.nn.softmax(sv, axis=-1).astype(h.dtype)
    nb = -(-T // PEER_BLOCK)
    pad = nb * PEER_BLOCK - T
    tp = jnp.pad(t, ((0, pad), (0, 0))).reshape(nb, PEER_BLOCK, D)
    ep = jnp.pad(eidx, ((0, pad), (0, 0), (0, 0))).reshape(nb, PEER_BLOCK, PEER_HEADS, PEER_TOPK)
    gp = jnp.pad(gates, ((0, pad), (0, 0), (0, 0))).reshape(nb, PEER_BLOCK, PEER_HEADS, PEER_TOPK)

    def block(args):
        xb, eb, gb = args
        ue = u[eb]
        ve = v[eb]
        act = jax.nn.gelu(jnp.einsum('td,thkd->thk', xb, ue), approximate=False)
        return jnp.einsum('thk,thkd->td', gb * act, ve)

    out = lax.map(block, (tp, ep, gp)).reshape(nb * PEER_BLOCK, D)[:T]
    return out.reshape(Bt, L, D)


def hybrid_layer(x, sc_buf, m_buf, ssm_h, ln1_w, w_in, sc_conv_w, sc_out_w, m_conv_w, m_conv_b,
                 m_dt_bias, m_A_log, m_D, m_norm_w, m_out_w, w_o, ln2_w, peer_wq, peer_keys, peer_u, peer_v):
    Bt, L, _ = x.shape
    h = rmsnorm(x, ln1_w)
    proj = h @ w_in
    sc_b, sc_c, sc_x, z, xbc, dt_raw, gate_a, gate_b = jnp.split(proj, SPLITS, axis=-1)
    sc_conv, sc_buf_new = causal_dwconv(sc_c * sc_x, sc_buf, sc_conv_w)
    y_a = (sc_b * sc_conv) @ sc_out_w
    xbc_conv, m_buf_new = causal_dwconv(xbc, m_buf, m_conv_w)
    xbc_act = jax.nn.silu(xbc_conv + m_conv_b)
    xs, bm, cm = jnp.split(xbc_act, (D_INNER, D_INNER + M_GROUPS * M_STATE), axis=-1)
    dt = jax.nn.softplus(dt_raw.astype(jnp.float32) + m_dt_bias.astype(jnp.float32))
    a = -jnp.exp(m_A_log.astype(jnp.float32))
    xs_h = xs.reshape(Bt, L, M_HEADS, M_HEADDIM)
    y_ssd, ssm_new = ssd_chunked(xs_h, dt, a, bm.reshape(Bt, L, M_GROUPS, M_STATE),
                                 cm.reshape(Bt, L, M_GROUPS, M_STATE), ssm_h)
    y_m = y_ssd.astype(x.dtype) + m_D[:, None] * xs_h
    y_m = rmsnorm(y_m.reshape(Bt, L, D_INNER) * jax.nn.silu(z), m_norm_w)
    y_b = y_m @ m_out_w
    mixed = (jax.nn.sigmoid(gate_a) * y_a + jax.nn.sigmoid(gate_b) * y_b) @ w_o
    x = x + mixed
    x = x + peer_ffn(rmsnorm(x, ln2_w), peer_wq, peer_keys, peer_u, peer_v)
    return x, sc_buf_new, m_buf_new, ssm_new.astype(ssm_h.dtype)


def setup_inputs(seed: int = 0) -> dict:
    key = jax.random.key(seed)
    ks = jax.random.split(key, 24)
    nrm = lambda k, shape, s: jax.random.normal(k, shape, jnp.float32) * s
    dt0 = jnp.exp(jax.random.uniform(ks[10], (DEPTH, M_HEADS), jnp.float32, math.log(1e-3), math.log(1e-1)))
    return {
        "x_prompt": nrm(ks[0], (BATCH, SEQ, D_MODEL), 1.0),
        "x_sample": nrm(ks[1], (DEC_BATCH, DEC_SEQ, D_MODEL), 1.0),
        "state_shortconv": nrm(ks[2], (DEPTH, DEC_BATCH, SC_WIDTH - 1, D_SC), 1.0),
        "state_mamba_conv": nrm(ks[3], (DEPTH, DEC_BATCH, M_CONV - 1, CONV_DIM), 1.0),
        "state_ssm": nrm(ks[4], (DEPTH, DEC_BATCH, M_HEADS, M_HEADDIM, M_STATE), 0.5),
        "ln1_w": 1.0 + nrm(ks[5], (DEPTH, D_MODEL), 0.02),
        "w_in": nrm(ks[6], (DEPTH, D_MODEL, IN_PROJ_DIM), D_MODEL ** -0.5),
        "sc_conv_w": nrm(ks[7], (DEPTH, SC_WIDTH, D_SC), SC_WIDTH ** -0.5),
        "sc_out_w": nrm(ks[8], (DEPTH, D_SC, D_MODEL), D_SC ** -0.5),
        "m_conv_w": nrm(ks[9], (DEPTH, M_CONV, CONV_DIM), M_CONV ** -0.5),
        "m_conv_b": nrm(ks[11], (DEPTH, CONV_DIM), 0.02),
        "m_dt_bias": dt0 + jnp.log(-jnp.expm1(-dt0)),
        "m_A_log": jnp.log(jax.random.uniform(ks[12], (DEPTH, M_HEADS), jnp.float32, 1.0, 16.0)),
        "m_D": 1.0 + nrm(ks[13], (DEPTH, M_HEADS), 0.02),
        "m_norm_w": 1.0 + nrm(ks[14], (DEPTH, D_INNER), 0.02),
        "m_out_w": nrm(ks[15], (DEPTH, D_INNER, D_MODEL), D_INNER ** -0.5),
        "w_o": nrm(ks[16], (DEPTH, D_MODEL, D_MODEL), D_MODEL ** -0.5),
        "ln2_w": 1.0 + nrm(ks[17], (DEPTH, D_MODEL), 0.02),
        "peer_wq": nrm(ks[18], (DEPTH, D_MODEL, PEER_HEADS * PEER_QDIM), D_MODEL ** -0.5),
        "peer_keys": nrm(ks[19], (DEPTH, 2, PEER_HEADS, PEER_NKEYS, PEER_QHALF), PEER_QHALF ** -0.5),
        "peer_u": nrm(ks[20], (DEPTH, N_EXPERTS, D_MODEL), D_MODEL ** -0.5),
        "peer_v": nrm(ks[21], (DEPTH, N_EXPERTS, D_MODEL), PEER_HEADS ** -0.5),
        "final_norm_w": 1.0 + nrm(ks[22], (D_MODEL,), 0.02),
    }


def reference(x_prompt, x_sample, state_shortconv, state_mamba_conv, state_ssm, ln1_w, w_in, sc_conv_w,
              sc_out_w, m_conv_w, m_conv_b, m_dt_bias, m_A_log, m_D, m_norm_w, m_out_w, w_o, ln2_w,
              peer_wq, peer_keys, peer_u, peer_v, final_norm_w):
    yp, ys = x_prompt, x_sample
    bp = x_prompt.shape[0]
    p_sc, p_mc, p_ssm, s_sc, s_mc, s_ssm = [], [], [], [], [], []
    for i in range(DEPTH):
        lw = (ln1_w[i], w_in[i], sc_conv_w[i], sc_out_w[i], m_conv_w[i], m_conv_b[i], m_dt_bias[i],
              m_A_log[i], m_D[i], m_norm_w[i], m_out_w[i], w_o[i], ln2_w[i], peer_wq[i], peer_keys[i],
              peer_u[i], peer_v[i])
        z_sc = jnp.zeros((bp, SC_WIDTH - 1, D_SC), x_prompt.dtype)
        z_mc = jnp.zeros((bp, M_CONV - 1, CONV_DIM), x_prompt.dtype)
        z_ssm = jnp.zeros((bp, M_HEADS, M_HEADDIM, M_STATE), state_ssm.dtype)
        yp, a1, a2, a3 = hybrid_layer(yp, z_sc, z_mc, z_ssm, *lw)
        ys, b1, b2, b3 = hybrid_layer(ys, state_shortconv[i], state_mamba_conv[i], state_ssm[i], *lw)
        p_sc.append(a1); p_mc.append(a2); p_ssm.append(a3)
        s_sc.append(b1); s_mc.append(b2); s_ssm.append(b3)
    y_prompt = rmsnorm(yp, final_norm_w)
    y_sample = rmsnorm(ys, final_norm_w)
    return (y_prompt, y_sample, jnp.stack(p_sc), jnp.stack(p_mc), jnp.stack(p_ssm),
            jnp.stack(s_sc), jnp.stack(s_mc), jnp.stack(s_ssm))
```

```python
import functools

import jax
import jax.numpy as jnp
from jax import lax
from jax.experimental import pallas as pl
from jax.experimental.pallas import tpu as pltpu

F32 = jnp.float32
BF16 = jnp.bfloat16
EPS = 1e-6
PEER_TOPK = 16
SSD_CHUNK = 128
VMEM_LIMIT_BYTES = 56 * 1024 * 1024
NEG_INF = float("-inf")


def _params(n_grid_axes):
    return pltpu.CompilerParams(
        dimension_semantics=("arbitrary",) * n_grid_axes,
        vmem_limit_bytes=VMEM_LIMIT_BYTES,
    )


def _dot(a, b):
    return jnp.dot(a, b, preferred_element_type=F32)


def _dot_nt(a, b):
    return lax.dot_general(a, b, (((1,), (1,)), ((), ())), preferred_element_type=F32)


def _silu(x):
    return x * jax.nn.sigmoid(x)


def _split_hi_lo(x):
    hi = x.astype(BF16)
    lo = (x - hi.astype(F32)).astype(BF16)
    return hi, lo


def _rms(x, w):
    ms = jnp.mean(x * x, axis=-1, keepdims=True)
    return x * lax.rsqrt(ms + EPS) * w


def _norm_in_kernel(xp_ref, xs_ref, w_ref, o_ref, *, n_p, ts):
    i = pl.program_id(0)

    @pl.when(i < n_p)
    def _():
        o_ref[...] = _rms(xp_ref[...], w_ref[...]).astype(o_ref.dtype)

    @pl.when(i == n_p)
    def _():
        o_ref[0:ts, :] = _rms(xs_ref[...], w_ref[...]).astype(o_ref.dtype)


def _norm_in(xp, xs, w, *, bt):
    tp, d = xp.shape
    ts = xs.shape[0]
    n_p = tp // bt
    return pl.pallas_call(
        functools.partial(_norm_in_kernel, n_p=n_p, ts=ts),
        grid=(n_p + 1,),
        in_specs=[
            pl.BlockSpec((bt, d), lambda i: (jnp.minimum(i, n_p - 1), 0)),
            pl.BlockSpec((ts, d), lambda i: (0, 0)),
            pl.BlockSpec((1, d), lambda i: (0, 0)),
        ],
        out_specs=pl.BlockSpec((bt, d), lambda i: (i, 0)),
        out_shape=jax.ShapeDtypeStruct((tp + ts, d), BF16),
        compiler_params=_params(1),
        name="norm_in",
    )(xp, xs, w.reshape(1, d))


def _norm_mid_kernel(x_ref, w_ref, o_ref):
    o_ref[...] = _rms(x_ref[...], w_ref[...]).astype(o_ref.dtype)


def _norm_mid(x, w, *, bt):
    t, d = x.shape
    return pl.pallas_call(
        _norm_mid_kernel,
        grid=(t // bt,),
        in_specs=[pl.BlockSpec((bt, d), lambda i: (i, 0)), pl.BlockSpec((1, d), lambda i: (0, 0))],
        out_specs=pl.BlockSpec((bt, d), lambda i: (i, 0)),
        out_shape=jax.ShapeDtypeStruct((t, d), BF16),
        compiler_params=_params(1),
        name="norm_mid",
    )(x, w.reshape(1, d))


def _norm_out_kernel(x_ref, y_ref, w_ref, op_ref, os_ref, *, n_p, ts):
    i = pl.program_id(0)

    @pl.when(i < n_p)
    def _():
        op_ref[...] = _rms(x_ref[...] + y_ref[...], w_ref[...])

    @pl.when(i == n_p)
    def _():
        os_ref[...] = _rms(x_ref[0:ts, :] + y_ref[0:ts, :], w_ref[...])


def _norm_out(x, y, w, *, tp, ts, bt):
    d = x.shape[1]
    n_p = tp // bt
    return pl.pallas_call(
        functools.partial(_norm_out_kernel, n_p=n_p, ts=ts),
        grid=(n_p + 1,),
        in_specs=[
            pl.BlockSpec((bt, d), lambda i: (i, 0)),
            pl.BlockSpec((bt, d), lambda i: (i, 0)),
            pl.BlockSpec((1, d), lambda i: (0, 0)),
        ],
        out_specs=[
            pl.BlockSpec((bt, d), lambda i: (jnp.minimum(i, n_p - 1), 0)),
            pl.BlockSpec((ts, d), lambda i: (0, 0)),
        ],
        out_shape=[jax.ShapeDtypeStruct((tp, d), F32), jax.ShapeDtypeStruct((ts, d), F32)],
        compiler_params=_params(1),
        name="norm_out",
    )(x, y, w.reshape(1, d))


def _apply_act(acc, act):
    if act == "silu":
        return _silu(acc)
    if act == "sigmoid":
        return jax.nn.sigmoid(acc)
    if act == "softplus":
        return jax.nn.softplus(acc)
    assert act == "none"
    return acc


def _mm_act_kernel(*refs, n_p, bm, ts, act, has_bias, has_resid):
    x_ref, w_ref = refs[0], refs[1]
    k = 2
    b_ref = None
    if has_bias:
        b_ref = refs[k]
        k += 1
    rp_ref = rs_ref = None
    if has_resid:
        rp_ref, rs_ref = refs[k], refs[k + 1]
        k += 2
    o_ref = refs[k]
    i = pl.program_id(1)

    def compute(rows, r_ref):
        acc = _dot(x_ref[0:rows, :], w_ref[...])
        if has_bias:
            acc = acc + b_ref[...]
        acc = _apply_act(acc, act)
        if has_resid:
            acc = acc + r_ref[0:rows, :]
        o_ref[0:rows, :] = acc.astype(o_ref.dtype)

    @pl.when(i < n_p)
    def _():
        compute(bm, rp_ref)

    @pl.when(i == n_p)
    def _():
        compute(ts, rs_ref)


def _mm_act(x, w, *, col0, n, tp, ts, bm, bn, act, out_dtype, bias=None, resid=None, name):
    t, kdim = x.shape
    n_p = tp // bm
    assert tp % bm == 0 and ts <= bm and n % bn == 0 and col0 % bn == 0
    cb = col0 // bn
    in_specs = [
        pl.BlockSpec((bm, kdim), lambda j, i: (i, 0)),
        pl.BlockSpec((kdim, bn), lambda j, i: (0, cb + j)),
    ]
    args = [x, w]
    if bias is not None:
        in_specs.append(pl.BlockSpec((1, bn), lambda j, i: (0, j)))
        args.append(bias.reshape(1, n))
    if resid is not None:
        in_specs.append(pl.BlockSpec((bm, bn), lambda j, i: (jnp.minimum(i, n_p - 1), j)))
        in_specs.append(pl.BlockSpec((ts, bn), lambda j, i: (0, j)))
        args.extend(resid)
    return pl.pallas_call(
        functools.partial(_mm_act_kernel, n_p=n_p, bm=bm, ts=ts, act=act,
                          has_bias=bias is not None, has_resid=resid is not None),
        grid=(n // bn, n_p + 1),
        in_specs=in_specs,
        out_specs=pl.BlockSpec((bm, bn), lambda j, i: (i, j)),
        out_shape=jax.ShapeDtypeStruct((t, n), out_dtype),
        compiler_params=_params(2),
        name=name,
    )(*args)


_CARRY = 8


def _conv_prompt(ubuf, cw_ref, u, *, bm, taps, first_in_seq):
    @pl.when(first_in_seq)
    def _():
        ubuf[0:_CARRY, :] = jnp.zeros((_CARRY, ubuf.shape[1]), F32)

    ubuf[_CARRY:_CARRY + bm, :] = u
    y = ubuf[_CARRY - (taps - 1):_CARRY - (taps - 1) + bm, :] * cw_ref[0:1, :]
    for k in range(1, taps - 1):
        off = _CARRY - (taps - 1) + k
        y = y + ubuf[off:off + bm, :] * cw_ref[k:k + 1, :]
    y = y + u * cw_ref[taps - 1:taps, :]
    return y


def _sc_kernel(x_ref, wb_ref, wc_ref, wx_ref, cw_ref, s0_ref, s1_ref,
               g_ref, tail_ref, us_ref, ubuf, *, n_p, bm, ts, tps):
    i = pl.program_id(1)

    @pl.when(i < n_p)
    def _():
        x = x_ref[...]
        u = _dot(x, wc_ref[...]) * _dot(x, wx_ref[...])
        y = _conv_prompt(ubuf, cw_ref, u, bm=bm, taps=3, first_in_seq=(i % tps) == 0)
        g_ref[...] = (_dot(x, wb_ref[...]) * y).astype(g_ref.dtype)
        tail_ref[0] = u[bm - 2:bm, :]
        ubuf[0:_CARRY, :] = ubuf[bm:bm + _CARRY, :]

    @pl.when(i == n_p)
    def _():
        x = x_ref[0:ts, :]
        u = _dot(x, wc_ref[...]) * _dot(x, wx_ref[...])
        us_ref[...] = u
        y = s0_ref[...] * cw_ref[0:1, :]
        y = y + s1_ref[...] * cw_ref[1:2, :]
        y = y + u * cw_ref[2:3, :]
        g_ref[0:ts, :] = (_dot(x, wb_ref[...]) * y).astype(g_ref.dtype)


def _sc_branch(h, w_in, conv_w, state, *, d_sc, tp, ts, seq, bm, bn):
    t, kdim = h.shape
    n_p = tp // bm
    tps = seq // bm
    nb = tp // seq
    nj = d_sc // bn
    st = state.reshape(ts, 2 * d_sc)
    w_spec = lambda off: pl.BlockSpec((kdim, bn), lambda j, i: (0, off + j))
    return pl.pallas_call(
        functools.partial(_sc_kernel, n_p=n_p, bm=bm, ts=ts, tps=tps),
        grid=(nj, n_p + 1),
        in_specs=[
            pl.BlockSpec((bm, kdim), lambda j, i: (i, 0)),
            w_spec(0), w_spec(nj), w_spec(2 * nj),
            pl.BlockSpec((3, bn), lambda j, i: (0, j)),
            pl.BlockSpec((ts, bn), lambda j, i: (0, j)),
            pl.BlockSpec((ts, bn), lambda j, i: (0, nj + j)),
        ],
        out_specs=[
            pl.BlockSpec((bm, bn), lambda j, i: (i, j)),
            pl.BlockSpec((1, 2, bn), lambda j, i: (jnp.minimum(i, n_p - 1) // tps, 0, j)),
            pl.BlockSpec((ts, bn), lambda j, i: (0, j)),
        ],
        out_shape=[
            jax.ShapeDtypeStruct((t, d_sc), BF16),
            jax.ShapeDtypeStruct((nb, 2, d_sc), F32),
            jax.ShapeDtypeStruct((ts, d_sc), F32),
        ],
        scratch_shapes=[pltpu.VMEM((_CARRY + bm, bn), F32)],
        compiler_params=_params(2),
        name="sc_branch",
    )(h, w_in, w_in, w_in, conv_w, st, st)


def _xbc_kernel(x_ref, w_ref, cw_ref, cb_ref, s0_ref, s1_ref, s2_ref,
                a_ref, tail_ref, rs_ref, ubuf, *, n_p, bm, ts, tps):
    i = pl.program_id(1)

    @pl.when(i < n_p)
    def _():
        r = _dot(x_ref[...], w_ref[...])
        y = _conv_prompt(ubuf, cw_ref, r, bm=bm, taps=4, first_in_seq=(i % tps) == 0)
        a_ref[...] = _silu(y + cb_ref[...]).astype(a_ref.dtype)
        tail_ref[0] = r[bm - 3:bm, :]
        ubuf[0:_CARRY, :] = ubuf[bm:bm + _CARRY, :]

    @pl.when(i == n_p)
    def _():
        r = _dot(x_ref[0:ts, :], w_ref[...])
        rs_ref[...] = r
        y = s0_ref[...] * cw_ref[0:1, :]
        y = y + s1_ref[...] * cw_ref[1:2, :]
        y = y + s2_ref[...] * cw_ref[2:3, :]
        y = y + r * cw_ref[3:4, :]
        a_ref[0:ts, :] = _silu(y + cb_ref[...]).astype(a_ref.dtype)


def _xbc_branch(h, w_in, conv_w, conv_b, state, *, col0, n, tp, ts, seq, bm, bn):
    t, kdim = h.shape
    n_p = tp // bm
    tps = seq // bm
    nb = tp // seq
    nj = n // bn
    cb = col0 // bn
    assert col0 % bn == 0
    st = state.reshape(ts, 3 * n)
    return pl.pallas_call(
        functools.partial(_xbc_kernel, n_p=n_p, bm=bm, ts=ts, tps=tps),
        grid=(nj, n_p + 1),
        in_specs=[
            pl.BlockSpec((bm, kdim), lambda j, i: (i, 0)),
            pl.BlockSpec((kdim, bn), lambda j, i: (0, cb + j)),
            pl.BlockSpec((4, bn), lambda j, i: (0, j)),
            pl.BlockSpec((1, bn), lambda j, i: (0, j)),
            pl.BlockSpec((ts, bn), lambda j, i: (0, j)),
            pl.BlockSpec((ts, bn), lambda j, i: (0, nj + j)),
            pl.BlockSpec((ts, bn), lambda j, i: (0, 2 * nj + j)),
        ],
        out_specs=[
            pl.BlockSpec((bm, bn), lambda j, i: (i, j)),
            pl.BlockSpec((1, 3, bn), lambda j, i: (jnp.minimum(i, n_p - 1) // tps, 0, j)),
            pl.BlockSpec((ts, bn), lambda j, i: (0, j)),
        ],
        out_shape=[
            jax.ShapeDtypeStruct((t, n), BF16),
            jax.ShapeDtypeStruct((nb, 3, n), F32),
            jax.ShapeDtypeStruct((ts, n), F32),
        ],
        scratch_shapes=[pltpu.VMEM((_CARRY + bm, bn), F32)],
        compiler_params=_params(2),
        name="xbc_branch",
    )(h, w_in, conv_w, conv_b.reshape(1, n), st, st, st)


def _expand_heads(vals, rep2_ref):
    hi, lo = _split_hi_lo(vals)
    return _dot(jnp.concatenate([hi, lo], axis=1), rep2_ref[...])


def _gated_norm_store(y, z_act, nw_ref, o_ref, rows):
    yz = y * z_act
    ms = jnp.mean(yz * yz, axis=-1, keepdims=True)
    o_ref[0:rows, :] = (yz * lax.rsqrt(ms + EPS) * nw_ref[...]).astype(o_ref.dtype)


def _ssd_prompt_kernel(xs_ref, b_ref, c_ref, dt_ref, z_ref, alog_ref, dexp_ref, nw_ref, rep2_ref,
                       ym_ref, st_out_ref,
                       st_ref, dte_ref, eae_ref, dee_ref, acg_ref, act_ref, y_ref,
                       *, n_chunks, groups, hpg, hd, ns):
    q = SSD_CHUNK
    c = pl.program_id(1)
    gw = hpg * hd

    @pl.when(c == 0)
    def _():
        st_ref[...] = jnp.zeros(st_ref.shape, F32)

    dt = dt_ref[...]
    a = -jnp.exp(alog_ref[...])
    da = dt * a
    row = lax.broadcasted_iota(jnp.int32, (q, q), 0)
    col = lax.broadcasted_iota(jnp.int32, (q, q), 1)
    tril = (row >= col)
    hi, lo = _split_hi_lo(da)
    trilb = jnp.where(tril, 1.0, 0.0).astype(BF16)
    mid = (da - hi.astype(F32) - lo.astype(F32)).astype(BF16)
    acum = _dot(trilb, hi) + _dot(trilb, lo) + _dot(trilb, mid)
    a_last = acum[q - 1:q, :]
    dte_ref[...] = _expand_heads(dt, rep2_ref)
    eae_ref[...] = _expand_heads(jnp.exp(acum), rep2_ref)
    dee_ref[...] = _expand_heads(jnp.exp(a_last - acum), rep2_ref)
    act_ref[...] = acum.T
    acg_ref[0] = acum
    for g in range(1, groups):
        acg_ref[g] = pltpu.roll(acum, shift=acum.shape[1] - g * hpg, axis=1)

    lane = lax.broadcasted_iota(jnp.int32, (q, 2 * hd), 1)
    lo_half = lane < hd

    def group_body(g, carry):
        off = pl.multiple_of(g * gw, gw)
        noff = pl.multiple_of(g * ns, ns)
        xs_g = xs_ref[:, pl.ds(off, gw)].astype(F32)
        b_g = b_ref[:, pl.ds(noff, ns)]
        c_g = c_ref[:, pl.ds(noff, ns)]
        bt = b_g.astype(F32).T.astype(BF16)
        cb = _dot(c_g, bt)
        xdt = xs_g * dte_ref[:, pl.ds(off, gw)]
        xdt_b = xdt.astype(BF16)
        xd_b = (xdt * dee_ref[:, pl.ds(off, gw)]).astype(BF16)
        eae_g = eae_ref[:, pl.ds(off, gw)]
        st_g = st_ref[g]
        y_off = _dot(c_g, st_g.astype(BF16)) * eae_g
        st_ref[g] = st_g * eae_g[q - 1:q, :] + _dot(bt, xd_b)
        ac = acg_ref[g]
        hoff = pl.multiple_of(g * hpg, hpg)
        ac_t = act_ref[pl.ds(hoff, hpg), :]
        for pr in range(hpg // 2):
            ms = []
            for r in (2 * pr, 2 * pr + 1):
                seg = ac[:, r:r + 1] - ac_t[r:r + 1, :]
                ms.append((jnp.where(tril, jnp.exp(seg), 0.0) * cb).astype(BF16))
            lhs = jnp.concatenate(ms, axis=1)
            xp = xdt_b[:, pr * 2 * hd:(pr + 1) * 2 * hd]
            zero = jnp.zeros_like(xp)
            rhs = jnp.concatenate([jnp.where(lo_half, xp, zero), jnp.where(lo_half, zero, xp)], axis=0)
            y_pair = _dot(lhs, rhs) + y_off[:, pr * 2 * hd:(pr + 1) * 2 * hd]
            y_ref[:, pl.ds(pl.multiple_of(off + pr * 2 * hd, 2 * hd), 2 * hd)] = y_pair
        return carry

    lax.fori_loop(0, groups, group_body, 0)

    y = y_ref[...] + dexp_ref[...] * xs_ref[...].astype(F32)
    _gated_norm_store(y, z_ref[...].astype(F32), nw_ref, ym_ref, q)

    @pl.when(c == n_chunks - 1)
    def _():
        for g in range(groups):
            st_out_ref[0, g * gw:(g + 1) * gw, :] = st_ref[g].T


def _ssd_prompt(act, dt, zact, a_log, d_exp, norm_w, rep2, *, nb, seq, heads, hd, groups, ns):
    q = SSD_CHUNK
    d_inner = heads * hd
    hpg = heads // groups
    gw = hpg * hd
    n_chunks = seq // q
    gn = groups * ns
    assert heads == q and hd * 2 == q and ns == q and d_inner % gn == 0
    row = lambda b, c: b * n_chunks + c
    return pl.pallas_call(
        functools.partial(_ssd_prompt_kernel, n_chunks=n_chunks, groups=groups, hpg=hpg, hd=hd, ns=ns),
        grid=(nb, n_chunks),
        in_specs=[
            pl.BlockSpec((q, d_inner), lambda b, c: (row(b, c), 0)),
            pl.BlockSpec((q, gn), lambda b, c: (row(b, c), d_inner // gn)),
            pl.BlockSpec((q, gn), lambda b, c: (row(b, c), d_inner // gn + 1)),
            pl.BlockSpec((q, heads), lambda b, c: (row(b, c), 0)),
            pl.BlockSpec((q, d_inner), lambda b, c: (row(b, c), 0)),
            pl.BlockSpec((1, heads), lambda b, c: (0, 0)),
            pl.BlockSpec((1, d_inner), lambda b, c: (0, 0)),
            pl.BlockSpec((1, d_inner), lambda b, c: (0, 0)),
            pl.BlockSpec((2 * heads, d_inner), lambda b, c: (0, 0)),
        ],
        out_specs=[
            pl.BlockSpec((q, d_inner), lambda b, c: (row(b, c), 0)),
            pl.BlockSpec((1, d_inner, ns), lambda b, c: (b, 0, 0)),
        ],
        out_shape=[
            jax.ShapeDtypeStruct((nb * seq, d_inner), BF16),
            jax.ShapeDtypeStruct((nb, d_inner, ns), F32),
        ],
        scratch_shapes=[
            pltpu.VMEM((groups, ns, gw), F32),
            pltpu.VMEM((q, d_inner), F32),
            pltpu.VMEM((q, d_inner), F32),
            pltpu.VMEM((q, d_inner), F32),
            pltpu.VMEM((groups, q, heads), F32),
            pltpu.VMEM((heads, q), F32),
            pltpu.VMEM((q, d_inner), F32),
        ],
        compiler_params=_params(2),
        name="ssd_prompt",
    )(act, act, act, dt, zact, a_log.reshape(1, heads), d_exp, norm_w.reshape(1, d_inner), rep2)


def _ssd_sample_kernel(st_ref, xs_ref, b_ref, c_ref, dt_ref, z_ref, alog_ref, dexp_ref, nw_ref, rep2_ref,
                       st_out_ref, ym_ref, *, groups, hpg, hd, ns):
    gw = hpg * hd
    d_inner = groups * gw
    xs = xs_ref[0].astype(F32)
    dt = dt_ref[0]
    a = -jnp.exp(alog_ref[...])
    pad = jnp.zeros((6, dt.shape[1]), F32)
    both = _expand_heads(jnp.concatenate([dt, jnp.exp(dt * a), pad], axis=0), rep2_ref)
    dte = both[0:1, :]
    dae = both[1:2, :]
    xdt = xs * dte
    kr = 2 * groups
    grp_of_lane = lax.broadcasted_iota(jnp.int32, (kr, d_inner), 1) // gw
    krow = lax.broadcasted_iota(jnp.int32, (kr, d_inner), 0)
    ltf = (jnp.where(grp_of_lane == krow, jnp.broadcast_to(xdt, (kr, d_inner)), 0.0)
           + jnp.where(krow == groups, jnp.broadcast_to(dae, (kr, d_inner)), 0.0))
    lt_hi, lt_lo = _split_hi_lo(ltf)
    lt = jnp.concatenate([lt_hi, lt_lo], axis=0)
    bmat = b_ref[0].astype(F32)
    rrow = lax.broadcasted_iota(jnp.int32, (groups, 2 * ns), 0)
    rlane = lax.broadcasted_iota(jnp.int32, (groups, 2 * ns), 1)
    r_top = jnp.concatenate([bmat, jnp.zeros_like(bmat)], axis=1)
    r_bot = jnp.where((rrow == 0) & (rlane >= ns), 1.0, 0.0)
    rtf = jnp.concatenate([r_top, r_bot], axis=0)
    rt = jnp.concatenate([rtf, rtf], axis=0).astype(BF16)
    both2 = lax.dot_general(lt, rt, (((0,), (0,)), ((), ())), preferred_element_type=F32)
    new = st_ref[0] * both2[:, ns:] + both2[:, :ns]
    st_out_ref[0] = new
    yg = _dot_nt(c_ref[0], new.astype(BF16))
    own = (lax.broadcasted_iota(jnp.int32, (groups, d_inner), 1) // gw
           == lax.broadcasted_iota(jnp.int32, (groups, d_inner), 0))
    y = jnp.sum(jnp.where(own, yg, 0.0), axis=0, keepdims=True)
    y = y + dexp_ref[...] * xs
    yz = y * z_ref[0].astype(F32)
    ms = jnp.mean(yz * yz, axis=-1, keepdims=True)
    ym_ref[0] = (yz * lax.rsqrt(ms + EPS) * nw_ref[...]).astype(ym_ref.dtype)


def _ssd_sample(state, xs, bmat, cmat, dt, zact, a_log, d_exp, norm_w, rep2, *, heads, hd, groups, ns):
    ts = state.shape[0]
    d_inner = heads * hd
    hpg = heads // groups
    seq_spec = lambda shape: pl.BlockSpec((1,) + shape, lambda b: (b, 0, 0))
    const = lambda shape: pl.BlockSpec(shape, lambda b: (0, 0))
    return pl.pallas_call(
        functools.partial(_ssd_sample_kernel, groups=groups, hpg=hpg, hd=hd, ns=ns),
        grid=(ts,),
        in_specs=[
            seq_spec((d_inner, ns)), seq_spec((1, d_inner)), seq_spec((groups, ns)), seq_spec((groups, ns)),
            seq_spec((1, heads)), seq_spec((1, d_inner)),
            const((1, heads)), const((1, d_inner)), const((1, d_inner)), const((2 * heads, d_inner)),
        ],
        out_specs=[seq_spec((d_inner, ns)), seq_spec((1, d_inner))],
        out_shape=[
            jax.ShapeDtypeStruct((ts, d_inner, ns), F32),
            jax.ShapeDtypeStruct((ts, 1, d_inner), F32),
        ],
        compiler_params=_params(1),
        name="ssd_sample",
    )(state, xs, bmat, cmat, dt, zact, a_log.reshape(1, heads), d_exp, norm_w.reshape(1, d_inner), rep2)


def _merge_kernel(x_ref, w_ref, ga_ref, *rest, n_p, bm, ts, has_prev):
    prev_ref = rest[0] if has_prev else None
    o_ref = rest[-1]
    i = pl.program_id(1)

    def compute(rows):
        acc = _dot(x_ref[0:rows, :], w_ref[...]) * ga_ref[0:rows, :].astype(F32)
        if has_prev:
            acc = acc + prev_ref[0:rows, :].astype(F32)
        o_ref[0:rows, :] = acc.astype(o_ref.dtype)

    @pl.when(i < n_p)
    def _():
        compute(bm)

    @pl.when(i == n_p)
    def _():
        compute(ts)


def _merge(x, w, gates, gate_col0, prev, *, tp, ts, bm, bn, out_dtype, name):
    t, kdim = x.shape
    n = w.shape[1]
    n_p = tp // bm
    gcb = gate_col0 // bn
    assert gate_col0 % bn == 0 and n % bn == 0 and tp % bm == 0
    has_prev = prev is not None
    in_specs = [
        pl.BlockSpec((bm, kdim), lambda j, i: (i, 0)),
        pl.BlockSpec((kdim, bn), lambda j, i: (0, j)),
        pl.BlockSpec((bm, bn), lambda j, i: (i, gcb + j)),
    ]
    args = [x, w, gates]
    if has_prev:
        in_specs.append(pl.BlockSpec((bm, bn), lambda j, i: (i, j)))
        args.append(prev)
    return pl.pallas_call(
        functools.partial(_merge_kernel, n_p=n_p, bm=bm, ts=ts, has_prev=has_prev),
        grid=(n // bn, n_p + 1),
        in_specs=in_specs,
        out_specs=pl.BlockSpec((bm, bn), lambda j, i: (i, j)),
        out_shape=jax.ShapeDtypeStruct((t, n), out_dtype),
        compiler_params=_params(2),
        name=name,
    )(*args)


def _topk_rows(s, k):
    r = s.shape[0]
    rows = lax.broadcasted_iota(jnp.int32, s.shape, 0).astype(F32)
    vals, idxs = [], []
    for _ in range(k):
        m = jnp.max(s, axis=0, keepdims=True)
        idx = jnp.min(jnp.where(s == m, rows, float(r)), axis=0, keepdims=True)
        vals.append(m)
        idxs.append(idx)
        s = jnp.where(rows == idx, NEG_INF, s)
    return jnp.concatenate(vals, axis=0), jnp.concatenate(idxs, axis=0)


def _pick_rows(table, sel):
    k = table.shape[0]
    out = jnp.zeros_like(sel)
    for r in range(k):
        out = jnp.where(sel == float(r), jnp.broadcast_to(table[r:r + 1, :], sel.shape), out)
    return out


def _route_kernel(h_ref, wq_ref, k1_ref, k2_ref, ia_ref, ib_ref, gt_ref, s1_ref, s2_ref, *, qh, lanes):
    k = PEER_TOPK
    qv = _dot(h_ref[...], wq_ref[...])
    hp = lax.Precision.HIGHEST
    nt = (((1,), (1,)), ((), ()))
    s1_ref[...] = lax.dot_general(k1_ref[0], qv[:, :qh], nt, precision=hp, preferred_element_type=F32)
    s2_ref[...] = lax.dot_general(k2_ref[0], qv[:, qh:], nt, precision=hp, preferred_element_type=F32)

    def chunk(ci, carry):
        off = pl.multiple_of(ci * lanes, lanes)
        v1, i1 = _topk_rows(s1_ref[:, pl.ds(off, lanes)], k)
        v2, i2 = _topk_rows(s2_ref[:, pl.ds(off, lanes)], k)
        cand = jnp.concatenate([v1[i:i + 1, :] + v2 for i in range(k)], axis=0)
        sv, sp = _topk_rows(cand, k)
        e = jnp.exp(sv - sv[0:1, :])
        gt_ref[0, :, pl.ds(off, lanes)] = e / jnp.sum(e, axis=0, keepdims=True)
        sp_hi = jnp.floor(sp * (1.0 / k))
        ia_ref[0, :, pl.ds(off, lanes)] = _pick_rows(i1, sp_hi)
        ib_ref[0, :, pl.ds(off, lanes)] = _pick_rows(i2, sp - sp_hi * k)
        return carry

    lax.fori_loop(0, s1_ref.shape[1] // lanes, chunk, 0)


def _route(h2, wq, keys, *, tq):
    t, d = h2.shape
    _, heads, nkeys, qh = keys.shape
    k = PEER_TOPK
    out = jax.ShapeDtypeStruct((heads, k, t), F32)
    out_spec = pl.BlockSpec((1, k, tq), lambda i, hh: (hh, 0, i))
    return pl.pallas_call(
        functools.partial(_route_kernel, qh=qh, lanes=128),
        grid=(t // tq, heads),
        in_specs=[
            pl.BlockSpec((tq, d), lambda i, hh: (i, 0)),
            pl.BlockSpec((d, 2 * qh), lambda i, hh: (0, hh)),
            pl.BlockSpec((1, nkeys, qh), lambda i, hh: (hh, 0, 0)),
            pl.BlockSpec((1, nkeys, qh), lambda i, hh: (hh, 0, 0)),
        ],
        out_specs=[out_spec, out_spec, out_spec],
        out_shape=[out, out, out],
        scratch_shapes=[pltpu.VMEM((nkeys, tq), F32), pltpu.VMEM((nkeys, tq), F32)],
        compiler_params=_params(2),
        name="peer_route",
    )(h2, wq, keys[0], keys[1])


def _scatter_kernel(ia_ref, ib_ref, gt_ref, w_ref, ia_s, ib_s, gt_s, *, nkeys):
    tw = ia_ref.shape[1]
    ia_s[...] = ia_ref[...].T
    ib_s[...] = ib_ref[...].T
    gt_s[...] = gt_ref[...].T
    nsel = ia_ref.shape[0]
    key_id = lax.broadcasted_iota(jnp.int32, (nkeys, nsel), 0).astype(F32)

    def token(tk, carry):
        a_row = ia_s[pl.ds(tk, 1), :]
        b_row = ib_s[pl.ds(tk, 1), :]
        g_row = gt_s[pl.ds(tk, 1), :]
        at = jnp.where(key_id == a_row, jnp.broadcast_to(g_row, key_id.shape), 0.0).astype(BF16)
        bt = jnp.where(key_id == b_row, 1.0, 0.0).astype(BF16)
        w_ref[tk] = _dot_nt(at, bt).astype(w_ref.dtype)
        return carry

    lax.fori_loop(0, tw, token, 0)


def _scatter(ia, ib, gt, *, nkeys, tw):
    nsel, t = ia.shape
    spec = pl.BlockSpec((nsel, tw), lambda i: (0, i))
    return pl.pallas_call(
        functools.partial(_scatter_kernel, nkeys=nkeys),
        grid=(t // tw,),
        in_specs=[spec, spec, spec],
        out_specs=pl.BlockSpec((tw, nkeys, nkeys), lambda i: (i, 0, 0)),
        out_shape=jax.ShapeDtypeStruct((t, nkeys, nkeys), BF16),
        scratch_shapes=[pltpu.VMEM((tw, nsel), F32)] * 3,
        compiler_params=_params(1),
        name="peer_scatter",
    )(ia, ib, gt)


def _experts_kernel(h_ref, u_ref, v_ref, w_ref, o_ref):
    e = pl.program_id(1)
    s = _dot_nt(h_ref[...], u_ref[...])
    gelu = 0.5 * s * (1.0 + lax.erf(s * (2.0 ** -0.5)))
    act = gelu * w_ref[...].astype(F32)
    part = _dot(act.astype(BF16), v_ref[...])

    @pl.when(e == 0)
    def _():
        o_ref[...] = part

    @pl.when(e > 0)
    def _():
        o_ref[...] = o_ref[...] + part


def _experts(h2, u, v, w, *, tm, te):
    t, d = h2.shape
    ne = u.shape[0]
    return pl.pallas_call(
        _experts_kernel,
        grid=(t // tm, ne // te),
        in_specs=[
            pl.BlockSpec((tm, d), lambda i, e: (i, 0), pipeline_mode=pl.Buffered(1)),
            pl.BlockSpec((te, d), lambda i, e: (e, 0)),
            pl.BlockSpec((te, d), lambda i, e: (e, 0)),
            pl.BlockSpec((tm, te), lambda i, e: (i, e)),
        ],
        out_specs=pl.BlockSpec((tm, d), lambda i, e: (i, 0), pipeline_mode=pl.Buffered(1)),
        out_shape=jax.ShapeDtypeStruct((t, d), F32),
        compiler_params=_params(2),
        name="peer_experts",
    )(h2, u, v, w)


def _largest_divisor(n, cap, multiple):
    best = None
    for cand in range(multiple, min(n, cap) + 1, multiple):
        if n % cand == 0:
            best = cand
    assert best is not None, (n, cap, multiple)
    return best


def kernel(x_prompt, x_sample, state_shortconv, state_mamba_conv, state_ssm, ln1_w, w_in, sc_conv_w,
           sc_out_w, m_conv_w, m_conv_b, m_dt_bias, m_A_log, m_D, m_norm_w, m_out_w, w_o, ln2_w,
           peer_wq, peer_keys, peer_u, peer_v, final_norm_w):
    nb, seq, d = x_prompt.shape
    ts = x_sample.shape[0]
    assert x_sample.shape[1] == 1 and ln1_w.shape[0] == 1, "single layer, one new token per sample"
    tp = nb * seq
    t = tp + ts
    d_sc = state_shortconv.shape[-1]
    conv_dim = state_mamba_conv.shape[-1]
    _, _, heads, hd, ns = state_ssm.shape
    d_inner = heads * hd
    groups = (conv_dim - d_inner) // (2 * ns)
    gn = groups * ns
    nkeys = peer_keys.shape[3]
    qh = peer_keys.shape[4]
    c_z = 3 * d_sc
    c_xbc = c_z + d_inner
    c_dt = c_xbc + conv_dim
    c_gate = c_dt + heads

    bm = _largest_divisor(seq, 1024, 128)
    bm_small = _largest_divisor(seq, 512, 128)
    bt = _largest_divisor(seq, 256, 8)

    xp2 = x_prompt.reshape(tp, d)
    xs2 = x_sample.reshape(ts, d)
    w_in_b = w_in[0].astype(BF16)
    w_gate_b = w_in[0][:, c_gate:].astype(BF16)

    h = _norm_in(xp2, xs2, ln1_w[0], bt=bt)

    g, sc_tail_p, sc_u_s = _sc_branch(h, w_in_b, sc_conv_w[0], state_shortconv[0], d_sc=d_sc, tp=tp, ts=ts,
                                      seq=seq, bm=bm, bn=_largest_divisor(d_sc, 256, 128))
    act, mc_tail_p, mc_raw_s = _xbc_branch(h, w_in_b, m_conv_w[0], m_conv_b[0], state_mamba_conv[0],
                                           col0=c_xbc, n=conv_dim, tp=tp, ts=ts, seq=seq, bm=bm,
                                           bn=_largest_divisor(conv_dim, 512, 128))
    zact = _mm_act(h, w_in_b, col0=c_z, n=d_inner, tp=tp, ts=ts, bm=bm, bn=_largest_divisor(d_inner, 1024, 128),
                   act="silu", out_dtype=BF16, name="proj_z")
    dt = _mm_act(h, w_in_b, col0=c_dt, n=heads, tp=tp, ts=ts, bm=bm, bn=heads, act="softplus",
                 out_dtype=F32, bias=m_dt_bias[0], name="proj_dt")
    gates = _mm_act(h, w_gate_b, col0=0, n=2 * d, tp=tp, ts=ts, bm=bm, bn=_largest_divisor(d, 1024, 128),
                    act="sigmoid", out_dtype=BF16, name="proj_gates")

    d_exp = jnp.repeat(m_D[0], hd).reshape(1, d_inner)
    rep = (jnp.arange(d_inner)[None, :] // hd == jnp.arange(heads)[:, None]).astype(BF16)
    rep2 = jnp.concatenate([rep, rep], axis=0)
    ym_p, ssm_p = _ssd_prompt(act, dt, zact, m_A_log[0], d_exp, m_norm_w[0], rep2, nb=nb, seq=seq,
                              heads=heads, hd=hd, groups=groups, ns=ns)
    act_s = act[tp:]
    ssm_s, ym_s = _ssd_sample(
        state_ssm[0].reshape(ts, d_inner, ns),
        act_s[:, :d_inner].astype(F32).reshape(ts, 1, d_inner),
        act_s[:, d_inner:d_inner + gn].reshape(ts, groups, ns),
        act_s[:, d_inner + gn:].reshape(ts, groups, ns),
        dt[tp:].reshape(ts, 1, heads),
        zact[tp:].astype(F32).reshape(ts, 1, d_inner),
        m_A_log[0], d_exp, m_norm_w[0], rep2, heads=heads, hd=hd, groups=groups, ns=ns)
    ym = jnp.concatenate([ym_p, ym_s.reshape(ts, d_inner).astype(BF16)], axis=0)

    bn_o = _largest_divisor(d, 1024, 128)
    mix_a = _merge(g, sc_out_w[0].astype(BF16), gates, 0, None, tp=tp, ts=ts, bm=bm, bn=bn_o,
                   out_dtype=F32, name="merge_a")
    mixed = _merge(ym, m_out_w[0].astype(BF16), gates, d, mix_a, tp=tp, ts=ts, bm=bm_small,
                   bn=_largest_divisor(d, 512, 128), out_dtype=BF16, name="merge_b")
    x1 = _mm_act(mixed, w_o[0].astype(BF16), col0=0, n=d, tp=tp, ts=ts, bm=bm,
                 bn=_largest_divisor(d, 512, 128), act="none",
                 out_dtype=F32, resid=(xp2, xs2), name="proj_o")

    tok_tile = _largest_divisor(t, 640, 128)
    h2 = _norm_mid(x1, ln2_w[0], bt=_largest_divisor(t, 256, 8))
    ia, ib, gt = _route(h2, peer_wq[0].astype(BF16), peer_keys[0], tq=tok_tile)
    nsel = ia.shape[0] * ia.shape[1]
    w = _scatter(ia.reshape(nsel, t), ib.reshape(nsel, t), gt.reshape(nsel, t), nkeys=nkeys, tw=128)
    peer = _experts(h2, peer_u[0].astype(BF16), peer_v[0].astype(BF16), w.reshape(t, nkeys * nkeys),
                    tm=tok_tile, te=_largest_divisor(nkeys * nkeys, 512, 128))
    y_p, y_s = _norm_out(x1, peer, final_norm_w, tp=tp, ts=ts, bt=bt)

    new_sc_s = jnp.stack([state_shortconv[0][:, 1], sc_u_s], axis=1)
    new_mc_s = jnp.stack([state_mamba_conv[0][:, 1], state_mamba_conv[0][:, 2], mc_raw_s], axis=1)
    return (
        y_p.reshape(nb, seq, d),
        y_s.reshape(ts, 1, d),
        sc_tail_p[None],
        mc_tail_p[None],
        ssm_p.reshape(1, nb, heads, hd, ns),
        new_sc_s[None],
        new_mc_s[None],
        ssm_s.reshape(1, ts, heads, hd, ns),
    )
```

```python
import functools

import jax
import jax.numpy as jnp
from jax import lax
from jax.experimental import pallas as pl
from jax.experimental.pallas import tpu as pltpu

F32 = jnp.float32
BF16 = jnp.bfloat16
EPS = 1e-6
PEER_TOPK = 16
SSD_CHUNK = 128
VMEM_LIMIT_BYTES = 56 * 1024 * 1024
NEG_INF = float("-inf")


def _params(n_grid_axes):
    return pltpu.CompilerParams(
        dimension_semantics=("arbitrary",) * n_grid_axes,
        vmem_limit_bytes=VMEM_LIMIT_BYTES,
    )


def _dot(a, b):
    return jnp.dot(a, b, preferred_element_type=F32)


def _dot_nt(a, b):
    return lax.dot_general(a, b, (((1,), (1,)), ((), ())), preferred_element_type=F32)


def _silu(x):
    return x * jax.nn.sigmoid(x)


ROW_CHUNK = 256
EXPERT_ROW_CHUNK = 640


def _row_chunks(rows):
    rc = ROW_CHUNK if rows % ROW_CHUNK == 0 else rows
    return [(r0, r0 + rc) for r0 in range(0, rows, rc)]


def _split_hi_lo(x):
    hi = x.astype(BF16)
    lo = (x - hi.astype(F32)).astype(BF16)
    return hi, lo


def _rms(x, w):
    ms = jnp.mean(x * x, axis=-1, keepdims=True)
    return x * lax.rsqrt(ms + EPS) * w


def _norm_in_kernel(xp_ref, xs_ref, w_ref, o_ref, *, n_p, ts):
    i = pl.program_id(0)

    @pl.when(i < n_p)
    def _():
        o_ref[...] = _rms(xp_ref[...], w_ref[...]).astype(o_ref.dtype)

    @pl.when(i == n_p)
    def _():
        o_ref[0:ts, :] = _rms(xs_ref[...], w_ref[...]).astype(o_ref.dtype)


def _norm_in(xp, xs, w, *, bt):
    tp, d = xp.shape
    ts = xs.shape[0]
    n_p = tp // bt
    return pl.pallas_call(
        functools.partial(_norm_in_kernel, n_p=n_p, ts=ts),
        grid=(n_p + 1,),
        in_specs=[
            pl.BlockSpec((bt, d), lambda i: (jnp.minimum(i, n_p - 1), 0)),
            pl.BlockSpec((ts, d), lambda i: (0, 0)),
            pl.BlockSpec((1, d), lambda i: (0, 0)),
        ],
        out_specs=pl.BlockSpec((bt, d), lambda i: (i, 0)),
        out_shape=jax.ShapeDtypeStruct((tp + ts, d), BF16),
        compiler_params=_params(1),
        name="norm_in",
    )(xp, xs, w.reshape(1, d))


def _norm_mid_kernel(x_ref, w_ref, o_ref):
    o_ref[...] = _rms(x_ref[...], w_ref[...]).astype(o_ref.dtype)


def _norm_mid(x, w, *, bt):
    t, d = x.shape
    return pl.pallas_call(
        _norm_mid_kernel,
        grid=(t // bt,),
        in_specs=[pl.BlockSpec((bt, d), lambda i: (i, 0)), pl.BlockSpec((1, d), lambda i: (0, 0))],
        out_specs=pl.BlockSpec((bt, d), lambda i: (i, 0)),
        out_shape=jax.ShapeDtypeStruct((t, d), BF16),
        compiler_params=_params(1),
        name="norm_mid",
    )(x, w.reshape(1, d))


def _norm_out_kernel(x_ref, y_ref, w_ref, op_ref, os_ref, *, n_p, ts):
    i = pl.program_id(0)

    @pl.when(i < n_p)
    def _():
        op_ref[...] = _rms(x_ref[...] + y_ref[...], w_ref[...])

    @pl.when(i == n_p)
    def _():
        os_ref[...] = _rms(x_ref[0:ts, :] + y_ref[0:ts, :], w_ref[...])


def _norm_out(x, y, w, *, tp, ts, bt):
    d = x.shape[1]
    n_p = tp // bt
    return pl.pallas_call(
        functools.partial(_norm_out_kernel, n_p=n_p, ts=ts),
        grid=(n_p + 1,),
        in_specs=[
            pl.BlockSpec((bt, d), lambda i: (i, 0)),
            pl.BlockSpec((bt, d), lambda i: (i, 0)),
            pl.BlockSpec((1, d), lambda i: (0, 0)),
        ],
        out_specs=[
            pl.BlockSpec((bt, d), lambda i: (jnp.minimum(i, n_p - 1), 0)),
            pl.BlockSpec((ts, d), lambda i: (0, 0)),
        ],
        out_shape=[jax.ShapeDtypeStruct((tp, d), F32), jax.ShapeDtypeStruct((ts, d), F32)],
        compiler_params=_params(1),
        name="norm_out",
    )(x, y, w.reshape(1, d))


def _apply_act(acc, act):
    if act == "silu":
        return _silu(acc)
    if act == "sigmoid":
        return jax.nn.sigmoid(acc)
    if act == "softplus":
        return jax.nn.softplus(acc)
    assert act == "none"
    return acc


def _mm_act_kernel(*refs, n_p, bm, ts, act, has_bias, has_resid):
    x_ref, w_ref = refs[0], refs[1]
    k = 2
    b_ref = None
    if has_bias:
        b_ref = refs[k]
        k += 1
    rp_ref = rs_ref = None
    if has_resid:
        rp_ref, rs_ref = refs[k], refs[k + 1]
        k += 2
    o_ref = refs[k]
    i = pl.program_id(1)

    def compute(rows, r_ref):
        for r0, r1 in _row_chunks(rows):
            acc = _dot(x_ref[r0:r1, :], w_ref[...])
            if has_bias:
                acc = acc + b_ref[...]
            acc = _apply_act(acc, act)
            if has_resid:
                acc = acc + r_ref[r0:r1, :]
            o_ref[r0:r1, :] = acc.astype(o_ref.dtype)

    @pl.when(i < n_p)
    def _():
        compute(bm, rp_ref)

    @pl.when(i == n_p)
    def _():
        compute(ts, rs_ref)


def _mm_act(x, w, *, col0, n, tp, ts, bm, bn, act, out_dtype, bias=None, resid=None, name):
    t, kdim = x.shape
    n_p = tp // bm
    assert tp % bm == 0 and ts <= bm and n % bn == 0 and col0 % bn == 0
    cb = col0 // bn
    in_specs = [
        pl.BlockSpec((bm, kdim), lambda j, i: (i, 0)),
        pl.BlockSpec((kdim, bn), lambda j, i: (0, cb + j)),
    ]
    args = [x, w]
    if bias is not None:
        in_specs.append(pl.BlockSpec((1, bn), lambda j, i: (0, j)))
        args.append(bias.reshape(1, n))
    if resid is not None:
        in_specs.append(pl.BlockSpec((bm, bn), lambda j, i: (jnp.minimum(i, n_p - 1), j)))
        in_specs.append(pl.BlockSpec((ts, bn), lambda j, i: (0, j)))
        args.extend(resid)
    return pl.pallas_call(
        functools.partial(_mm_act_kernel, n_p=n_p, bm=bm, ts=ts, act=act,
                          has_bias=bias is not None, has_resid=resid is not None),
        grid=(n // bn, n_p + 1),
        in_specs=in_specs,
        out_specs=pl.BlockSpec((bm, bn), lambda j, i: (i, j)),
        out_shape=jax.ShapeDtypeStruct((t, n), out_dtype),
        compiler_params=_params(2),
        name=name,
    )(*args)


_CARRY = 8


def _conv_reset(ubuf, first_in_seq):
    @pl.when(first_in_seq)
    def _():
        ubuf[0:_CARRY, :] = jnp.zeros((_CARRY, ubuf.shape[1]), F32)


def _conv_rows(ubuf, cw_ref, u, r0, *, taps):
    rc = u.shape[0]
    ubuf[_CARRY + r0:_CARRY + r0 + rc, :] = u
    base = _CARRY + r0 - (taps - 1)
    y = ubuf[base:base + rc, :] * cw_ref[0:1, :]
    for k in range(1, taps - 1):
        y = y + ubuf[base + k:base + k + rc, :] * cw_ref[k:k + 1, :]
    y = y + u * cw_ref[taps - 1:taps, :]
    return y


def _sc_kernel(x_ref, wb_ref, wc_ref, wx_ref, cw_ref, s0_ref, s1_ref,
               g_ref, tail_ref, us_ref, ubuf, wcat, *, n_p, bm, ts, tps):
    i = pl.program_id(1)
    bn = wb_ref.shape[1]

    @pl.when(i == 0)
    def _():
        wcat[:, 0:bn] = wb_ref[...]
        wcat[:, bn:2 * bn] = wc_ref[...]
        wcat[:, 2 * bn:3 * bn] = wx_ref[...]

    @pl.when(i < n_p)
    def _():
        _conv_reset(ubuf, (i % tps) == 0)
        for r0, r1 in _row_chunks(bm):
            p = _dot(x_ref[r0:r1, :], wcat[...])
            u = p[:, bn:2 * bn] * p[:, 2 * bn:3 * bn]
            y = _conv_rows(ubuf, cw_ref, u, r0, taps=3)
            g_ref[r0:r1, :] = (p[:, 0:bn] * y).astype(g_ref.dtype)
        tail_ref[0] = ubuf[_CARRY + bm - 2:_CARRY + bm, :]
        ubuf[0:_CARRY, :] = ubuf[bm:bm + _CARRY, :]

    @pl.when(i == n_p)
    def _():
        p = _dot(x_ref[0:ts, :], wcat[...])
        u = p[:, bn:2 * bn] * p[:, 2 * bn:3 * bn]
        us_ref[...] = u
        y = s0_ref[...] * cw_ref[0:1, :]
        y = y + s1_ref[...] * cw_ref[1:2, :]
        y = y + u * cw_ref[2:3, :]
        g_ref[0:ts, :] = (p[:, 0:bn] * y).astype(g_ref.dtype)


def _sc_branch(h, w_in, conv_w, state, *, d_sc, tp, ts, seq, bm, bn):
    t, kdim = h.shape
    n_p = tp // bm
    tps = seq // bm
    nb = tp // seq
    nj = d_sc // bn
    st = state.reshape(ts, 2 * d_sc)
    w_spec = lambda off: pl.BlockSpec((kdim, bn), lambda j, i: (0, off + j))
    return pl.pallas_call(
        functools.partial(_sc_kernel, n_p=n_p, bm=bm, ts=ts, tps=tps),
        grid=(nj, n_p + 1),
        in_specs=[
            pl.BlockSpec((bm, kdim), lambda j, i: (i, 0)),
            w_spec(0), w_spec(nj), w_spec(2 * nj),
            pl.BlockSpec((3, bn), lambda j, i: (0, j)),
            pl.BlockSpec((ts, bn), lambda j, i: (0, j)),
            pl.BlockSpec((ts, bn), lambda j, i: (0, nj + j)),
        ],
        out_specs=[
            pl.BlockSpec((bm, bn), lambda j, i: (i, j)),
            pl.BlockSpec((1, 2, bn), lambda j, i: (jnp.minimum(i, n_p - 1) // tps, 0, j)),
            pl.BlockSpec((ts, bn), lambda j, i: (0, j)),
        ],
        out_shape=[
            jax.ShapeDtypeStruct((t, d_sc), BF16),
            jax.ShapeDtypeStruct((nb, 2, d_sc), F32),
            jax.ShapeDtypeStruct((ts, d_sc), F32),
        ],
        scratch_shapes=[pltpu.VMEM((_CARRY + bm, bn), F32), pltpu.VMEM((kdim, 3 * bn), BF16)],
        compiler_params=_params(2),
        name="sc_branch",
    )(h, w_in, w_in, w_in, conv_w, st, st)


def _xbc_kernel(x_ref, w_ref, cw_ref, cb_ref, s0_ref, s1_ref, s2_ref,
                a_ref, tail_ref, rs_ref, ubuf, *, n_p, bm, ts, tps):
    i = pl.program_id(1)

    @pl.when(i < n_p)
    def _():
        _conv_reset(ubuf, (i % tps) == 0)
        for r0, r1 in _row_chunks(bm):
            r = _dot(x_ref[r0:r1, :], w_ref[...])
            y = _conv_rows(ubuf, cw_ref, r, r0, taps=4)
            a_ref[r0:r1, :] = _silu(y + cb_ref[...]).astype(a_ref.dtype)
        tail_ref[0] = ubuf[_CARRY + bm - 3:_CARRY + bm, :]
        ubuf[0:_CARRY, :] = ubuf[bm:bm + _CARRY, :]

    @pl.when(i == n_p)
    def _():
        r = _dot(x_ref[0:ts, :], w_ref[...])
        rs_ref[...] = r
        y = s0_ref[...] * cw_ref[0:1, :]
        y = y + s1_ref[...] * cw_ref[1:2, :]
        y = y + s2_ref[...] * cw_ref[2:3, :]
        y = y + r * cw_ref[3:4, :]
        a_ref[0:ts, :] = _silu(y + cb_ref[...]).astype(a_ref.dtype)


def _xbc_branch(h, w_in, conv_w, conv_b, state, *, col0, n, tp, ts, seq, bm, bn):
    t, kdim = h.shape
    n_p = tp // bm
    tps = seq // bm
    nb = tp // seq
    nj = n // bn
    cb = col0 // bn
    assert col0 % bn == 0
    st = state.reshape(ts, 3 * n)
    return pl.pallas_call(
        functools.partial(_xbc_kernel, n_p=n_p, bm=bm, ts=ts, tps=tps),
        grid=(nj, n_p + 1),
        in_specs=[
            pl.BlockSpec((bm, kdim), lambda j, i: (i, 0)),
            pl.BlockSpec((kdim, bn), lambda j, i: (0, cb + j)),
            pl.BlockSpec((4, bn), lambda j, i: (0, j)),
            pl.BlockSpec((1, bn), lambda j, i: (0, j)),
            pl.BlockSpec((ts, bn), lambda j, i: (0, j)),
            pl.BlockSpec((ts, bn), lambda j, i: (0, nj + j)),
            pl.BlockSpec((ts, bn), lambda j, i: (0, 2 * nj + j)),
        ],
        out_specs=[
            pl.BlockSpec((bm, bn), lambda j, i: (i, j)),
            pl.BlockSpec((1, 3, bn), lambda j, i: (jnp.minimum(i, n_p - 1) // tps, 0, j)),
            pl.BlockSpec((ts, bn), lambda j, i: (0, j)),
        ],
        out_shape=[
            jax.ShapeDtypeStruct((t, n), BF16),
            jax.ShapeDtypeStruct((nb, 3, n), F32),
            jax.ShapeDtypeStruct((ts, n), F32),
        ],
        scratch_shapes=[pltpu.VMEM((_CARRY + bm, bn), F32)],
        compiler_params=_params(2),
        name="xbc_branch",
    )(h, w_in, conv_w, conv_b.reshape(1, n), st, st, st)


def _expand_heads(vals, rep2_ref):
    hi, lo = _split_hi_lo(vals)
    return _dot(jnp.concatenate([hi, lo], axis=1), rep2_ref[...])


def _gated_norm_store(y, z_act, nw_ref, o_ref, rows):
    yz = y * z_act
    ms = jnp.mean(yz * yz, axis=-1, keepdims=True)
    o_ref[0:rows, :] = (yz * lax.rsqrt(ms + EPS) * nw_ref[...]).astype(o_ref.dtype)


def _ssd_prompt_kernel(xs_ref, b_ref, c_ref, dt_ref, z_ref, alog_ref, dexp_ref, nw_ref, rep2_ref,
                       ym_ref, st_out_ref,
                       st_ref, dte_ref, eae_ref, dee_ref, acg_ref, act_ref, y_ref,
                       *, n_chunks, groups, hpg, hd, ns):
    q = SSD_CHUNK
    c = pl.program_id(1)
    gw = hpg * hd

    @pl.when(c == 0)
    def _():
        st_ref[...] = jnp.zeros(st_ref.shape, F32)

    dt = dt_ref[...]
    a = -jnp.exp(alog_ref[...])
    da = dt * a
    row = lax.broadcasted_iota(jnp.int32, (q, q), 0)
    col = lax.broadcasted_iota(jnp.int32, (q, q), 1)
    tril = (row >= col)
    hi, lo = _split_hi_lo(da)
    trilb = jnp.where(tril, 1.0, 0.0).astype(BF16)
    mid = (da - hi.astype(F32) - lo.astype(F32)).astype(BF16)
    acum = _dot(trilb, hi) + _dot(trilb, lo) + _dot(trilb, mid)
    a_last = acum[q - 1:q, :]
    dte_ref[...] = _expand_heads(dt, rep2_ref)
    eae_ref[...] = _expand_heads(jnp.exp(acum), rep2_ref)
    dee_ref[...] = _expand_heads(jnp.exp(a_last - acum), rep2_ref)
    act_ref[...] = acum.T
    acg_ref[0] = acum
    for g in range(1, groups):
        acg_ref[g] = pltpu.roll(acum, shift=acum.shape[1] - g * hpg, axis=1)

    lane = lax.broadcasted_iota(jnp.int32, (q, 2 * hd), 1)
    lo_half = lane < hd

    def group_body(g, carry):
        off = pl.multiple_of(g * gw, gw)
        noff = pl.multiple_of(g * ns, ns)
        xs_g = xs_ref[:, pl.ds(off, gw)].astype(F32)
        b_g = b_ref[:, pl.ds(noff, ns)]
        c_g = c_ref[:, pl.ds(noff, ns)]
        bt = b_g.astype(F32).T.astype(BF16)
        cb = _dot(c_g, bt)
        xdt = xs_g * dte_ref[:, pl.ds(off, gw)]
        xdt_b = xdt.astype(BF16)
        xd_b = (xdt * dee_ref[:, pl.ds(off, gw)]).astype(BF16)
        eae_g = eae_ref[:, pl.ds(off, gw)]
        st_g = st_ref[g]
        y_off = _dot(c_g, st_g.astype(BF16)) * eae_g
        st_ref[g] = st_g * eae_g[q - 1:q, :] + _dot(bt, xd_b)
        ac = acg_ref[g]
        hoff = pl.multiple_of(g * hpg, hpg)
        ac_t = act_ref[pl.ds(hoff, hpg), :]
        for pr in range(hpg // 2):
            ms = []
            for r in (2 * pr, 2 * pr + 1):
                seg = ac[:, r:r + 1] - ac_t[r:r + 1, :]
                ms.append((jnp.where(tril, jnp.exp(seg), 0.0) * cb).astype(BF16))
            lhs = jnp.concatenate(ms, axis=1)
            xp = xdt_b[:, pr * 2 * hd:(pr + 1) * 2 * hd]
            zero = jnp.zeros_like(xp)
            rhs = jnp.concatenate([jnp.where(lo_half, xp, zero), jnp.where(lo_half, zero, xp)], axis=0)
            y_pair = _dot(lhs, rhs) + y_off[:, pr * 2 * hd:(pr + 1) * 2 * hd]
            y_ref[:, pl.ds(pl.multiple_of(off + pr * 2 * hd, 2 * hd), 2 * hd)] = y_pair
        return carry

    lax.fori_loop(0, groups, group_body, 0)

    y = y_ref[...] + dexp_ref[...] * xs_ref[...].astype(F32)
    _gated_norm_store(y, z_ref[...].astype(F32), nw_ref, ym_ref, q)

    @pl.when(c == n_chunks - 1)
    def _():
        for g in range(groups):
            st_out_ref[0, g * gw:(g + 1) * gw, :] = st_ref[g].T


def _ssd_prompt(act, dt, zact, a_log, d_exp, norm_w, rep2, *, nb, seq, heads, hd, groups, ns):
    q = SSD_CHUNK
    d_inner = heads * hd
    hpg = heads // groups
    gw = hpg * hd
    n_chunks = seq // q
    gn = groups * ns
    assert heads == q and hd * 2 == q and ns == q and d_inner % gn == 0
    row = lambda b, c: b * n_chunks + c
    return pl.pallas_call(
        functools.partial(_ssd_prompt_kernel, n_chunks=n_chunks, groups=groups, hpg=hpg, hd=hd, ns=ns),
        grid=(nb, n_chunks),
        in_specs=[
            pl.BlockSpec((q, d_inner), lambda b, c: (row(b, c), 0)),
            pl.BlockSpec((q, gn), lambda b, c: (row(b, c), d_inner // gn)),
            pl.BlockSpec((q, gn), lambda b, c: (row(b, c), d_inner // gn + 1)),
            pl.BlockSpec((q, heads), lambda b, c: (row(b, c), 0)),
            pl.BlockSpec((q, d_inner), lambda b, c: (row(b, c), 0)),
            pl.BlockSpec((1, heads), lambda b, c: (0, 0)),
            pl.BlockSpec((1, d_inner), lambda b, c: (0, 0)),
            pl.BlockSpec((1, d_inner), lambda b, c: (0, 0)),
            pl.BlockSpec((2 * heads, d_inner), lambda b, c: (0, 0)),
        ],
        out_specs=[
            pl.BlockSpec((q, d_inner), lambda b, c: (row(b, c), 0)),
            pl.BlockSpec((1, d_inner, ns), lambda b, c: (b, 0, 0)),
        ],
        out_shape=[
            jax.ShapeDtypeStruct((nb * seq, d_inner), BF16),
            jax.ShapeDtypeStruct((nb, d_inner, ns), F32),
        ],
        scratch_shapes=[
            pltpu.VMEM((groups, ns, gw), F32),
            pltpu.VMEM((q, d_inner), F32),
            pltpu.VMEM((q, d_inner), F32),
            pltpu.VMEM((q, d_inner), F32),
            pltpu.VMEM((groups, q, heads), F32),
            pltpu.VMEM((heads, q), F32),
            pltpu.VMEM((q, d_inner), F32),
        ],
        compiler_params=_params(2),
        name="ssd_prompt",
    )(act, act, act, dt, zact, a_log.reshape(1, heads), d_exp, norm_w.reshape(1, d_inner), rep2)


def _ssd_sample_kernel(st_ref, xs_ref, b_ref, c_ref, dt_ref, z_ref, alog_ref, dexp_ref, nw_ref, rep2_ref,
                       st_out_ref, ym_ref, *, groups, hpg, hd, ns):
    gw = hpg * hd
    d_inner = groups * gw
    xs = xs_ref[0].astype(F32)
    dt = dt_ref[0]
    a = -jnp.exp(alog_ref[...])
    pad = jnp.zeros((6, dt.shape[1]), F32)
    both = _expand_heads(jnp.concatenate([dt, jnp.exp(dt * a), pad], axis=0), rep2_ref)
    dte = both[0:1, :]
    dae = both[1:2, :]
    xdt = xs * dte
    kr = 2 * groups
    grp_of_lane = lax.broadcasted_iota(jnp.int32, (kr, d_inner), 1) // gw
    krow = lax.broadcasted_iota(jnp.int32, (kr, d_inner), 0)
    ltf = (jnp.where(grp_of_lane == krow, jnp.broadcast_to(xdt, (kr, d_inner)), 0.0)
           + jnp.where(krow == groups, jnp.broadcast_to(dae, (kr, d_inner)), 0.0))
    lt_hi, lt_lo = _split_hi_lo(ltf)
    lt = jnp.concatenate([lt_hi, lt_lo], axis=0)
    bmat = b_ref[0].astype(F32)
    rrow = lax.broadcasted_iota(jnp.int32, (groups, 2 * ns), 0)
    rlane = lax.broadcasted_iota(jnp.int32, (groups, 2 * ns), 1)
    r_top = jnp.concatenate([bmat, jnp.zeros_like(bmat)], axis=1)
    r_bot = jnp.where((rrow == 0) & (rlane >= ns), 1.0, 0.0)
    rtf = jnp.concatenate([r_top, r_bot], axis=0)
    rt = jnp.concatenate([rtf, rtf], axis=0).astype(BF16)
    both2 = lax.dot_general(lt, rt, (((0,), (0,)), ((), ())), preferred_element_type=F32)
    new = st_ref[0] * both2[:, ns:] + both2[:, :ns]
    st_out_ref[0] = new
    yg = _dot_nt(c_ref[0], new.astype(BF16))
    own = (lax.broadcasted_iota(jnp.int32, (groups, d_inner), 1) // gw
           == lax.broadcasted_iota(jnp.int32, (groups, d_inner), 0))
    y = jnp.sum(jnp.where(own, yg, 0.0), axis=0, keepdims=True)
    y = y + dexp_ref[...] * xs
    yz = y * z_ref[0].astype(F32)
    ms = jnp.mean(yz * yz, axis=-1, keepdims=True)
    ym_ref[0] = (yz * lax.rsqrt(ms + EPS) * nw_ref[...]).astype(ym_ref.dtype)


def _ssd_sample(state, xs, bmat, cmat, dt, zact, a_log, d_exp, norm_w, rep2, *, heads, hd, groups, ns):
    ts = state.shape[0]
    d_inner = heads * hd
    hpg = heads // groups
    seq_spec = lambda shape: pl.BlockSpec((1,) + shape, lambda b: (b, 0, 0))
    const = lambda shape: pl.BlockSpec(shape, lambda b: (0, 0))
    return pl.pallas_call(
        functools.partial(_ssd_sample_kernel, groups=groups, hpg=hpg, hd=hd, ns=ns),
        grid=(ts,),
        in_specs=[
            seq_spec((d_inner, ns)), seq_spec((1, d_inner)), seq_spec((groups, ns)), seq_spec((groups, ns)),
            seq_spec((1, heads)), seq_spec((1, d_inner)),
            const((1, heads)), const((1, d_inner)), const((1, d_inner)), const((2 * heads, d_inner)),
        ],
        out_specs=[seq_spec((d_inner, ns)), seq_spec((1, d_inner))],
        out_shape=[
            jax.ShapeDtypeStruct((ts, d_inner, ns), F32),
            jax.ShapeDtypeStruct((ts, 1, d_inner), F32),
        ],
        compiler_params=_params(1),
        name="ssd_sample",
    )(state, xs, bmat, cmat, dt, zact, a_log.reshape(1, heads), d_exp, norm_w.reshape(1, d_inner), rep2)


def _merge_kernel(x_ref, w_ref, ga_ref, *rest, n_p, bm, ts, has_prev):
    prev_ref = rest[0] if has_prev else None
    o_ref = rest[-1]
    i = pl.program_id(1)

    def compute(rows):
        for r0, r1 in _row_chunks(rows):
            acc = _dot(x_ref[r0:r1, :], w_ref[...]) * ga_ref[r0:r1, :].astype(F32)
            if has_prev:
                acc = acc + prev_ref[r0:r1, :].astype(F32)
            o_ref[r0:r1, :] = acc.astype(o_ref.dtype)

    @pl.when(i < n_p)
    def _():
        compute(bm)

    @pl.when(i == n_p)
    def _():
        compute(ts)


def _merge(x, w, gates, gate_col0, prev, *, tp, ts, bm, bn, out_dtype, name):
    t, kdim = x.shape
    n = w.shape[1]
    n_p = tp // bm
    gcb = gate_col0 // bn
    assert gate_col0 % bn == 0 and n % bn == 0 and tp % bm == 0
    has_prev = prev is not None
    in_specs = [
        pl.BlockSpec((bm, kdim), lambda j, i: (i, 0)),
        pl.BlockSpec((kdim, bn), lambda j, i: (0, j)),
        pl.BlockSpec((bm, bn), lambda j, i: (i, gcb + j)),
    ]
    args = [x, w, gates]
    if has_prev:
        in_specs.append(pl.BlockSpec((bm, bn), lambda j, i: (i, j)))
        args.append(prev)
    return pl.pallas_call(
        functools.partial(_merge_kernel, n_p=n_p, bm=bm, ts=ts, has_prev=has_prev),
        grid=(n // bn, n_p + 1),
        in_specs=in_specs,
        out_specs=pl.BlockSpec((bm, bn), lambda j, i: (i, j)),
        out_shape=jax.ShapeDtypeStruct((t, n), out_dtype),
        compiler_params=_params(2),
        name=name,
    )(*args)


def _topk_rows(s, k):
    r = s.shape[0]
    rows = lax.broadcasted_iota(jnp.int32, s.shape, 0).astype(F32)
    vals, idxs = [], []
    for _ in range(k):
        m = jnp.max(s, axis=0, keepdims=True)
        idx = jnp.min(jnp.where(s == m, rows, float(r)), axis=0, keepdims=True)
        vals.append(m)
        idxs.append(idx)
        s = jnp.where(rows == idx, NEG_INF, s)
    return jnp.concatenate(vals, axis=0), jnp.concatenate(idxs, axis=0)


def _pick_rows(table, sel):
    k = table.shape[0]
    out = jnp.zeros_like(sel)
    for r in range(k):
        out = jnp.where(sel == float(r), jnp.broadcast_to(table[r:r + 1, :], sel.shape), out)
    return out


def _staircase_counts(k):
    return [k // (i + 1) for i in range(k)]


def _route_kernel(h_ref, wq_ref, k1_ref, k2_ref, ia_ref, ib_ref, gt_ref, s1_ref, s2_ref, cand_ref,
                  *, qh, lanes):
    k = PEER_TOPK
    qv = _dot(h_ref[...], wq_ref[...])
    hp = lax.Precision.HIGHEST
    nt = (((1,), (1,)), ((), ()))
    s1_ref[...] = lax.dot_general(k1_ref[0], qv[:, :qh], nt, precision=hp, preferred_element_type=F32)
    s2_ref[...] = lax.dot_general(k2_ref[0], qv[:, qh:], nt, precision=hp, preferred_element_type=F32)
    counts = _staircase_counts(k)
    starts = [sum(counts[:i]) for i in range(k)]
    n_cand = sum(counts)

    def chunk(ci, carry):
        off = pl.multiple_of(ci * lanes, lanes)
        v1, i1 = _topk_rows(s1_ref[:, pl.ds(off, lanes)], k)
        v2, i2 = _topk_rows(s2_ref[:, pl.ds(off, lanes)], k)
        for i in range(k):
            cand_ref[starts[i]:starts[i] + counts[i], :] = v1[i:i + 1, :] + v2[0:counts[i], :]
        cand_ref[n_cand:, :] = jnp.full((cand_ref.shape[0] - n_cand, lanes), NEG_INF, F32)
        sv, sp = _topk_rows(cand_ref[...], k)
        e = jnp.exp(sv - sv[0:1, :])
        gt_ref[0, :, pl.ds(off, lanes)] = e / jnp.sum(e, axis=0, keepdims=True)
        sel_i = jnp.zeros_like(sp)
        sel_start = jnp.zeros_like(sp)
        for i in range(1, k):
            ge = sp >= float(starts[i])
            sel_i = sel_i + jnp.where(ge, 1.0, 0.0)
            sel_start = sel_start + jnp.where(ge, float(counts[i - 1]), 0.0)
        ia_ref[0, :, pl.ds(off, lanes)] = _pick_rows(i1, sel_i)
        ib_ref[0, :, pl.ds(off, lanes)] = _pick_rows(i2, sp - sel_start)
        return carry

    lax.fori_loop(0, s1_ref.shape[1] // lanes, chunk, 0)


def _route(h2, wq, keys, *, tq):
    t, d = h2.shape
    _, heads, nkeys, qh = keys.shape
    k = PEER_TOPK
    out = jax.ShapeDtypeStruct((heads, k, t), F32)
    out_spec = pl.BlockSpec((1, k, tq), lambda i, hh: (hh, 0, i))
    return pl.pallas_call(
        functools.partial(_route_kernel, qh=qh, lanes=128),
        grid=(t // tq, heads),
        in_specs=[
            pl.BlockSpec((tq, d), lambda i, hh: (i, 0)),
            pl.BlockSpec((d, 2 * qh), lambda i, hh: (0, hh)),
            pl.BlockSpec((1, nkeys, qh), lambda i, hh: (hh, 0, 0)),
            pl.BlockSpec((1, nkeys, qh), lambda i, hh: (hh, 0, 0)),
        ],
        out_specs=[out_spec, out_spec, out_spec],
        out_shape=[out, out, out],
        scratch_shapes=[pltpu.VMEM((nkeys, tq), F32), pltpu.VMEM((nkeys, tq), F32),
                        pltpu.VMEM((-(-sum(_staircase_counts(k)) // 8) * 8, 128), F32)],
        compiler_params=_params(2),
        name="peer_route",
    )(h2, wq, keys[0], keys[1])


_SCATTER_UNROLL = 8
_SCATTER_PITCH = 136
_PACK_ROWS = 16


def _scatter_kernel(ia_ref, ib_ref, gt_ref, w_ref, ia_s, ib_s, gt_s, wsc, *, nkeys):
    tw = ia_ref.shape[1]
    ia_s[...] = ia_ref[...].T
    ib_s[...] = ib_ref[...].T
    gt_s[...] = gt_ref[...].T
    nsel = ia_ref.shape[0]
    key_id = lax.broadcasted_iota(jnp.int32, (nkeys, nsel), 0).astype(F32)

    def token_group(gi, carry):
        for k in range(_SCATTER_UNROLL):
            tk = gi * _SCATTER_UNROLL + k
            a_row = ia_s[pl.ds(tk, 1), :]
            b_row = ib_s[pl.ds(tk, 1), :]
            g_row = gt_s[pl.ds(tk, 1), :]
            at = jnp.where(key_id == a_row, jnp.broadcast_to(g_row, key_id.shape), 0.0).astype(BF16)
            bt = jnp.where(key_id == b_row, 1.0, 0.0).astype(BF16)
            wsc[pl.ds(pl.multiple_of(tk * _SCATTER_PITCH, 8), nkeys), :] = _dot_nt(at, bt)
        return carry

    lax.fori_loop(0, tw // _SCATTER_UNROLL, token_group, 0)

    def relayout(tg, carry):
        row0 = pl.multiple_of(tg * _PACK_ROWS, _PACK_ROWS)
        base = tg * (_PACK_ROWS * _SCATTER_PITCH)
        for a in range(nkeys):
            rows = wsc[pl.ds(base + a, _PACK_ROWS, stride=_SCATTER_PITCH), :]
            w_ref[pl.ds(row0, _PACK_ROWS), a * nkeys:(a + 1) * nkeys] = rows.astype(w_ref.dtype)
        return carry

    lax.fori_loop(0, tw // _PACK_ROWS, relayout, 0)


def _scatter(ia, ib, gt, *, nkeys, tw):
    nsel, t = ia.shape
    assert tw % _SCATTER_UNROLL == 0 and tw % _PACK_ROWS == 0
    spec = pl.BlockSpec((nsel, tw), lambda i: (0, i))
    return pl.pallas_call(
        functools.partial(_scatter_kernel, nkeys=nkeys),
        grid=(t // tw,),
        in_specs=[spec, spec, spec],
        out_specs=pl.BlockSpec((tw, nkeys * nkeys), lambda i: (i, 0)),
        out_shape=jax.ShapeDtypeStruct((t, nkeys * nkeys), BF16),
        scratch_shapes=[pltpu.VMEM((tw, nsel), F32)] * 3 + [pltpu.VMEM((tw * _SCATTER_PITCH, nkeys), F32)],
        compiler_params=_params(1),
        name="peer_scatter",
    )(ia, ib, gt)


def _experts_kernel(h_ref, u_ref, v_ref, w_ref, o_ref):
    e = pl.program_id(1)

    @pl.when(e == 0)
    def _():
        o_ref[...] = jnp.zeros(o_ref.shape, F32)

    tm = h_ref.shape[0]
    rc = EXPERT_ROW_CHUNK if tm % EXPERT_ROW_CHUNK == 0 else tm
    for r0 in range(0, tm, rc):
        s = _dot_nt(h_ref[r0:r0 + rc, :], u_ref[...])
        gelu = 0.5 * s * (1.0 + lax.erf(s * (2.0 ** -0.5)))
        act = gelu * w_ref[r0:r0 + rc, :].astype(F32)
        o_ref[r0:r0 + rc, :] += _dot(act.astype(BF16), v_ref[...])


def _experts(h2, u, v, w, *, tm, te):
    t, d = h2.shape
    ne = u.shape[0]
    return pl.pallas_call(
        _experts_kernel,
        grid=(t // tm, ne // te),
        in_specs=[
            pl.BlockSpec((tm, d), lambda i, e: (i, 0), pipeline_mode=pl.Buffered(1)),
            pl.BlockSpec((te, d), lambda i, e: (e, 0)),
            pl.BlockSpec((te, d), lambda i, e: (e, 0)),
            pl.BlockSpec((tm, te), lambda i, e: (i, e)),
        ],
        out_specs=pl.BlockSpec((tm, d), lambda i, e: (i, 0), pipeline_mode=pl.Buffered(1)),
        out_shape=jax.ShapeDtypeStruct((t, d), F32),
        compiler_params=_params(2),
        name="peer_experts",
    )(h2, u, v, w)


def _largest_divisor(n, cap, multiple):
    best = None
    for cand in range(multiple, min(n, cap) + 1, multiple):
        if n % cand == 0:
            best = cand
    assert best is not None, (n, cap, multiple)
    return best


def kernel(x_prompt, x_sample, state_shortconv, state_mamba_conv, state_ssm, ln1_w, w_in, sc_conv_w,
           sc_out_w, m_conv_w, m_conv_b, m_dt_bias, m_A_log, m_D, m_norm_w, m_out_w, w_o, ln2_w,
           peer_wq, peer_keys, peer_u, peer_v, final_norm_w):
    nb, seq, d = x_prompt.shape
    ts = x_sample.shape[0]
    assert x_sample.shape[1] == 1 and ln1_w.shape[0] == 1, "single layer, one new token per sample"
    tp = nb * seq
    t = tp + ts
    d_sc = state_shortconv.shape[-1]
    conv_dim = state_mamba_conv.shape[-1]
    _, _, heads, hd, ns = state_ssm.shape
    d_inner = heads * hd
    groups = (conv_dim - d_inner) // (2 * ns)
    gn = groups * ns
    nkeys = peer_keys.shape[3]
    qh = peer_keys.shape[4]
    c_z = 3 * d_sc
    c_xbc = c_z + d_inner
    c_dt = c_xbc + conv_dim
    c_gate = c_dt + heads

    bm = _largest_divisor(seq, 1024, 128)
    bm_small = _largest_divisor(seq, 512, 128)
    bt = _largest_divisor(seq, 256, 8)

    xp2 = x_prompt.reshape(tp, d)
    xs2 = x_sample.reshape(ts, d)
    w_in_b = w_in[0].astype(BF16)
    w_gate_b = w_in[0][:, c_gate:].astype(BF16)

    h = _norm_in(xp2, xs2, ln1_w[0], bt=bt)

    g, sc_tail_p, sc_u_s = _sc_branch(h, w_in_b, sc_conv_w[0], state_shortconv[0], d_sc=d_sc, tp=tp, ts=ts,
                                      seq=seq, bm=bm, bn=_largest_divisor(d_sc, 256, 128))
    act, mc_tail_p, mc_raw_s = _xbc_branch(h, w_in_b, m_conv_w[0], m_conv_b[0], state_mamba_conv[0],
                                           col0=c_xbc, n=conv_dim, tp=tp, ts=ts, seq=seq, bm=bm,
                                           bn=_largest_divisor(conv_dim, 512, 128))
    zact = _mm_act(h, w_in_b, col0=c_z, n=d_inner, tp=tp, ts=ts, bm=bm, bn=_largest_divisor(d_inner, 1024, 128),
                   act="silu", out_dtype=BF16, name="proj_z")
    dt = _mm_act(h, w_in_b, col0=c_dt, n=heads, tp=tp, ts=ts, bm=bm, bn=heads, act="softplus",
                 out_dtype=F32, bias=m_dt_bias[0], name="proj_dt")
    gates = _mm_act(h, w_gate_b, col0=0, n=2 * d, tp=tp, ts=ts, bm=bm, bn=_largest_divisor(d, 1024, 128),
                    act="sigmoid", out_dtype=BF16, name="proj_gates")

    d_exp = jnp.repeat(m_D[0], hd).reshape(1, d_inner)
    rep = (jnp.arange(d_inner)[None, :] // hd == jnp.arange(heads)[:, None]).astype(BF16)
    rep2 = jnp.concatenate([rep, rep], axis=0)
    ym_p, ssm_p = _ssd_prompt(act, dt, zact, m_A_log[0], d_exp, m_norm_w[0], rep2, nb=nb, seq=seq,
                              heads=heads, hd=hd, groups=groups, ns=ns)
    act_s = act[tp:]
    ssm_s, ym_s = _ssd_sample(
        state_ssm[0].reshape(ts, d_inner, ns),
        act_s[:, :d_inner].astype(F32).reshape(ts, 1, d_inner),
        act_s[:, d_inner:d_inner + gn].reshape(ts, groups, ns),
        act_s[:, d_inner + gn:].reshape(ts, groups, ns),
        dt[tp:].reshape(ts, 1, heads),
        zact[tp:].astype(F32).reshape(ts, 1, d_inner),
        m_A_log[0], d_exp, m_norm_w[0], rep2, heads=heads, hd=hd, groups=groups, ns=ns)
    ym = jnp.concatenate([ym_p, ym_s.reshape(ts, d_inner).astype(BF16)], axis=0)

    bn_o = _largest_divisor(d, 1024, 128)
    mix_a = _merge(g, sc_out_w[0].astype(BF16), gates, 0, None, tp=tp, ts=ts, bm=bm, bn=bn_o,
                   out_dtype=F32, name="merge_a")
    mixed = _merge(ym, m_out_w[0].astype(BF16), gates, d, mix_a, tp=tp, ts=ts, bm=bm_small,
                   bn=_largest_divisor(d, 512, 128), out_dtype=BF16, name="merge_b")
    x1 = _mm_act(mixed, w_o[0].astype(BF16), col0=0, n=d, tp=tp, ts=ts, bm=bm,
                 bn=_largest_divisor(d, 512, 128), act="none",
                 out_dtype=F32, resid=(xp2, xs2), name="proj_o")

    tok_tile = _largest_divisor(t, 640, 128)
    h2 = _norm_mid(x1, ln2_w[0], bt=_largest_divisor(t, 256, 8))
    ia, ib, gt = _route(h2, peer_wq[0].astype(BF16), peer_keys[0], tq=tok_tile)
    nsel = ia.shape[0] * ia.shape[1]
    w = _scatter(ia.reshape(nsel, t), ib.reshape(nsel, t), gt.reshape(nsel, t), nkeys=nkeys, tw=128)
    peer = _experts(h2, peer_u[0].astype(BF16), peer_v[0].astype(BF16), w,
                    tm=tok_tile, te=_largest_divisor(nkeys * nkeys, 512, 128))
    y_p, y_s = _norm_out(x1, peer, final_norm_w, tp=tp, ts=ts, bt=bt)

    new_sc_s = jnp.stack([state_shortconv[0][:, 1], sc_u_s], axis=1)
    new_mc_s = jnp.stack([state_mamba_conv[0][:, 1], state_mamba_conv[0][:, 2], mc_raw_s], axis=1)
    return (
        y_p.reshape(nb, seq, d),
        y_s.reshape(ts, 1, d),
        sc_tail_p[None],
        mc_tail_p[None],
        ssm_p.reshape(1, nb, heads, hd, ns),
        new_sc_s[None],
        new_mc_s[None],
        ssm_s.reshape(1, ts, heads, hd, ns),
    )
```

```python
import functools

import jax
import jax.numpy as jnp
from jax import lax
from jax.experimental import pallas as pl
from jax.experimental.pallas import tpu as pltpu

F32 = jnp.float32
BF16 = jnp.bfloat16
EPS = 1e-6
PEER_TOPK = 16
SSD_CHUNK = 128
VMEM_LIMIT_BYTES = 56 * 1024 * 1024
NEG_INF = float("-inf")
LANES = 128


def _params(n_grid_axes):
    return pltpu.CompilerParams(
        dimension_semantics=("arbitrary",) * n_grid_axes,
        vmem_limit_bytes=VMEM_LIMIT_BYTES,
    )


def _dot(a, b):
    return jnp.dot(a, b, preferred_element_type=F32)


def _dot_nt(a, b):
    return lax.dot_general(a, b, (((1,), (1,)), ((), ())), preferred_element_type=F32)


def _silu(x):
    return x * jax.nn.sigmoid(x)


ROW_CHUNK = 256
EXPERT_ROW_CHUNK = 640


def _row_chunks(rows):
    rc = ROW_CHUNK if rows % ROW_CHUNK == 0 else rows
    return [(r0, r0 + rc) for r0 in range(0, rows, rc)]


def _split_hi_lo(x):
    hi = x.astype(BF16)
    lo = (x - hi.astype(F32)).astype(BF16)
    return hi, lo


def _rms(x, w):
    ms = jnp.mean(x * x, axis=-1, keepdims=True)
    return x * lax.rsqrt(ms + EPS) * w


def _norm_in_kernel(xp_ref, xs_ref, w_ref, o_ref, *, n_p, ts):
    i = pl.program_id(0)

    @pl.when(i < n_p)
    def _():
        o_ref[...] = _rms(xp_ref[...], w_ref[...]).astype(o_ref.dtype)

    @pl.when(i == n_p)
    def _():
        o_ref[0:ts, :] = _rms(xs_ref[...], w_ref[...]).astype(o_ref.dtype)


def _norm_in(xp, xs, w, *, bt):
    tp, d = xp.shape
    ts = xs.shape[0]
    n_p = tp // bt
    return pl.pallas_call(
        functools.partial(_norm_in_kernel, n_p=n_p, ts=ts),
        grid=(n_p + 1,),
        in_specs=[
            pl.BlockSpec((bt, d), lambda i: (jnp.minimum(i, n_p - 1), 0)),
            pl.BlockSpec((ts, d), lambda i: (0, 0)),
            pl.BlockSpec((1, d), lambda i: (0, 0)),
        ],
        out_specs=pl.BlockSpec((bt, d), lambda i: (i, 0)),
        out_shape=jax.ShapeDtypeStruct((tp + ts, d), BF16),
        compiler_params=_params(1),
        name="norm_in",
    )(xp, xs, w.reshape(1, d))


def _norm_mid_kernel(x_ref, w_ref, o_ref):
    o_ref[...] = _rms(x_ref[...], w_ref[...]).astype(o_ref.dtype)


def _norm_mid(x, w, *, bt):
    t, d = x.shape
    return pl.pallas_call(
        _norm_mid_kernel,
        grid=(t // bt,),
        in_specs=[pl.BlockSpec((bt, d), lambda i: (i, 0)), pl.BlockSpec((1, d), lambda i: (0, 0))],
        out_specs=pl.BlockSpec((bt, d), lambda i: (i, 0)),
        out_shape=jax.ShapeDtypeStruct((t, d), BF16),
        compiler_params=_params(1),
        name="norm_mid",
    )(x, w.reshape(1, d))


def _norm_out_kernel(x_ref, y_ref, w_ref, op_ref, os_ref, *, n_p, ts):
    i = pl.program_id(0)

    @pl.when(i < n_p)
    def _():
        op_ref[...] = _rms(x_ref[...] + y_ref[...], w_ref[...])

    @pl.when(i == n_p)
    def _():
        os_ref[...] = _rms(x_ref[0:ts, :] + y_ref[0:ts, :], w_ref[...])


def _norm_out(x, y, w, *, tp, ts, bt):
    d = x.shape[1]
    n_p = tp // bt
    return pl.pallas_call(
        functools.partial(_norm_out_kernel, n_p=n_p, ts=ts),
        grid=(n_p + 1,),
        in_specs=[
            pl.BlockSpec((bt, d), lambda i: (i, 0)),
            pl.BlockSpec((bt, d), lambda i: (i, 0)),
            pl.BlockSpec((1, d), lambda i: (0, 0)),
        ],
        out_specs=[
            pl.BlockSpec((bt, d), lambda i: (jnp.minimum(i, n_p - 1), 0)),
            pl.BlockSpec((ts, d), lambda i: (0, 0)),
        ],
        out_shape=[jax.ShapeDtypeStruct((tp, d), F32), jax.ShapeDtypeStruct((ts, d), F32)],
        compiler_params=_params(1),
        name="norm_out",
    )(x, y, w.reshape(1, d))


def _apply_act(acc, act):
    if act == "silu":
        return _silu(acc)
    if act == "sigmoid":
        return jax.nn.sigmoid(acc)
    if act == "softplus":
        return jax.nn.softplus(acc)
    assert act == "none"
    return acc


def _weight_spec(kdim, bn, col0):
    if col0 % bn == 0:
        return pl.BlockSpec((kdim, bn), lambda j, i: (0, col0 // bn + j))
    assert col0 % LANES == 0 and bn % LANES == 0
    return pl.BlockSpec((pl.Element(kdim), pl.Element(bn)),
                        lambda j, i: (0, (col0 // LANES + j * (bn // LANES)) * LANES))


def _resident_weight(w_ref, wbf_ref, i):
    if wbf_ref is None:
        return w_ref

    @pl.when(i == 0)
    def _():
        wbf_ref[...] = w_ref[...].astype(BF16)

    return wbf_ref


def _weight_scratch(w, kdim, bn):
    return [] if w.dtype == BF16 else [pltpu.VMEM((kdim, bn), BF16)]


def _mm_act_kernel(*refs, n_p, bm, ts, act, has_bias, has_resid, has_wbf):
    x_ref, w_ref = refs[0], refs[1]
    k = 2
    b_ref = None
    if has_bias:
        b_ref = refs[k]
        k += 1
    rp_ref = rs_ref = None
    if has_resid:
        rp_ref, rs_ref = refs[k], refs[k + 1]
        k += 2
    o_ref = refs[k]
    i = pl.program_id(1)
    w_ref = _resident_weight(w_ref, refs[k + 1] if has_wbf else None, i)

    def compute(rows, r_ref):
        for r0, r1 in _row_chunks(rows):
            acc = _dot(x_ref[r0:r1, :], w_ref[...])
            if has_bias:
                acc = acc + b_ref[...]
            acc = _apply_act(acc, act)
            if has_resid:
                acc = acc + r_ref[r0:r1, :]
            o_ref[r0:r1, :] = acc.astype(o_ref.dtype)

    @pl.when(i < n_p)
    def _():
        compute(bm, rp_ref)

    @pl.when(i == n_p)
    def _():
        compute(ts, rs_ref)


def _mm_act(x, w, *, col0, n, tp, ts, bm, bn, act, out_dtype, bias=None, resid=None, name):
    t, kdim = x.shape
    n_p = tp // bm
    assert tp % bm == 0 and ts <= bm and n % bn == 0
    in_specs = [
        pl.BlockSpec((bm, kdim), lambda j, i: (i, 0)),
        _weight_spec(kdim, bn, col0),
    ]
    args = [x, w]
    if bias is not None:
        in_specs.append(pl.BlockSpec((1, bn), lambda j, i: (0, j)))
        args.append(bias.reshape(1, n))
    if resid is not None:
        in_specs.append(pl.BlockSpec((bm, bn), lambda j, i: (jnp.minimum(i, n_p - 1), j)))
        in_specs.append(pl.BlockSpec((ts, bn), lambda j, i: (0, j)))
        args.extend(resid)
    return pl.pallas_call(
        functools.partial(_mm_act_kernel, n_p=n_p, bm=bm, ts=ts, act=act,
                          has_bias=bias is not None, has_resid=resid is not None,
                          has_wbf=w.dtype != BF16),
        grid=(n // bn, n_p + 1),
        in_specs=in_specs,
        out_specs=pl.BlockSpec((bm, bn), lambda j, i: (i, j)),
        out_shape=jax.ShapeDtypeStruct((t, n), out_dtype),
        scratch_shapes=_weight_scratch(w, kdim, bn),
        compiler_params=_params(2),
        name=name,
    )(*args)


_CARRY = 8


def _conv_reset(ubuf, first_in_seq):
    @pl.when(first_in_seq)
    def _():
        ubuf[0:_CARRY, :] = jnp.zeros((_CARRY, ubuf.shape[1]), F32)


def _conv_rows(ubuf, cw_ref, u, r0, *, taps):
    rc = u.shape[0]
    ubuf[_CARRY + r0:_CARRY + r0 + rc, :] = u
    base = _CARRY + r0 - (taps - 1)
    y = ubuf[base:base + rc, :] * cw_ref[0:1, :]
    for k in range(1, taps - 1):
        y = y + ubuf[base + k:base + k + rc, :] * cw_ref[k:k + 1, :]
    y = y + u * cw_ref[taps - 1:taps, :]
    return y


def _sc_kernel(x_ref, wb_ref, wc_ref, wx_ref, cw_ref, s0_ref, s1_ref,
               g_ref, tail_ref, us_ref, ubuf, wcat, *, n_p, bm, ts, tps):
    i = pl.program_id(1)
    bn = wb_ref.shape[1]

    @pl.when(i == 0)
    def _():
        wcat[:, 0:bn] = wb_ref[...].astype(BF16)
        wcat[:, bn:2 * bn] = wc_ref[...].astype(BF16)
        wcat[:, 2 * bn:3 * bn] = wx_ref[...].astype(BF16)

    @pl.when(i < n_p)
    def _():
        _conv_reset(ubuf, (i % tps) == 0)
        for r0, r1 in _row_chunks(bm):
            p = _dot(x_ref[r0:r1, :], wcat[...])
            u = p[:, bn:2 * bn] * p[:, 2 * bn:3 * bn]
            y = _conv_rows(ubuf, cw_ref, u, r0, taps=3)
            g_ref[r0:r1, :] = (p[:, 0:bn] * y).astype(g_ref.dtype)
        tail_ref[0] = ubuf[_CARRY + bm - 2:_CARRY + bm, :]
        ubuf[0:_CARRY, :] = ubuf[bm:bm + _CARRY, :]

    @pl.when(i == n_p)
    def _():
        p = _dot(x_ref[0:ts, :], wcat[...])
        u = p[:, bn:2 * bn] * p[:, 2 * bn:3 * bn]
        us_ref[...] = u
        y = s0_ref[...] * cw_ref[0:1, :]
        y = y + s1_ref[...] * cw_ref[1:2, :]
        y = y + u * cw_ref[2:3, :]
        g_ref[0:ts, :] = (p[:, 0:bn] * y).astype(g_ref.dtype)


def _sc_branch(h, w_in, conv_w, state, *, d_sc, tp, ts, seq, bm, bn):
    t, kdim = h.shape
    n_p = tp // bm
    tps = seq // bm
    nb = tp // seq
    nj = d_sc // bn
    st = state.reshape(ts, 2 * d_sc)
    w_spec = lambda off: pl.BlockSpec((kdim, bn), lambda j, i: (0, off + j))
    return pl.pallas_call(
        functools.partial(_sc_kernel, n_p=n_p, bm=bm, ts=ts, tps=tps),
        grid=(nj, n_p + 1),
        in_specs=[
            pl.BlockSpec((bm, kdim), lambda j, i: (i, 0)),
            w_spec(0), w_spec(nj), w_spec(2 * nj),
            pl.BlockSpec((3, bn), lambda j, i: (0, j)),
            pl.BlockSpec((ts, bn), lambda j, i: (0, j)),
            pl.BlockSpec((ts, bn), lambda j, i: (0, nj + j)),
        ],
        out_specs=[
            pl.BlockSpec((bm, bn), lambda j, i: (i, j)),
            pl.BlockSpec((1, 2, bn), lambda j, i: (jnp.minimum(i, n_p - 1) // tps, 0, j)),
            pl.BlockSpec((ts, bn), lambda j, i: (0, j)),
        ],
        out_shape=[
            jax.ShapeDtypeStruct((t, d_sc), BF16),
            jax.ShapeDtypeStruct((nb, 2, d_sc), F32),
            jax.ShapeDtypeStruct((ts, d_sc), F32),
        ],
        scratch_shapes=[pltpu.VMEM((_CARRY + bm, bn), F32), pltpu.VMEM((kdim, 3 * bn), BF16)],
        compiler_params=_params(2),
        name="sc_branch",
    )(h, w_in, w_in, w_in, conv_w, st, st)


def _xbc_kernel(x_ref, w_ref, cw_ref, cb_ref, s0_ref, s1_ref, s2_ref,
                a_ref, tail_ref, rs_ref, ubuf, *wbf, n_p, bm, ts, tps):
    i = pl.program_id(1)
    w_ref = _resident_weight(w_ref, wbf[0] if wbf else None, i)

    @pl.when(i < n_p)
    def _():
        _conv_reset(ubuf, (i % tps) == 0)
        for r0, r1 in _row_chunks(bm):
            r = _dot(x_ref[r0:r1, :], w_ref[...])
            y = _conv_rows(ubuf, cw_ref, r, r0, taps=4)
            a_ref[r0:r1, :] = _silu(y + cb_ref[...]).astype(a_ref.dtype)
        tail_ref[0] = ubuf[_CARRY + bm - 3:_CARRY + bm, :]
        ubuf[0:_CARRY, :] = ubuf[bm:bm + _CARRY, :]

    @pl.when(i == n_p)
    def _():
        r = _dot(x_ref[0:ts, :], w_ref[...])
        rs_ref[...] = r
        y = s0_ref[...] * cw_ref[0:1, :]
        y = y + s1_ref[...] * cw_ref[1:2, :]
        y = y + s2_ref[...] * cw_ref[2:3, :]
        y = y + r * cw_ref[3:4, :]
        a_ref[0:ts, :] = _silu(y + cb_ref[...]).astype(a_ref.dtype)


def _xbc_branch(h, w_in, conv_w, conv_b, state, *, col0, n, tp, ts, seq, bm, bn):
    t, kdim = h.shape
    n_p = tp // bm
    tps = seq // bm
    nb = tp // seq
    nj = n // bn
    cb = col0 // bn
    assert col0 % bn == 0
    st = state.reshape(ts, 3 * n)
    return pl.pallas_call(
        functools.partial(_xbc_kernel, n_p=n_p, bm=bm, ts=ts, tps=tps),
        grid=(nj, n_p + 1),
        in_specs=[
            pl.BlockSpec((bm, kdim), lambda j, i: (i, 0)),
            pl.BlockSpec((kdim, bn), lambda j, i: (0, cb + j)),
            pl.BlockSpec((4, bn), lambda j, i: (0, j)),
            pl.BlockSpec((1, bn), lambda j, i: (0, j)),
            pl.BlockSpec((ts, bn), lambda j, i: (0, j)),
            pl.BlockSpec((ts, bn), lambda j, i: (0, nj + j)),
            pl.BlockSpec((ts, bn), lambda j, i: (0, 2 * nj + j)),
        ],
        out_specs=[
            pl.BlockSpec((bm, bn), lambda j, i: (i, j)),
            pl.BlockSpec((1, 3, bn), lambda j, i: (jnp.minimum(i, n_p - 1) // tps, 0, j)),
            pl.BlockSpec((ts, bn), lambda j, i: (0, j)),
        ],
        out_shape=[
            jax.ShapeDtypeStruct((t, n), BF16),
            jax.ShapeDtypeStruct((nb, 3, n), F32),
            jax.ShapeDtypeStruct((ts, n), F32),
        ],
        scratch_shapes=[pltpu.VMEM((_CARRY + bm, bn), F32)] + _weight_scratch(w_in, kdim, bn),
        compiler_params=_params(2),
        name="xbc_branch",
    )(h, w_in, conv_w, conv_b.reshape(1, n), st, st, st)


def _expand_heads(vals, rep2_ref):
    hi, lo = _split_hi_lo(vals)
    return _dot(jnp.concatenate([hi, lo], axis=1), rep2_ref[...])


def _gated_norm_store(y, z_act, nw_ref, o_ref, rows):
    yz = y * z_act
    ms = jnp.mean(yz * yz, axis=-1, keepdims=True)
    o_ref[0:rows, :] = (yz * lax.rsqrt(ms + EPS) * nw_ref[...]).astype(o_ref.dtype)


def _ssd_prompt_kernel(xs_ref, b_ref, c_ref, dt_ref, z_ref, alog_ref, dexp_ref, nw_ref, rep2_ref,
                       ym_ref, st_out_ref,
                       st_ref, dte_ref, eae_ref, dee_ref, acg_ref, act_ref, y_ref,
                       *, n_chunks, groups, hpg, hd, ns):
    q = SSD_CHUNK
    c = pl.program_id(1)
    gw = hpg * hd

    @pl.when(c == 0)
    def _():
        st_ref[...] = jnp.zeros(st_ref.shape, F32)

    dt = dt_ref[...]
    a = -jnp.exp(alog_ref[...])
    da = dt * a
    row = lax.broadcasted_iota(jnp.int32, (q, q), 0)
    col = lax.broadcasted_iota(jnp.int32, (q, q), 1)
    tril = (row >= col)
    hi, lo = _split_hi_lo(da)
    trilb = jnp.where(tril, 1.0, 0.0).astype(BF16)
    mid = (da - hi.astype(F32) - lo.astype(F32)).astype(BF16)
    acum = _dot(trilb, hi) + _dot(trilb, lo) + _dot(trilb, mid)
    a_last = acum[q - 1:q, :]
    dte_ref[...] = _expand_heads(dt, rep2_ref)
    eae_ref[...] = _expand_heads(jnp.exp(acum), rep2_ref)
    dee_ref[...] = _expand_heads(jnp.exp(a_last - acum), rep2_ref)
    act_ref[...] = acum.T
    acg_ref[0] = acum
    for g in range(1, groups):
        acg_ref[g] = pltpu.roll(acum, shift=acum.shape[1] - g * hpg, axis=1)

    lane = lax.broadcasted_iota(jnp.int32, (q, 2 * hd), 1)
    lo_half = lane < hd

    def group_body(g, carry):
        off = pl.multiple_of(g * gw, gw)
        noff = pl.multiple_of(g * ns, ns)
        xs_g = xs_ref[:, pl.ds(off, gw)].astype(F32)
        b_g = b_ref[:, pl.ds(noff, ns)]
        c_g = c_ref[:, pl.ds(noff, ns)]
        bt = b_g.astype(F32).T.astype(BF16)
        cb = _dot(c_g, bt)
        xdt = xs_g * dte_ref[:, pl.ds(off, gw)]
        xdt_b = xdt.astype(BF16)
        xd_b = (xdt * dee_ref[:, pl.ds(off, gw)]).astype(BF16)
        eae_g = eae_ref[:, pl.ds(off, gw)]
        st_g = st_ref[g]
        y_off = _dot(c_g, st_g.astype(BF16)) * eae_g
        st_ref[g] = st_g * eae_g[q - 1:q, :] + _dot(bt, xd_b)
        ac = acg_ref[g]
        hoff = pl.multiple_of(g * hpg, hpg)
        ac_t = act_ref[pl.ds(hoff, hpg), :]
        for pr in range(hpg // 2):
            ms = []
            for r in (2 * pr, 2 * pr + 1):
                seg = ac[:, r:r + 1] - ac_t[r:r + 1, :]
                ms.append((jnp.where(tril, jnp.exp(seg), 0.0) * cb).astype(BF16))
            lhs = jnp.concatenate(ms, axis=1)
            xp = xdt_b[:, pr * 2 * hd:(pr + 1) * 2 * hd]
            zero = jnp.zeros_like(xp)
            rhs = jnp.concatenate([jnp.where(lo_half, xp, zero), jnp.where(lo_half, zero, xp)], axis=0)
            y_pair = _dot(lhs, rhs) + y_off[:, pr * 2 * hd:(pr + 1) * 2 * hd]
            y_ref[:, pl.ds(pl.multiple_of(off + pr * 2 * hd, 2 * hd), 2 * hd)] = y_pair
        return carry

    lax.fori_loop(0, groups, group_body, 0)

    y = y_ref[...] + dexp_ref[...] * xs_ref[...].astype(F32)
    _gated_norm_store(y, z_ref[...].astype(F32), nw_ref, ym_ref, q)

    @pl.when(c == n_chunks - 1)
    def _():
        for g in range(groups):
            st_out_ref[0, g * gw:(g + 1) * gw, :] = st_ref[g].T


def _ssd_prompt(act, dt, zact, a_log, d_exp, norm_w, rep2, *, nb, seq, heads, hd, groups, ns):
    q = SSD_CHUNK
    d_inner = heads * hd
    hpg = heads // groups
    gw = hpg * hd
    n_chunks = seq // q
    gn = groups * ns
    assert heads == q and hd * 2 == q and ns == q and d_inner % gn == 0
    row = lambda b, c: b * n_chunks + c
    return pl.pallas_call(
        functools.partial(_ssd_prompt_kernel, n_chunks=n_chunks, groups=groups, hpg=hpg, hd=hd, ns=ns),
        grid=(nb, n_chunks),
        in_specs=[
            pl.BlockSpec((q, d_inner), lambda b, c: (row(b, c), 0)),
            pl.BlockSpec((q, gn), lambda b, c: (row(b, c), d_inner // gn)),
            pl.BlockSpec((q, gn), lambda b, c: (row(b, c), d_inner // gn + 1)),
            pl.BlockSpec((q, heads), lambda b, c: (row(b, c), 0)),
            pl.BlockSpec((q, d_inner), lambda b, c: (row(b, c), 0)),
            pl.BlockSpec((1, heads), lambda b, c: (0, 0)),
            pl.BlockSpec((1, d_inner), lambda b, c: (0, 0)),
            pl.BlockSpec((1, d_inner), lambda b, c: (0, 0)),
            pl.BlockSpec((2 * heads, d_inner), lambda b, c: (0, 0)),
        ],
        out_specs=[
            pl.BlockSpec((q, d_inner), lambda b, c: (row(b, c), 0)),
            pl.BlockSpec((1, d_inner, ns), lambda b, c: (b, 0, 0)),
        ],
        out_shape=[
            jax.ShapeDtypeStruct((nb * seq, d_inner), BF16),
            jax.ShapeDtypeStruct((nb, d_inner, ns), F32),
        ],
        scratch_shapes=[
            pltpu.VMEM((groups, ns, gw), F32),
            pltpu.VMEM((q, d_inner), F32),
            pltpu.VMEM((q, d_inner), F32),
            pltpu.VMEM((q, d_inner), F32),
            pltpu.VMEM((groups, q, heads), F32),
            pltpu.VMEM((heads, q), F32),
            pltpu.VMEM((q, d_inner), F32),
        ],
        compiler_params=_params(2),
        name="ssd_prompt",
    )(act, act, act, dt, zact, a_log.reshape(1, heads), d_exp, norm_w.reshape(1, d_inner), rep2)


def _ssd_sample_kernel(st_ref, xs_ref, b_ref, c_ref, dt_ref, z_ref, alog_ref, dexp_ref, nw_ref, rep2_ref,
                       st_out_ref, ym_ref, *, groups, hpg, hd, ns):
    gw = hpg * hd
    d_inner = groups * gw
    xs = xs_ref[0].astype(F32)
    dt = dt_ref[0]
    a = -jnp.exp(alog_ref[...])
    pad = jnp.zeros((6, dt.shape[1]), F32)
    both = _expand_heads(jnp.concatenate([dt, jnp.exp(dt * a), pad], axis=0), rep2_ref)
    dte = both[0:1, :]
    dae = both[1:2, :]
    xdt = xs * dte
    kr = 2 * groups
    grp_of_lane = lax.broadcasted_iota(jnp.int32, (kr, d_inner), 1) // gw
    krow = lax.broadcasted_iota(jnp.int32, (kr, d_inner), 0)
    ltf = (jnp.where(grp_of_lane == krow, jnp.broadcast_to(xdt, (kr, d_inner)), 0.0)
           + jnp.where(krow == groups, jnp.broadcast_to(dae, (kr, d_inner)), 0.0))
    lt_hi, lt_lo = _split_hi_lo(ltf)
    lt = jnp.concatenate([lt_hi, lt_lo], axis=0)
    bmat = b_ref[0].astype(F32)
    rrow = lax.broadcasted_iota(jnp.int32, (groups, 2 * ns), 0)
    rlane = lax.broadcasted_iota(jnp.int32, (groups, 2 * ns), 1)
    r_top = jnp.concatenate([bmat, jnp.zeros_like(bmat)], axis=1)
    r_bot = jnp.where((rrow == 0) & (rlane >= ns), 1.0, 0.0)
    rtf = jnp.concatenate([r_top, r_bot], axis=0)
    rt = jnp.concatenate([rtf, rtf], axis=0).astype(BF16)
    both2 = lax.dot_general(lt, rt, (((0,), (0,)), ((), ())), preferred_element_type=F32)
    new = st_ref[0] * both2[:, ns:] + both2[:, :ns]
    st_out_ref[0] = new
    yg = _dot_nt(c_ref[0], new.astype(BF16))
    own = (lax.broadcasted_iota(jnp.int32, (groups, d_inner), 1) // gw
           == lax.broadcasted_iota(jnp.int32, (groups, d_inner), 0))
    y = jnp.sum(jnp.where(own, yg, 0.0), axis=0, keepdims=True)
    y = y + dexp_ref[...] * xs
    yz = y * z_ref[0].astype(F32)
    ms = jnp.mean(yz * yz, axis=-1, keepdims=True)
    ym_ref[0] = (yz * lax.rsqrt(ms + EPS) * nw_ref[...]).astype(ym_ref.dtype)


def _ssd_sample(state, xs, bmat, cmat, dt, zact, a_log, d_exp, norm_w, rep2, *, heads, hd, groups, ns):
    ts = state.shape[0]
    d_inner = heads * hd
    hpg = heads // groups
    seq_spec = lambda shape: pl.BlockSpec((1,) + shape, lambda b: (b, 0, 0))
    const = lambda shape: pl.BlockSpec(shape, lambda b: (0, 0))
    return pl.pallas_call(
        functools.partial(_ssd_sample_kernel, groups=groups, hpg=hpg, hd=hd, ns=ns),
        grid=(ts,),
        in_specs=[
            seq_spec((d_inner, ns)), seq_spec((1, d_inner)), seq_spec((groups, ns)), seq_spec((groups, ns)),
            seq_spec((1, heads)), seq_spec((1, d_inner)),
            const((1, heads)), const((1, d_inner)), const((1, d_inner)), const((2 * heads, d_inner)),
        ],
        out_specs=[seq_spec((d_inner, ns)), seq_spec((1, d_inner))],
        out_shape=[
            jax.ShapeDtypeStruct((ts, d_inner, ns), F32),
            jax.ShapeDtypeStruct((ts, 1, d_inner), F32),
        ],
        compiler_params=_params(1),
        name="ssd_sample",
    )(state, xs, bmat, cmat, dt, zact, a_log.reshape(1, heads), d_exp, norm_w.reshape(1, d_inner), rep2)


def _merge_kernel(*refs, n_p, bm, ts, has_xs, has_prev, has_wbf):
    x_ref = refs[0]
    k = 1
    xs_ref = x_ref
    if has_xs:
        xs_ref = refs[k]
        k += 1
    w_ref, ga_ref = refs[k], refs[k + 1]
    k += 2
    prev_ref = None
    if has_prev:
        prev_ref = refs[k]
        k += 1
    o_ref = refs[k]
    i = pl.program_id(1)
    w_ref = _resident_weight(w_ref, refs[k + 1] if has_wbf else None, i)

    def compute(rows, src_ref):
        for r0, r1 in _row_chunks(rows):
            acc = _dot(src_ref[r0:r1, :], w_ref[...]) * ga_ref[r0:r1, :].astype(F32)
            if has_prev:
                acc = acc + prev_ref[r0:r1, :].astype(F32)
            o_ref[r0:r1, :] = acc.astype(o_ref.dtype)

    @pl.when(i < n_p)
    def _():
        compute(bm, x_ref)

    @pl.when(i == n_p)
    def _():
        compute(ts, xs_ref)


def _merge(x, xs, w, gates, gate_col0, prev, *, tp, ts, bm, bn, out_dtype, name):
    kdim = x.shape[1]
    n = w.shape[1]
    n_p = tp // bm
    gcb = gate_col0 // bn
    assert gate_col0 % bn == 0 and n % bn == 0 and tp % bm == 0
    has_xs = xs is not None
    has_prev = prev is not None
    if has_xs:
        in_specs = [pl.BlockSpec((bm, kdim), lambda j, i: (jnp.minimum(i, n_p - 1), 0)),
                    pl.BlockSpec((ts, kdim), lambda j, i: (0, 0))]
        args = [x, xs]
    else:
        in_specs = [pl.BlockSpec((bm, kdim), lambda j, i: (i, 0))]
        args = [x]
    in_specs += [_weight_spec(kdim, bn, 0), pl.BlockSpec((bm, bn), lambda j, i: (i, gcb + j))]
    args += [w, gates]
    if has_prev:
        in_specs.append(pl.BlockSpec((bm, bn), lambda j, i: (i, j)))
        args.append(prev)
    return pl.pallas_call(
        functools.partial(_merge_kernel, n_p=n_p, bm=bm, ts=ts, has_xs=has_xs, has_prev=has_prev,
                          has_wbf=w.dtype != BF16),
        grid=(n // bn, n_p + 1),
        in_specs=in_specs,
        out_specs=pl.BlockSpec((bm, bn), lambda j, i: (i, j)),
        out_shape=jax.ShapeDtypeStruct((tp + ts, n), out_dtype),
        scratch_shapes=_weight_scratch(w, kdim, bn),
        compiler_params=_params(2),
        name=name,
    )(*args)


def _topk_rows(s, k):
    r = s.shape[0]
    rows = lax.broadcasted_iota(jnp.int32, s.shape, 0).astype(F32)
    vals, idxs = [], []
    for _ in range(k):
        m = jnp.max(s, axis=0, keepdims=True)
        idx = jnp.min(jnp.where(s == m, rows, float(r)), axis=0, keepdims=True)
        vals.append(m)
        idxs.append(idx)
        s = jnp.where(rows == idx, NEG_INF, s)
    return jnp.concatenate(vals, axis=0), jnp.concatenate(idxs, axis=0)


def _pick_rows(table, sel):
    k = table.shape[0]
    out = jnp.zeros_like(sel)
    for r in range(k):
        out = jnp.where(sel == float(r), jnp.broadcast_to(table[r:r + 1, :], sel.shape), out)
    return out


def _staircase_counts(k):
    return [k // (i + 1) for i in range(k)]


def _fold_keys_kernel(wq_ref, k1_ref, k2_ref, o_ref, *, qh, nkeys):
    hp = lax.Precision.HIGHEST
    nt = (((1,), (1,)), ((), ()))
    w = wq_ref[...]
    o_ref[:, 0:nkeys] = lax.dot_general(w[:, :qh], k1_ref[0], nt, precision=hp,
                                        preferred_element_type=F32).astype(o_ref.dtype)
    o_ref[:, nkeys:] = lax.dot_general(w[:, qh:], k2_ref[0], nt, precision=hp,
                                       preferred_element_type=F32).astype(o_ref.dtype)


def _fold_keys(wq, keys):
    d = wq.shape[0]
    _, heads, nkeys, qh = keys.shape
    return pl.pallas_call(
        functools.partial(_fold_keys_kernel, qh=qh, nkeys=nkeys),
        grid=(heads,),
        in_specs=[
            pl.BlockSpec((d, 2 * qh), lambda hh: (0, hh)),
            pl.BlockSpec((1, nkeys, qh), lambda hh: (hh, 0, 0)),
            pl.BlockSpec((1, nkeys, qh), lambda hh: (hh, 0, 0)),
        ],
        out_specs=pl.BlockSpec((d, 2 * nkeys), lambda hh: (0, hh)),
        out_shape=jax.ShapeDtypeStruct((d, heads * 2 * nkeys), BF16),
        compiler_params=_params(1),
        name="peer_fold_keys",
    )(wq, keys[0], keys[1])


def _route_kernel(h_ref, wk_ref, ia_ref, ib_ref, gt_ref, s1_ref, s2_ref, cand_ref, *, nkeys, lanes):
    k = PEER_TOPK
    s = _dot(h_ref[...], wk_ref[...])
    s1_ref[...] = s[:, :nkeys].T
    s2_ref[...] = s[:, nkeys:].T
    counts = _staircase_counts(k)
    starts = [sum(counts[:i]) for i in range(k)]
    n_cand = sum(counts)

    def chunk(ci, carry):
        off = pl.multiple_of(ci * lanes, lanes)
        v1, i1 = _topk_rows(s1_ref[:, pl.ds(off, lanes)], k)
        v2, i2 = _topk_rows(s2_ref[:, pl.ds(off, lanes)], k)
        for i in range(k):
            cand_ref[starts[i]:starts[i] + counts[i], :] = v1[i:i + 1, :] + v2[0:counts[i], :]
        cand_ref[n_cand:, :] = jnp.full((cand_ref.shape[0] - n_cand, lanes), NEG_INF, F32)
        sv, sp = _topk_rows(cand_ref[...], k)
        e = jnp.exp(sv - sv[0:1, :])
        gt_ref[0, :, pl.ds(off, lanes)] = e / jnp.sum(e, axis=0, keepdims=True)
        sel_i = jnp.zeros_like(sp)
        sel_start = jnp.zeros_like(sp)
        for i in range(1, k):
            ge = sp >= float(starts[i])
            sel_i = sel_i + jnp.where(ge, 1.0, 0.0)
            sel_start = sel_start + jnp.where(ge, float(counts[i - 1]), 0.0)
        ia_ref[0, :, pl.ds(off, lanes)] = _pick_rows(i1, sel_i)
        ib_ref[0, :, pl.ds(off, lanes)] = _pick_rows(i2, sp - sel_start)
        return carry

    lax.fori_loop(0, s1_ref.shape[1] // lanes, chunk, 0)


def _route(h2, wk, *, heads, nkeys, tq):
    t, d = h2.shape
    k = PEER_TOPK
    out = jax.ShapeDtypeStruct((heads, k, t), F32)
    out_spec = pl.BlockSpec((1, k, tq), lambda i, hh: (hh, 0, i))
    return pl.pallas_call(
        functools.partial(_route_kernel, nkeys=nkeys, lanes=128),
        grid=(t // tq, heads),
        in_specs=[
            pl.BlockSpec((tq, d), lambda i, hh: (i, 0)),
            pl.BlockSpec((d, 2 * nkeys), lambda i, hh: (0, hh)),
        ],
        out_specs=[out_spec, out_spec, out_spec],
        out_shape=[out, out, out],
        scratch_shapes=[pltpu.VMEM((nkeys, tq), F32), pltpu.VMEM((nkeys, tq), F32),
                        pltpu.VMEM((-(-sum(_staircase_counts(k)) // 8) * 8, 128), F32)],
        compiler_params=_params(2),
        name="peer_route",
    )(h2, wk)


_SCATTER_UNROLL = 8
_SCATTER_PITCH = 136
_PACK_ROWS = 16


def _scatter_kernel(ia_ref, ib_ref, gt_ref, w_ref, ia_s, ib_s, gt_s, wsc, *, nkeys):
    tw = ia_ref.shape[1]
    ia_s[...] = ia_ref[...].T
    ib_s[...] = ib_ref[...].T
    gt_s[...] = gt_ref[...].T
    nsel = ia_ref.shape[0]
    key_id = lax.broadcasted_iota(jnp.int32, (nkeys, nsel), 0).astype(F32)

    def token_group(gi, carry):
        for k in range(_SCATTER_UNROLL):
            tk = gi * _SCATTER_UNROLL + k
            a_row = ia_s[pl.ds(tk, 1), :]
            b_row = ib_s[pl.ds(tk, 1), :]
            g_row = gt_s[pl.ds(tk, 1), :]
            at = jnp.where(key_id == a_row, jnp.broadcast_to(g_row, key_id.shape), 0.0).astype(BF16)
            bt = jnp.where(key_id == b_row, 1.0, 0.0).astype(BF16)
            wsc[pl.ds(pl.multiple_of(tk * _SCATTER_PITCH, 8), nkeys), :] = _dot_nt(at, bt)
        return carry

    lax.fori_loop(0, tw // _SCATTER_UNROLL, token_group, 0)

    def relayout(tg, carry):
        row0 = pl.multiple_of(tg * _PACK_ROWS, _PACK_ROWS)
        base = tg * (_PACK_ROWS * _SCATTER_PITCH)
        for a in range(nkeys):
            rows = wsc[pl.ds(base + a, _PACK_ROWS, stride=_SCATTER_PITCH), :]
            w_ref[pl.ds(row0, _PACK_ROWS), a * nkeys:(a + 1) * nkeys] = rows.astype(w_ref.dtype)
        return carry

    lax.fori_loop(0, tw // _PACK_ROWS, relayout, 0)


def _scatter(ia, ib, gt, *, nkeys, tw):
    nsel, t = ia.shape
    assert tw % _SCATTER_UNROLL == 0 and tw % _PACK_ROWS == 0
    spec = pl.BlockSpec((nsel, tw), lambda i: (0, i))
    return pl.pallas_call(
        functools.partial(_scatter_kernel, nkeys=nkeys),
        grid=(t // tw,),
        in_specs=[spec, spec, spec],
        out_specs=pl.BlockSpec((tw, nkeys * nkeys), lambda i: (i, 0)),
        out_shape=jax.ShapeDtypeStruct((t, nkeys * nkeys), BF16),
        scratch_shapes=[pltpu.VMEM((tw, nsel), F32)] * 3 + [pltpu.VMEM((tw * _SCATTER_PITCH, nkeys), F32)],
        compiler_params=_params(1),
        name="peer_scatter",
    )(ia, ib, gt)


def _experts_kernel(h_ref, u_ref, v_ref, w_ref, o_ref):
    e = pl.program_id(1)

    @pl.when(e == 0)
    def _():
        o_ref[...] = jnp.zeros(o_ref.shape, F32)

    tm = h_ref.shape[0]
    rc = EXPERT_ROW_CHUNK if tm % EXPERT_ROW_CHUNK == 0 else tm
    for r0 in range(0, tm, rc):
        s = _dot_nt(h_ref[r0:r0 + rc, :], u_ref[...])
        gelu = 0.5 * s * (1.0 + lax.erf(s * (2.0 ** -0.5)))
        act = gelu * w_ref[r0:r0 + rc, :].astype(F32)
        o_ref[r0:r0 + rc, :] += _dot(act.astype(BF16), v_ref[...])


def _experts(h2, u, v, w, *, tm, te):
    t, d = h2.shape
    ne = u.shape[0]
    return pl.pallas_call(
        _experts_kernel,
        grid=(t // tm, ne // te),
        in_specs=[
            pl.BlockSpec((tm, d), lambda i, e: (i, 0), pipeline_mode=pl.Buffered(1)),
            pl.BlockSpec((te, d), lambda i, e: (e, 0)),
            pl.BlockSpec((te, d), lambda i, e: (e, 0)),
            pl.BlockSpec((tm, te), lambda i, e: (i, e)),
        ],
        out_specs=pl.BlockSpec((tm, d), lambda i, e: (i, 0), pipeline_mode=pl.Buffered(1)),
        out_shape=jax.ShapeDtypeStruct((t, d), F32),
        compiler_params=_params(2),
        name="peer_experts",
    )(h2, u, v, w)


def _largest_divisor(n, cap, multiple):
    best = None
    for cand in range(multiple, min(n, cap) + 1, multiple):
        if n % cand == 0:
            best = cand
    assert best is not None, (n, cap, multiple)
    return best


def kernel(x_prompt, x_sample, state_shortconv, state_mamba_conv, state_ssm, ln1_w, w_in, sc_conv_w,
           sc_out_w, m_conv_w, m_conv_b, m_dt_bias, m_A_log, m_D, m_norm_w, m_out_w, w_o, ln2_w,
           peer_wq, peer_keys, peer_u, peer_v, final_norm_w):
    nb, seq, d = x_prompt.shape
    ts = x_sample.shape[0]
    assert x_sample.shape[1] == 1 and ln1_w.shape[0] == 1, "single layer, one new token per sample"
    tp = nb * seq
    t = tp + ts
    d_sc = state_shortconv.shape[-1]
    conv_dim = state_mamba_conv.shape[-1]
    _, _, heads, hd, ns = state_ssm.shape
    d_inner = heads * hd
    groups = (conv_dim - d_inner) // (2 * ns)
    gn = groups * ns
    nkeys = peer_keys.shape[3]
    c_z = 3 * d_sc
    c_xbc = c_z + d_inner
    c_dt = c_xbc + conv_dim
    c_gate = c_dt + heads

    bm = _largest_divisor(seq, 1024, 128)
    bm_small = _largest_divisor(seq, 512, 128)
    bt = _largest_divisor(seq, 256, 8)

    xp2 = x_prompt.reshape(tp, d)
    xs2 = x_sample.reshape(ts, d)
    w_in0 = w_in[0]

    h = _norm_in(xp2, xs2, ln1_w[0], bt=bt)

    g, sc_tail_p, sc_u_s = _sc_branch(h, w_in0, sc_conv_w[0], state_shortconv[0], d_sc=d_sc, tp=tp, ts=ts,
                                      seq=seq, bm=bm, bn=_largest_divisor(d_sc, 256, 128))
    act, mc_tail_p, mc_raw_s = _xbc_branch(h, w_in0, m_conv_w[0], m_conv_b[0], state_mamba_conv[0],
                                           col0=c_xbc, n=conv_dim, tp=tp, ts=ts, seq=seq, bm=bm,
                                           bn=_largest_divisor(conv_dim, 512, 128))
    zact = _mm_act(h, w_in0, col0=c_z, n=d_inner, tp=tp, ts=ts, bm=bm_small,
                   bn=_largest_divisor(d_inner, 1024, 128), act="silu", out_dtype=BF16, name="proj_z")
    dt = _mm_act(h, w_in0, col0=c_dt, n=heads, tp=tp, ts=ts, bm=bm, bn=heads, act="softplus",
                 out_dtype=F32, bias=m_dt_bias[0], name="proj_dt")
    gates = _mm_act(h, w_in0, col0=c_gate, n=2 * d, tp=tp, ts=ts, bm=bm_small,
                    bn=_largest_divisor(d, 1024, 128), act="sigmoid", out_dtype=BF16, name="proj_gates")

    d_exp = jnp.repeat(m_D[0], hd).reshape(1, d_inner)
    rep = (jnp.arange(d_inner)[None, :] // hd == jnp.arange(heads)[:, None]).astype(BF16)
    rep2 = jnp.concatenate([rep, rep], axis=0)
    ym_p, ssm_p = _ssd_prompt(act, dt, zact, m_A_log[0], d_exp, m_norm_w[0], rep2, nb=nb, seq=seq,
                              heads=heads, hd=hd, groups=groups, ns=ns)
    act_s = act[tp:]
    ssm_s, ym_s = _ssd_sample(
        state_ssm[0].reshape(ts, d_inner, ns),
        act_s[:, :d_inner].astype(F32).reshape(ts, 1, d_inner),
        act_s[:, d_inner:d_inner + gn].reshape(ts, groups, ns),
        act_s[:, d_inner + gn:].reshape(ts, groups, ns),
        dt[tp:].reshape(ts, 1, heads),
        zact[tp:].astype(F32).reshape(ts, 1, d_inner),
        m_A_log[0], d_exp, m_norm_w[0], rep2, heads=heads, hd=hd, groups=groups, ns=ns)
    bn_o = _largest_divisor(d, 512, 128)
    mix_a = _merge(g, None, sc_out_w[0], gates, 0, None, tp=tp, ts=ts, bm=bm, bn=bn_o,
                   out_dtype=F32, name="merge_a")
    mixed = _merge(ym_p, ym_s.reshape(ts, d_inner).astype(BF16), m_out_w[0].astype(BF16), gates, d, mix_a,
                   tp=tp, ts=ts, bm=bm_small, bn=bn_o, out_dtype=BF16, name="merge_b")
    x1 = _mm_act(mixed, w_o[0], col0=0, n=d, tp=tp, ts=ts, bm=bm, bn=bn_o, act="none",
                 out_dtype=F32, resid=(xp2, xs2), name="proj_o")

    tok_tile = _largest_divisor(t, 640, 128)
    h2 = _norm_mid(x1, ln2_w[0], bt=_largest_divisor(t, 256, 8))
    wk = _fold_keys(peer_wq[0], peer_keys[0])
    ia, ib, gt = _route(h2, wk, heads=peer_keys.shape[2], nkeys=nkeys, tq=tok_tile)
    nsel = ia.shape[0] * ia.shape[1]
    w = _scatter(ia.reshape(nsel, t), ib.reshape(nsel, t), gt.reshape(nsel, t), nkeys=nkeys, tw=128)
    peer = _experts(h2, peer_u[0].astype(BF16), peer_v[0].astype(BF16), w,
                    tm=tok_tile, te=_largest_divisor(nkeys * nkeys, 512, 128))
    y_p, y_s = _norm_out(x1, peer, final_norm_w, tp=tp, ts=ts, bt=bt)

    new_sc_s = jnp.stack([state_shortconv[0][:, 1], sc_u_s], axis=1)
    new_mc_s = jnp.stack([state_mamba_conv[0][:, 1], state_mamba_conv[0][:, 2], mc_raw_s], axis=1)
    return (
        y_p.reshape(nb, seq, d),
        y_s.reshape(ts, 1, d),
        sc_tail_p[None],
        mc_tail_p[None],
        ssm_p.reshape(1, nb, heads, hd, ns),
        new_sc_s[None],
        new_mc_s[None],
        ssm_s.reshape(1, ts, heads, hd, ns),
    )
```

```python
import functools

import jax
import jax.numpy as jnp
from jax import lax
from jax.experimental import pallas as pl
from jax.experimental.pallas import tpu as pltpu

F32 = jnp.float32
BF16 = jnp.bfloat16
EPS = 1e-6
PEER_TOPK = 16
SSD_CHUNK = 128
VMEM_LIMIT_BYTES = 56 * 1024 * 1024
NEG_INF = float("-inf")
LANES = 128


def _params(n_grid_axes):
    return pltpu.CompilerParams(
        dimension_semantics=("arbitrary",) * n_grid_axes,
        vmem_limit_bytes=VMEM_LIMIT_BYTES,
    )


def _dot(a, b):
    return jnp.dot(a, b, preferred_element_type=F32)


def _dot_nt(a, b):
    return lax.dot_general(a, b, (((1,), (1,)), ((), ())), preferred_element_type=F32)


def _silu(x):
    return x * jax.nn.sigmoid(x)


ROW_CHUNK = 256
EXPERT_ROW_CHUNK = 640


def _row_chunks(rows):
    rc = ROW_CHUNK if rows % ROW_CHUNK == 0 else rows
    return [(r0, r0 + rc) for r0 in range(0, rows, rc)]


def _split_hi_lo(x):
    hi = x.astype(BF16)
    lo = (x - hi.astype(F32)).astype(BF16)
    return hi, lo


def _rms(x, w):
    ms = jnp.mean(x * x, axis=-1, keepdims=True)
    return x * lax.rsqrt(ms + EPS) * w


def _norm_in_kernel(xp_ref, xs_ref, w_ref, o_ref, *, n_p, ts):
    i = pl.program_id(0)

    @pl.when(i < n_p)
    def _():
        o_ref[...] = _rms(xp_ref[...], w_ref[...]).astype(o_ref.dtype)

    @pl.when(i == n_p)
    def _():
        o_ref[0:ts, :] = _rms(xs_ref[...], w_ref[...]).astype(o_ref.dtype)


def _norm_in(xp, xs, w, *, bt):
    tp, d = xp.shape
    ts = xs.shape[0]
    n_p = tp // bt
    return pl.pallas_call(
        functools.partial(_norm_in_kernel, n_p=n_p, ts=ts),
        grid=(n_p + 1,),
        in_specs=[
            pl.BlockSpec((bt, d), lambda i: (jnp.minimum(i, n_p - 1), 0)),
            pl.BlockSpec((ts, d), lambda i: (0, 0)),
            pl.BlockSpec((1, d), lambda i: (0, 0)),
        ],
        out_specs=pl.BlockSpec((bt, d), lambda i: (i, 0)),
        out_shape=jax.ShapeDtypeStruct((tp + ts, d), BF16),
        compiler_params=_params(1),
        name="norm_in",
    )(xp, xs, w.reshape(1, d))


def _norm_mid_kernel(x_ref, w_ref, o_ref):
    o_ref[...] = _rms(x_ref[...], w_ref[...]).astype(o_ref.dtype)


def _norm_mid(x, w, *, bt):
    t, d = x.shape
    return pl.pallas_call(
        _norm_mid_kernel,
        grid=(t // bt,),
        in_specs=[pl.BlockSpec((bt, d), lambda i: (i, 0)), pl.BlockSpec((1, d), lambda i: (0, 0))],
        out_specs=pl.BlockSpec((bt, d), lambda i: (i, 0)),
        out_shape=jax.ShapeDtypeStruct((t, d), BF16),
        compiler_params=_params(1),
        name="norm_mid",
    )(x, w.reshape(1, d))


def _norm_out_kernel(x_ref, y_ref, w_ref, op_ref, os_ref, *, n_p, ts):
    i = pl.program_id(0)

    @pl.when(i < n_p)
    def _():
        op_ref[...] = _rms(x_ref[...] + y_ref[...], w_ref[...])

    @pl.when(i == n_p)
    def _():
        os_ref[...] = _rms(x_ref[0:ts, :] + y_ref[0:ts, :], w_ref[...])


def _norm_out(x, y, w, *, tp, ts, bt):
    d = x.shape[1]
    n_p = tp // bt
    return pl.pallas_call(
        functools.partial(_norm_out_kernel, n_p=n_p, ts=ts),
        grid=(n_p + 1,),
        in_specs=[
            pl.BlockSpec((bt, d), lambda i: (i, 0)),
            pl.BlockSpec((bt, d), lambda i: (i, 0)),
            pl.BlockSpec((1, d), lambda i: (0, 0)),
        ],
        out_specs=[
            pl.BlockSpec((bt, d), lambda i: (jnp.minimum(i, n_p - 1), 0)),
            pl.BlockSpec((ts, d), lambda i: (0, 0)),
        ],
        out_shape=[jax.ShapeDtypeStruct((tp, d), F32), jax.ShapeDtypeStruct((ts, d), F32)],
        compiler_params=_params(1),
        name="norm_out",
    )(x, y, w.reshape(1, d))


def _apply_act(acc, act):
    if act == "silu":
        return _silu(acc)
    if act == "sigmoid":
        return jax.nn.sigmoid(acc)
    if act == "softplus":
        return jax.nn.softplus(acc)
    assert act == "none"
    return acc


def _weight_spec(kdim, bn, col0):
    if col0 % bn == 0:
        return pl.BlockSpec((kdim, bn), lambda j, i: (0, col0 // bn + j))
    assert col0 % LANES == 0 and bn % LANES == 0
    return pl.BlockSpec((pl.Element(kdim), pl.Element(bn)),
                        lambda j, i: (0, (col0 // LANES + j * (bn // LANES)) * LANES))


def _resident_weight(w_ref, wbf_ref, i):
    if wbf_ref is None:
        return w_ref

    @pl.when(i == 0)
    def _():
        wbf_ref[...] = w_ref[...].astype(BF16)

    return wbf_ref


def _weight_scratch(w, kdim, bn):
    return [] if w.dtype == BF16 else [pltpu.VMEM((kdim, bn), BF16)]


def _mm_act_kernel(*refs, n_p, bm, ts, act, has_bias, has_resid, has_wbf):
    x_ref, w_ref = refs[0], refs[1]
    k = 2
    b_ref = None
    if has_bias:
        b_ref = refs[k]
        k += 1
    rp_ref = rs_ref = None
    if has_resid:
        rp_ref, rs_ref = refs[k], refs[k + 1]
        k += 2
    o_ref = refs[k]
    i = pl.program_id(1)
    w_ref = _resident_weight(w_ref, refs[k + 1] if has_wbf else None, i)

    def compute(rows, r_ref):
        for r0, r1 in _row_chunks(rows):
            acc = _dot(x_ref[r0:r1, :], w_ref[...])
            if has_bias:
                acc = acc + b_ref[...]
            acc = _apply_act(acc, act)
            if has_resid:
                acc = acc + r_ref[r0:r1, :]
            o_ref[r0:r1, :] = acc.astype(o_ref.dtype)

    @pl.when(i < n_p)
    def _():
        compute(bm, rp_ref)

    @pl.when(i == n_p)
    def _():
        compute(ts, rs_ref)


def _mm_act(x, w, *, col0, n, tp, ts, bm, bn, act, out_dtype, bias=None, resid=None, name):
    t, kdim = x.shape
    n_p = tp // bm
    assert tp % bm == 0 and ts <= bm and n % bn == 0
    in_specs = [
        pl.BlockSpec((bm, kdim), lambda j, i: (i, 0)),
        _weight_spec(kdim, bn, col0),
    ]
    args = [x, w]
    if bias is not None:
        in_specs.append(pl.BlockSpec((1, bn), lambda j, i: (0, j)))
        args.append(bias.reshape(1, n))
    if resid is not None:
        in_specs.append(pl.BlockSpec((bm, bn), lambda j, i: (jnp.minimum(i, n_p - 1), j)))
        in_specs.append(pl.BlockSpec((ts, bn), lambda j, i: (0, j)))
        args.extend(resid)
    return pl.pallas_call(
        functools.partial(_mm_act_kernel, n_p=n_p, bm=bm, ts=ts, act=act,
                          has_bias=bias is not None, has_resid=resid is not None,
                          has_wbf=w.dtype != BF16),
        grid=(n // bn, n_p + 1),
        in_specs=in_specs,
        out_specs=pl.BlockSpec((bm, bn), lambda j, i: (i, j)),
        out_shape=jax.ShapeDtypeStruct((t, n), out_dtype),
        scratch_shapes=_weight_scratch(w, kdim, bn),
        compiler_params=_params(2),
        name=name,
    )(*args)


_CARRY = 8


def _conv_reset(ubuf, first_in_seq):
    @pl.when(first_in_seq)
    def _():
        ubuf[0:_CARRY, :] = jnp.zeros((_CARRY, ubuf.shape[1]), F32)


def _conv_rows(ubuf, cw_ref, u, r0, *, taps):
    rc = u.shape[0]
    ubuf[_CARRY + r0:_CARRY + r0 + rc, :] = u
    ext = ubuf[r0:r0 + _CARRY + rc, :]
    y = None
    for k in range(taps - 1):
        shifted = pltpu.roll(ext, shift=taps - 1 - k, axis=0)[_CARRY:_CARRY + rc, :]
        term = shifted * cw_ref[k:k + 1, :]
        y = term if y is None else y + term
    y = y + u * cw_ref[taps - 1:taps, :]
    return y


def _sc_kernel(x_ref, wb_ref, wc_ref, wx_ref, cw_ref, s0_ref, s1_ref,
               g_ref, tail_ref, us_ref, ubuf, wcat, *, n_p, bm, ts, tps):
    i = pl.program_id(1)
    bn = wb_ref.shape[1]

    @pl.when(i == 0)
    def _():
        wcat[:, 0:bn] = wb_ref[...].astype(BF16)
        wcat[:, bn:2 * bn] = wc_ref[...].astype(BF16)
        wcat[:, 2 * bn:3 * bn] = wx_ref[...].astype(BF16)

    @pl.when(i < n_p)
    def _():
        _conv_reset(ubuf, (i % tps) == 0)
        for r0, r1 in _row_chunks(bm):
            p = _dot(x_ref[r0:r1, :], wcat[...])
            u = p[:, bn:2 * bn] * p[:, 2 * bn:3 * bn]
            y = _conv_rows(ubuf, cw_ref, u, r0, taps=3)
            g_ref[r0:r1, :] = (p[:, 0:bn] * y).astype(g_ref.dtype)
        tail_ref[0] = ubuf[_CARRY + bm - 2:_CARRY + bm, :]
        ubuf[0:_CARRY, :] = ubuf[bm:bm + _CARRY, :]

    @pl.when(i == n_p)
    def _():
        p = _dot(x_ref[0:ts, :], wcat[...])
        u = p[:, bn:2 * bn] * p[:, 2 * bn:3 * bn]
        us_ref[...] = u
        y = s0_ref[...] * cw_ref[0:1, :]
        y = y + s1_ref[...] * cw_ref[1:2, :]
        y = y + u * cw_ref[2:3, :]
        g_ref[0:ts, :] = (p[:, 0:bn] * y).astype(g_ref.dtype)


def _sc_branch(h, w_in, conv_w, state, *, d_sc, tp, ts, seq, bm, bn):
    t, kdim = h.shape
    n_p = tp // bm
    tps = seq // bm
    nb = tp // seq
    nj = d_sc // bn
    st = state.reshape(ts, 2 * d_sc)
    w_spec = lambda off: pl.BlockSpec((kdim, bn), lambda j, i: (0, off + j))
    return pl.pallas_call(
        functools.partial(_sc_kernel, n_p=n_p, bm=bm, ts=ts, tps=tps),
        grid=(nj, n_p + 1),
        in_specs=[
            pl.BlockSpec((bm, kdim), lambda j, i: (i, 0)),
            w_spec(0), w_spec(nj), w_spec(2 * nj),
            pl.BlockSpec((3, bn), lambda j, i: (0, j)),
            pl.BlockSpec((ts, bn), lambda j, i: (0, j)),
            pl.BlockSpec((ts, bn), lambda j, i: (0, nj + j)),
        ],
        out_specs=[
            pl.BlockSpec((bm, bn), lambda j, i: (i, j)),
            pl.BlockSpec((1, 2, bn), lambda j, i: (jnp.minimum(i, n_p - 1) // tps, 0, j)),
            pl.BlockSpec((ts, bn), lambda j, i: (0, j)),
        ],
        out_shape=[
            jax.ShapeDtypeStruct((t, d_sc), BF16),
            jax.ShapeDtypeStruct((nb, 2, d_sc), F32),
            jax.ShapeDtypeStruct((ts, d_sc), F32),
        ],
        scratch_shapes=[pltpu.VMEM((_CARRY + bm, bn), F32), pltpu.VMEM((kdim, 3 * bn), BF16)],
        compiler_params=_params(2),
        name="sc_branch",
    )(h, w_in, w_in, w_in, conv_w, st, st)


def _xbc_kernel(x_ref, w_ref, cw_ref, cb_ref, s0_ref, s1_ref, s2_ref,
                a_ref, tail_ref, rs_ref, ubuf, *wbf, n_p, bm, ts, tps):
    i = pl.program_id(1)
    w_ref = _resident_weight(w_ref, wbf[0] if wbf else None, i)

    @pl.when(i < n_p)
    def _():
        _conv_reset(ubuf, (i % tps) == 0)
        for r0, r1 in _row_chunks(bm):
            r = _dot(x_ref[r0:r1, :], w_ref[...])
            y = _conv_rows(ubuf, cw_ref, r, r0, taps=4)
            a_ref[r0:r1, :] = _silu(y + cb_ref[...]).astype(a_ref.dtype)
        tail_ref[0] = ubuf[_CARRY + bm - 3:_CARRY + bm, :]
        ubuf[0:_CARRY, :] = ubuf[bm:bm + _CARRY, :]

    @pl.when(i == n_p)
    def _():
        r = _dot(x_ref[0:ts, :], w_ref[...])
        rs_ref[...] = r
        y = s0_ref[...] * cw_ref[0:1, :]
        y = y + s1_ref[...] * cw_ref[1:2, :]
        y = y + s2_ref[...] * cw_ref[2:3, :]
        y = y + r * cw_ref[3:4, :]
        a_ref[0:ts, :] = _silu(y + cb_ref[...]).astype(a_ref.dtype)


def _xbc_branch(h, w_in, conv_w, conv_b, state, *, col0, n, tp, ts, seq, bm, bn):
    t, kdim = h.shape
    n_p = tp // bm
    tps = seq // bm
    nb = tp // seq
    nj = n // bn
    cb = col0 // bn
    assert col0 % bn == 0
    st = state.reshape(ts, 3 * n)
    return pl.pallas_call(
        functools.partial(_xbc_kernel, n_p=n_p, bm=bm, ts=ts, tps=tps),
        grid=(nj, n_p + 1),
        in_specs=[
            pl.BlockSpec((bm, kdim), lambda j, i: (i, 0)),
            pl.BlockSpec((kdim, bn), lambda j, i: (0, cb + j)),
            pl.BlockSpec((4, bn), lambda j, i: (0, j)),
            pl.BlockSpec((1, bn), lambda j, i: (0, j)),
            pl.BlockSpec((ts, bn), lambda j, i: (0, j)),
            pl.BlockSpec((ts, bn), lambda j, i: (0, nj + j)),
            pl.BlockSpec((ts, bn), lambda j, i: (0, 2 * nj + j)),
        ],
        out_specs=[
            pl.BlockSpec((bm, bn), lambda j, i: (i, j)),
            pl.BlockSpec((1, 3, bn), lambda j, i: (jnp.minimum(i, n_p - 1) // tps, 0, j)),
            pl.BlockSpec((ts, bn), lambda j, i: (0, j)),
        ],
        out_shape=[
            jax.ShapeDtypeStruct((t, n), BF16),
            jax.ShapeDtypeStruct((nb, 3, n), F32),
            jax.ShapeDtypeStruct((ts, n), F32),
        ],
        scratch_shapes=[pltpu.VMEM((_CARRY + bm, bn), F32)] + _weight_scratch(w_in, kdim, bn),
        compiler_params=_params(2),
        name="xbc_branch",
    )(h, w_in, conv_w, conv_b.reshape(1, n), st, st, st)


def _expand_heads(vals, rep2_ref):
    hi, lo = _split_hi_lo(vals)
    return _dot(jnp.concatenate([hi, lo], axis=1), rep2_ref[...])


def _gated_norm_store(y, z_act, nw_ref, o_ref, rows):
    yz = y * z_act
    ms = jnp.mean(yz * yz, axis=-1, keepdims=True)
    o_ref[0:rows, :] = (yz * lax.rsqrt(ms + EPS) * nw_ref[...]).astype(o_ref.dtype)


def _ssd_prompt_kernel(xs_ref, b_ref, c_ref, dt_ref, z_ref, alog_ref, dexp_ref, nw_ref, rep2_ref,
                       ym_ref, st_out_ref,
                       st_ref, dte_ref, eae_ref, dee_ref, acg_ref, act_ref, y_ref,
                       *, n_chunks, groups, hpg, hd, ns):
    q = SSD_CHUNK
    c = pl.program_id(1)
    gw = hpg * hd

    @pl.when(c == 0)
    def _():
        st_ref[...] = jnp.zeros(st_ref.shape, F32)

    dt = dt_ref[...]
    a = -jnp.exp(alog_ref[...])
    da = dt * a
    row = lax.broadcasted_iota(jnp.int32, (q, q), 0)
    col = lax.broadcasted_iota(jnp.int32, (q, q), 1)
    tril = (row >= col)
    hi, lo = _split_hi_lo(da)
    trilb = jnp.where(tril, 1.0, 0.0).astype(BF16)
    mid = (da - hi.astype(F32) - lo.astype(F32)).astype(BF16)
    acum = _dot(trilb, hi) + _dot(trilb, lo) + _dot(trilb, mid)
    a_last = acum[q - 1:q, :]
    dte_ref[...] = _expand_heads(dt, rep2_ref)
    eae_ref[...] = _expand_heads(jnp.exp(acum), rep2_ref)
    dee_ref[...] = _expand_heads(jnp.exp(a_last - acum), rep2_ref)
    act_ref[...] = acum.T
    acg_ref[0] = acum
    for g in range(1, groups):
        acg_ref[g] = pltpu.roll(acum, shift=acum.shape[1] - g * hpg, axis=1)

    lane = lax.broadcasted_iota(jnp.int32, (q, 2 * hd), 1)
    lo_half = lane < hd

    def group_body(g, carry):
        off = pl.multiple_of(g * gw, gw)
        noff = pl.multiple_of(g * ns, ns)
        xs_g = xs_ref[:, pl.ds(off, gw)].astype(F32)
        b_g = b_ref[:, pl.ds(noff, ns)]
        c_g = c_ref[:, pl.ds(noff, ns)]
        bt = b_g.astype(F32).T.astype(BF16)
        cb = _dot(c_g, bt)
        xdt = xs_g * dte_ref[:, pl.ds(off, gw)]
        xdt_b = xdt.astype(BF16)
        xd_b = (xdt * dee_ref[:, pl.ds(off, gw)]).astype(BF16)
        eae_g = eae_ref[:, pl.ds(off, gw)]
        st_g = st_ref[g]
        y_off = _dot(c_g, st_g.astype(BF16)) * eae_g
        st_ref[g] = st_g * eae_g[q - 1:q, :] + _dot(bt, xd_b)
        ac = acg_ref[g]
        hoff = pl.multiple_of(g * hpg, hpg)
        ac_t = act_ref[pl.ds(hoff, hpg), :]
        for pr in range(hpg // 2):
            ms = []
            for r in (2 * pr, 2 * pr + 1):
                seg = ac[:, r:r + 1] - ac_t[r:r + 1, :]
                ms.append((jnp.where(tril, jnp.exp(seg), 0.0) * cb).astype(BF16))
            lhs = jnp.concatenate(ms, axis=1)
            xp = xdt_b[:, pr * 2 * hd:(pr + 1) * 2 * hd]
            zero = jnp.zeros_like(xp)
            rhs = jnp.concatenate([jnp.where(lo_half, xp, zero), jnp.where(lo_half, zero, xp)], axis=0)
            y_pair = _dot(lhs, rhs) + y_off[:, pr * 2 * hd:(pr + 1) * 2 * hd]
            y_ref[:, pl.ds(pl.multiple_of(off + pr * 2 * hd, 2 * hd), 2 * hd)] = y_pair
        return carry

    lax.fori_loop(0, groups, group_body, 0)

    y = y_ref[...] + dexp_ref[...] * xs_ref[...].astype(F32)
    _gated_norm_store(y, z_ref[...].astype(F32), nw_ref, ym_ref, q)

    @pl.when(c == n_chunks - 1)
    def _():
        for g in range(groups):
            st_out_ref[0, g * gw:(g + 1) * gw, :] = st_ref[g].T


def _ssd_prompt(act, dt, zact, a_log, d_exp, norm_w, rep2, *, nb, seq, heads, hd, groups, ns):
    q = SSD_CHUNK
    d_inner = heads * hd
    hpg = heads // groups
    gw = hpg * hd
    n_chunks = seq // q
    gn = groups * ns
    assert heads == q and hd * 2 == q and ns == q and d_inner % gn == 0
    row = lambda b, c: b * n_chunks + c
    return pl.pallas_call(
        functools.partial(_ssd_prompt_kernel, n_chunks=n_chunks, groups=groups, hpg=hpg, hd=hd, ns=ns),
        grid=(nb, n_chunks),
        in_specs=[
            pl.BlockSpec((q, d_inner), lambda b, c: (row(b, c), 0)),
            pl.BlockSpec((q, gn), lambda b, c: (row(b, c), d_inner // gn)),
            pl.BlockSpec((q, gn), lambda b, c: (row(b, c), d_inner // gn + 1)),
            pl.BlockSpec((q, heads), lambda b, c: (row(b, c), 0)),
            pl.BlockSpec((q, d_inner), lambda b, c: (row(b, c), 0)),
            pl.BlockSpec((1, heads), lambda b, c: (0, 0)),
            pl.BlockSpec((1, d_inner), lambda b, c: (0, 0)),
            pl.BlockSpec((1, d_inner), lambda b, c: (0, 0)),
            pl.BlockSpec((2 * heads, d_inner), lambda b, c: (0, 0)),
        ],
        out_specs=[
            pl.BlockSpec((q, d_inner), lambda b, c: (row(b, c), 0)),
            pl.BlockSpec((1, d_inner, ns), lambda b, c: (b, 0, 0)),
        ],
        out_shape=[
            jax.ShapeDtypeStruct((nb * seq, d_inner), BF16),
            jax.ShapeDtypeStruct((nb, d_inner, ns), F32),
        ],
        scratch_shapes=[
            pltpu.VMEM((groups, ns, gw), F32),
            pltpu.VMEM((q, d_inner), F32),
            pltpu.VMEM((q, d_inner), F32),
            pltpu.VMEM((q, d_inner), F32),
            pltpu.VMEM((groups, q, heads), F32),
            pltpu.VMEM((heads, q), F32),
            pltpu.VMEM((q, d_inner), F32),
        ],
        compiler_params=_params(2),
        name="ssd_prompt",
    )(act, act, act, dt, zact, a_log.reshape(1, heads), d_exp, norm_w.reshape(1, d_inner), rep2)


def _ssd_sample_kernel(st_ref, xs_ref, b_ref, c_ref, dt_ref, z_ref, alog_ref, dexp_ref, nw_ref, rep2_ref,
                       st_out_ref, ym_ref, *, groups, hpg, hd, ns):
    gw = hpg * hd
    d_inner = groups * gw
    xs = xs_ref[0].astype(F32)
    dt = dt_ref[0]
    a = -jnp.exp(alog_ref[...])
    pad = jnp.zeros((6, dt.shape[1]), F32)
    both = _expand_heads(jnp.concatenate([dt, jnp.exp(dt * a), pad], axis=0), rep2_ref)
    dte = both[0:1, :]
    dae = both[1:2, :]
    xdt = xs * dte
    kr = 2 * groups
    grp_of_lane = lax.broadcasted_iota(jnp.int32, (kr, d_inner), 1) // gw
    krow = lax.broadcasted_iota(jnp.int32, (kr, d_inner), 0)
    ltf = (jnp.where(grp_of_lane == krow, jnp.broadcast_to(xdt, (kr, d_inner)), 0.0)
           + jnp.where(krow == groups, jnp.broadcast_to(dae, (kr, d_inner)), 0.0))
    lt_hi, lt_lo = _split_hi_lo(ltf)
    lt = jnp.concatenate([lt_hi, lt_lo], axis=0)
    bmat = b_ref[0].astype(F32)
    rrow = lax.broadcasted_iota(jnp.int32, (groups, 2 * ns), 0)
    rlane = lax.broadcasted_iota(jnp.int32, (groups, 2 * ns), 1)
    r_top = jnp.concatenate([bmat, jnp.zeros_like(bmat)], axis=1)
    r_bot = jnp.where((rrow == 0) & (rlane >= ns), 1.0, 0.0)
    rtf = jnp.concatenate([r_top, r_bot], axis=0)
    rt = jnp.concatenate([rtf, rtf], axis=0).astype(BF16)
    both2 = lax.dot_general(lt, rt, (((0,), (0,)), ((), ())), preferred_element_type=F32)
    new = st_ref[0] * both2[:, ns:] + both2[:, :ns]
    st_out_ref[0] = new
    yg = _dot_nt(c_ref[0], new.astype(BF16))
    own = (lax.broadcasted_iota(jnp.int32, (groups, d_inner), 1) // gw
           == lax.broadcasted_iota(jnp.int32, (groups, d_inner), 0))
    y = jnp.sum(jnp.where(own, yg, 0.0), axis=0, keepdims=True)
    y = y + dexp_ref[...] * xs
    yz = y * z_ref[0].astype(F32)
    ms = jnp.mean(yz * yz, axis=-1, keepdims=True)
    ym_ref[0] = (yz * lax.rsqrt(ms + EPS) * nw_ref[...]).astype(ym_ref.dtype)


def _ssd_sample(state, xs, bmat, cmat, dt, zact, a_log, d_exp, norm_w, rep2, *, heads, hd, groups, ns):
    ts = state.shape[0]
    d_inner = heads * hd
    hpg = heads // groups
    seq_spec = lambda shape: pl.BlockSpec((1,) + shape, lambda b: (b, 0, 0))
    const = lambda shape: pl.BlockSpec(shape, lambda b: (0, 0))
    return pl.pallas_call(
        functools.partial(_ssd_sample_kernel, groups=groups, hpg=hpg, hd=hd, ns=ns),
        grid=(ts,),
        in_specs=[
            seq_spec((d_inner, ns)), seq_spec((1, d_inner)), seq_spec((groups, ns)), seq_spec((groups, ns)),
            seq_spec((1, heads)), seq_spec((1, d_inner)),
            const((1, heads)), const((1, d_inner)), const((1, d_inner)), const((2 * heads, d_inner)),
        ],
        out_specs=[seq_spec((d_inner, ns)), seq_spec((1, d_inner))],
        out_shape=[
            jax.ShapeDtypeStruct((ts, d_inner, ns), F32),
            jax.ShapeDtypeStruct((ts, 1, d_inner), F32),
        ],
        compiler_params=_params(1),
        name="ssd_sample",
    )(state, xs, bmat, cmat, dt, zact, a_log.reshape(1, heads), d_exp, norm_w.reshape(1, d_inner), rep2)


def _merge_kernel(*refs, n_p, bm, ts, has_xs, has_prev, has_wbf):
    x_ref = refs[0]
    k = 1
    xs_ref = x_ref
    if has_xs:
        xs_ref = refs[k]
        k += 1
    w_ref, ga_ref = refs[k], refs[k + 1]
    k += 2
    prev_ref = None
    if has_prev:
        prev_ref = refs[k]
        k += 1
    o_ref = refs[k]
    i = pl.program_id(1)
    w_ref = _resident_weight(w_ref, refs[k + 1] if has_wbf else None, i)

    def compute(rows, src_ref):
        for r0, r1 in _row_chunks(rows):
            acc = _dot(src_ref[r0:r1, :], w_ref[...]) * ga_ref[r0:r1, :].astype(F32)
            if has_prev:
                acc = acc + prev_ref[r0:r1, :].astype(F32)
            o_ref[r0:r1, :] = acc.astype(o_ref.dtype)

    @pl.when(i < n_p)
    def _():
        compute(bm, x_ref)

    @pl.when(i == n_p)
    def _():
        compute(ts, xs_ref)


def _merge(x, xs, w, gates, gate_col0, prev, *, tp, ts, bm, bn, out_dtype, name):
    kdim = x.shape[1]
    n = w.shape[1]
    n_p = tp // bm
    gcb = gate_col0 // bn
    assert gate_col0 % bn == 0 and n % bn == 0 and tp % bm == 0
    has_xs = xs is not None
    has_prev = prev is not None
    if has_xs:
        in_specs = [pl.BlockSpec((bm, kdim), lambda j, i: (jnp.minimum(i, n_p - 1), 0)),
                    pl.BlockSpec((ts, kdim), lambda j, i: (0, 0))]
        args = [x, xs]
    else:
        in_specs = [pl.BlockSpec((bm, kdim), lambda j, i: (i, 0))]
        args = [x]
    in_specs += [_weight_spec(kdim, bn, 0), pl.BlockSpec((bm, bn), lambda j, i: (i, gcb + j))]
    args += [w, gates]
    if has_prev:
        in_specs.append(pl.BlockSpec((bm, bn), lambda j, i: (i, j)))
        args.append(prev)
    return pl.pallas_call(
        functools.partial(_merge_kernel, n_p=n_p, bm=bm, ts=ts, has_xs=has_xs, has_prev=has_prev,
                          has_wbf=w.dtype != BF16),
        grid=(n // bn, n_p + 1),
        in_specs=in_specs,
        out_specs=pl.BlockSpec((bm, bn), lambda j, i: (i, j)),
        out_shape=jax.ShapeDtypeStruct((tp + ts, n), out_dtype),
        scratch_shapes=_weight_scratch(w, kdim, bn),
        compiler_params=_params(2),
        name=name,
    )(*args)


def _topk_rows(s, k):
    r = s.shape[0]
    rows = lax.broadcasted_iota(jnp.int32, s.shape, 0).astype(F32)
    vals, idxs = [], []
    for _ in range(k):
        m = jnp.max(s, axis=0, keepdims=True)
        idx = jnp.min(jnp.where(s == m, rows, float(r)), axis=0, keepdims=True)
        vals.append(m)
        idxs.append(idx)
        s = jnp.where(rows == idx, NEG_INF, s)
    return jnp.concatenate(vals, axis=0), jnp.concatenate(idxs, axis=0)


def _pick_rows(table, sel):
    k = table.shape[0]
    out = jnp.zeros_like(sel)
    for r in range(k):
        out = jnp.where(sel == float(r), jnp.broadcast_to(table[r:r + 1, :], sel.shape), out)
    return out


def _staircase_counts(k):
    return [k // (i + 1) for i in range(k)]


def _fold_keys_kernel(wq_ref, k1_ref, k2_ref, o_ref, *, qh, nkeys):
    hp = lax.Precision.HIGHEST
    nt = (((1,), (1,)), ((), ()))
    w = wq_ref[...]
    o_ref[:, 0:nkeys] = lax.dot_general(w[:, :qh], k1_ref[0], nt, precision=hp,
                                        preferred_element_type=F32).astype(o_ref.dtype)
    o_ref[:, nkeys:] = lax.dot_general(w[:, qh:], k2_ref[0], nt, precision=hp,
                                       preferred_element_type=F32).astype(o_ref.dtype)


def _fold_keys(wq, keys):
    d = wq.shape[0]
    _, heads, nkeys, qh = keys.shape
    return pl.pallas_call(
        functools.partial(_fold_keys_kernel, qh=qh, nkeys=nkeys),
        grid=(heads,),
        in_specs=[
            pl.BlockSpec((d, 2 * qh), lambda hh: (0, hh)),
            pl.BlockSpec((1, nkeys, qh), lambda hh: (hh, 0, 0)),
            pl.BlockSpec((1, nkeys, qh), lambda hh: (hh, 0, 0)),
        ],
        out_specs=pl.BlockSpec((d, 2 * nkeys), lambda hh: (0, hh)),
        out_shape=jax.ShapeDtypeStruct((d, heads * 2 * nkeys), BF16),
        compiler_params=_params(1),
        name="peer_fold_keys",
    )(wq, keys[0], keys[1])


def _route_kernel(h_ref, wk_ref, ia_ref, ib_ref, gt_ref, s1_ref, s2_ref, canda_ref, candb_ref,
                  *, nkeys, lanes):
    k = PEER_TOPK
    s = _dot(h_ref[...], wk_ref[...])
    s1_ref[...] = s[:, :nkeys].T
    s2_ref[...] = s[:, nkeys:].T
    counts = _staircase_counts(k)
    starts = [sum(counts[:i]) for i in range(k)]
    n_cand = sum(counts)

    def one_chunk(ci, cand_ref):
        off = ci * lanes if isinstance(ci, int) else pl.multiple_of(ci * lanes, lanes)
        v1, i1 = _topk_rows(s1_ref[:, pl.ds(off, lanes)], k)
        v2, i2 = _topk_rows(s2_ref[:, pl.ds(off, lanes)], k)
        for i in range(k):
            cand_ref[starts[i]:starts[i] + counts[i], :] = v1[i:i + 1, :] + v2[0:counts[i], :]
        cand_ref[n_cand:, :] = jnp.full((cand_ref.shape[0] - n_cand, lanes), NEG_INF, F32)
        sv, sp = _topk_rows(cand_ref[...], k)
        e = jnp.exp(sv - sv[0:1, :])
        gt_ref[0, :, pl.ds(off, lanes)] = e / jnp.sum(e, axis=0, keepdims=True)
        sel_i = jnp.zeros_like(sp)
        sel_start = jnp.zeros_like(sp)
        for i in range(1, k):
            ge = sp >= float(starts[i])
            sel_i = sel_i + jnp.where(ge, 1.0, 0.0)
            sel_start = sel_start + jnp.where(ge, float(counts[i - 1]), 0.0)
        ia_ref[0, :, pl.ds(off, lanes)] = _pick_rows(i1, sel_i)
        ib_ref[0, :, pl.ds(off, lanes)] = _pick_rows(i2, sp - sel_start)

    n_chunks = s1_ref.shape[1] // lanes

    def chunk_pair(pi, carry):
        one_chunk(2 * pi, canda_ref)
        one_chunk(2 * pi + 1, candb_ref)
        return carry

    lax.fori_loop(0, n_chunks // 2, chunk_pair, 0)
    if n_chunks % 2:
        one_chunk(n_chunks - 1, canda_ref)


def _route(h2, wk, *, heads, nkeys, tq):
    t, d = h2.shape
    k = PEER_TOPK
    out = jax.ShapeDtypeStruct((heads, k, t), F32)
    out_spec = pl.BlockSpec((1, k, tq), lambda i, hh: (hh, 0, i))
    return pl.pallas_call(
        functools.partial(_route_kernel, nkeys=nkeys, lanes=128),
        grid=(t // tq, heads),
        in_specs=[
            pl.BlockSpec((tq, d), lambda i, hh: (i, 0)),
            pl.BlockSpec((d, 2 * nkeys), lambda i, hh: (0, hh)),
        ],
        out_specs=[out_spec, out_spec, out_spec],
        out_shape=[out, out, out],
        scratch_shapes=[pltpu.VMEM((nkeys, tq), F32), pltpu.VMEM((nkeys, tq), F32)]
        + [pltpu.VMEM((-(-sum(_staircase_counts(k)) // 8) * 8, 128), F32)] * 2,
        compiler_params=_params(2),
        name="peer_route",
    )(h2, wk)


_SCATTER_UNROLL = 8
_SCATTER_PITCH = 136
_PACK_ROWS = 16


def _scatter_kernel(ia_ref, ib_ref, gt_ref, w_ref, ia_s, ib_s, gt_s, wsc, *, nkeys):
    tw = ia_ref.shape[1]
    ia_s[...] = ia_ref[...].T
    ib_s[...] = ib_ref[...].T
    gt_s[...] = gt_ref[...].T
    nsel = ia_ref.shape[0]
    key_id = lax.broadcasted_iota(jnp.int32, (nkeys, nsel), 0).astype(F32)

    def token_group(gi, carry):
        for k in range(_SCATTER_UNROLL):
            tk = gi * _SCATTER_UNROLL + k
            a_row = ia_s[pl.ds(tk, 1), :]
            b_row = ib_s[pl.ds(tk, 1), :]
            g_row = gt_s[pl.ds(tk, 1), :]
            at = jnp.where(key_id == a_row, jnp.broadcast_to(g_row, key_id.shape), 0.0).astype(BF16)
            bt = jnp.where(key_id == b_row, 1.0, 0.0).astype(BF16)
            wsc[pl.ds(pl.multiple_of(tk * _SCATTER_PITCH, 8), nkeys), :] = _dot_nt(at, bt)
        return carry

    lax.fori_loop(0, tw // _SCATTER_UNROLL, token_group, 0)

    def relayout(tg, carry):
        row0 = pl.multiple_of(tg * _PACK_ROWS, _PACK_ROWS)
        base = tg * (_PACK_ROWS * _SCATTER_PITCH)
        for a in range(nkeys):
            rows = wsc[pl.ds(base + a, _PACK_ROWS, stride=_SCATTER_PITCH), :]
            w_ref[pl.ds(row0, _PACK_ROWS), a * nkeys:(a + 1) * nkeys] = rows.astype(w_ref.dtype)
        return carry

    lax.fori_loop(0, tw // _PACK_ROWS, relayout, 0)


def _scatter(ia, ib, gt, *, nkeys, tw):
    nsel, t = ia.shape
    assert tw % _SCATTER_UNROLL == 0 and tw % _PACK_ROWS == 0
    spec = pl.BlockSpec((nsel, tw), lambda i: (0, i))
    return pl.pallas_call(
        functools.partial(_scatter_kernel, nkeys=nkeys),
        grid=(t // tw,),
        in_specs=[spec, spec, spec],
        out_specs=pl.BlockSpec((tw, nkeys * nkeys), lambda i: (i, 0)),
        out_shape=jax.ShapeDtypeStruct((t, nkeys * nkeys), BF16),
        scratch_shapes=[pltpu.VMEM((tw, nsel), F32)] * 3 + [pltpu.VMEM((tw * _SCATTER_PITCH, nkeys), F32)],
        compiler_params=_params(1),
        name="peer_scatter",
    )(ia, ib, gt)


def _experts_kernel(h_ref, u_ref, v_ref, w_ref, o_ref):
    e = pl.program_id(1)

    @pl.when(e == 0)
    def _():
        o_ref[...] = jnp.zeros(o_ref.shape, F32)

    tm = h_ref.shape[0]
    rc = EXPERT_ROW_CHUNK if tm % EXPERT_ROW_CHUNK == 0 else tm
    for r0 in range(0, tm, rc):
        s = _dot_nt(h_ref[r0:r0 + rc, :], u_ref[...])
        gelu = 0.5 * s * (1.0 + lax.erf(s * (2.0 ** -0.5)))
        act = gelu * w_ref[r0:r0 + rc, :].astype(F32)
        o_ref[r0:r0 + rc, :] += _dot(act.astype(BF16), v_ref[...])


def _experts(h2, u, v, w, *, tm, te):
    t, d = h2.shape
    ne = u.shape[0]
    return pl.pallas_call(
        _experts_kernel,
        grid=(t // tm, ne // te),
        in_specs=[
            pl.BlockSpec((tm, d), lambda i, e: (i, 0), pipeline_mode=pl.Buffered(1)),
            pl.BlockSpec((te, d), lambda i, e: (e, 0)),
            pl.BlockSpec((te, d), lambda i, e: (e, 0)),
            pl.BlockSpec((tm, te), lambda i, e: (i, e)),
        ],
        out_specs=pl.BlockSpec((tm, d), lambda i, e: (i, 0), pipeline_mode=pl.Buffered(1)),
        out_shape=jax.ShapeDtypeStruct((t, d), F32),
        compiler_params=_params(2),
        name="peer_experts",
    )(h2, u, v, w)


def _largest_divisor(n, cap, multiple):
    best = None
    for cand in range(multiple, min(n, cap) + 1, multiple):
        if n % cand == 0:
            best = cand
    assert best is not None, (n, cap, multiple)
    return best


def kernel(x_prompt, x_sample, state_shortconv, state_mamba_conv, state_ssm, ln1_w, w_in, sc_conv_w,
           sc_out_w, m_conv_w, m_conv_b, m_dt_bias, m_A_log, m_D, m_norm_w, m_out_w, w_o, ln2_w,
           peer_wq, peer_keys, peer_u, peer_v, final_norm_w):
    nb, seq, d = x_prompt.shape
    ts = x_sample.shape[0]
    assert x_sample.shape[1] == 1 and ln1_w.shape[0] == 1, "single layer, one new token per sample"
    tp = nb * seq
    t = tp + ts
    d_sc = state_shortconv.shape[-1]
    conv_dim = state_mamba_conv.shape[-1]
    _, _, heads, hd, ns = state_ssm.shape
    d_inner = heads * hd
    groups = (conv_dim - d_inner) // (2 * ns)
    gn = groups * ns
    nkeys = peer_keys.shape[3]
    c_z = 3 * d_sc
    c_xbc = c_z + d_inner
    c_dt = c_xbc + conv_dim
    c_gate = c_dt + heads

    bm = _largest_divisor(seq, 1024, 128)
    bm_small = _largest_divisor(seq, 512, 128)
    bt = _largest_divisor(seq, 256, 8)

    xp2 = x_prompt.reshape(tp, d)
    xs2 = x_sample.reshape(ts, d)
    w_in0 = w_in[0]

    h = _norm_in(xp2, xs2, ln1_w[0], bt=bt)

    g, sc_tail_p, sc_u_s = _sc_branch(h, w_in0, sc_conv_w[0], state_shortconv[0], d_sc=d_sc, tp=tp, ts=ts,
                                      seq=seq, bm=bm, bn=_largest_divisor(d_sc, 256, 128))
    act, mc_tail_p, mc_raw_s = _xbc_branch(h, w_in0, m_conv_w[0], m_conv_b[0], state_mamba_conv[0],
                                           col0=c_xbc, n=conv_dim, tp=tp, ts=ts, seq=seq, bm=bm,
                                           bn=_largest_divisor(conv_dim, 512, 128))
    zact = _mm_act(h, w_in0, col0=c_z, n=d_inner, tp=tp, ts=ts, bm=bm_small,
                   bn=_largest_divisor(d_inner, 1024, 128), act="silu", out_dtype=BF16, name="proj_z")
    dt = _mm_act(h, w_in0, col0=c_dt, n=heads, tp=tp, ts=ts, bm=bm, bn=heads, act="softplus",
                 out_dtype=F32, bias=m_dt_bias[0], name="proj_dt")
    gates = _mm_act(h, w_in0, col0=c_gate, n=2 * d, tp=tp, ts=ts, bm=bm_small,
                    bn=_largest_divisor(d, 1024, 128), act="sigmoid", out_dtype=BF16, name="proj_gates")

    d_exp = jnp.repeat(m_D[0], hd).reshape(1, d_inner)
    rep = (jnp.arange(d_inner)[None, :] // hd == jnp.arange(heads)[:, None]).astype(BF16)
    rep2 = jnp.concatenate([rep, rep], axis=0)
    ym_p, ssm_p = _ssd_prompt(act, dt, zact, m_A_log[0], d_exp, m_norm_w[0], rep2, nb=nb, seq=seq,
                              heads=heads, hd=hd, groups=groups, ns=ns)
    act_s = act[tp:]
    ssm_s, ym_s = _ssd_sample(
        state_ssm[0].reshape(ts, d_inner, ns),
        act_s[:, :d_inner].astype(F32).reshape(ts, 1, d_inner),
        act_s[:, d_inner:d_inner + gn].reshape(ts, groups, ns),
        act_s[:, d_inner + gn:].reshape(ts, groups, ns),
        dt[tp:].reshape(ts, 1, heads),
        zact[tp:].astype(F32).reshape(ts, 1, d_inner),
        m_A_log[0], d_exp, m_norm_w[0], rep2, heads=heads, hd=hd, groups=groups, ns=ns)
    bn_o = _largest_divisor(d, 512, 128)
    mix_a = _merge(g, None, sc_out_w[0], gates, 0, None, tp=tp, ts=ts, bm=bm, bn=bn_o,
                   out_dtype=F32, name="merge_a")
    mixed = _merge(ym_p, ym_s.reshape(ts, d_inner).astype(BF16), m_out_w[0].astype(BF16), gates, d, mix_a,
                   tp=tp, ts=ts, bm=bm_small, bn=bn_o, out_dtype=BF16, name="merge_b")
    x1 = _mm_act(mixed, w_o[0], col0=0, n=d, tp=tp, ts=ts, bm=bm, bn=bn_o, act="none",
                 out_dtype=F32, resid=(xp2, xs2), name="proj_o")

    tok_tile = _largest_divisor(t, 640, 128)
    h2 = _norm_mid(x1, ln2_w[0], bt=_largest_divisor(t, 256, 8))
    wk = _fold_keys(peer_wq[0], peer_keys[0])
    ia, ib, gt = _route(h2, wk, heads=peer_keys.shape[2], nkeys=nkeys, tq=tok_tile)
    nsel = ia.shape[0] * ia.shape[1]
    w = _scatter(ia.reshape(nsel, t), ib.reshape(nsel, t), gt.reshape(nsel, t), nkeys=nkeys, tw=128)
    peer = _experts(h2, peer_u[0].astype(BF16), peer_v[0].astype(BF16), w,
                    tm=tok_tile, te=_largest_divisor(nkeys * nkeys, 512, 128))
    y_p, y_s = _norm_out(x1, peer, final_norm_w, tp=tp, ts=ts, bt=bt)

    new_sc_s = jnp.stack([state_shortconv[0][:, 1], sc_u_s], axis=1)
    new_mc_s = jnp.stack([state_mamba_conv[0][:, 1], state_mamba_conv[0][:, 2], mc_raw_s], axis=1)
    return (
        y_p.reshape(nb, seq, d),
        y_s.reshape(ts, 1, d),
        sc_tail_p[None],
        mc_tail_p[None],
        ssm_p.reshape(1, nb, heads, hd, ns),
        new_sc_s[None],
        new_mc_s[None],
        ssm_s.reshape(1, ts, heads, hd, ns),
    )
```

```python
import functools

import jax
import jax.numpy as jnp
from jax import lax
from jax.experimental import pallas as pl
from jax.experimental.pallas import tpu as pltpu

F32 = jnp.float32
BF16 = jnp.bfloat16
EPS = 1e-6
PEER_TOPK = 16
SSD_CHUNK = 128
VMEM_LIMIT_BYTES = 56 * 1024 * 1024
NEG_INF = float("-inf")
LANES = 128


def _params(n_grid_axes):
    return pltpu.CompilerParams(
        dimension_semantics=("arbitrary",) * n_grid_axes,
        vmem_limit_bytes=VMEM_LIMIT_BYTES,
    )


def _dot(a, b):
    return jnp.dot(a, b, preferred_element_type=F32)


def _dot_nt(a, b):
    return lax.dot_general(a, b, (((1,), (1,)), ((), ())), preferred_element_type=F32)


def _silu(x):
    return x * jax.nn.sigmoid(x)


ROW_CHUNK = 256
EXPERT_ROW_CHUNK = 640


def _row_chunks(rows):
    rc = ROW_CHUNK if rows % ROW_CHUNK == 0 else rows
    return [(r0, r0 + rc) for r0 in range(0, rows, rc)]


def _split_hi_lo(x):
    hi = x.astype(BF16)
    lo = (x - hi.astype(F32)).astype(BF16)
    return hi, lo


def _rms(x, w):
    ms = jnp.mean(x * x, axis=-1, keepdims=True)
    return x * lax.rsqrt(ms + EPS) * w


def _norm_in_kernel(xp_ref, xs_ref, w_ref, o_ref, *, n_p, ts):
    i = pl.program_id(0)

    @pl.when(i < n_p)
    def _():
        o_ref[...] = _rms(xp_ref[...], w_ref[...]).astype(o_ref.dtype)

    @pl.when(i == n_p)
    def _():
        o_ref[0:ts, :] = _rms(xs_ref[...], w_ref[...]).astype(o_ref.dtype)


def _norm_in(xp, xs, w, *, bt):
    tp, d = xp.shape
    ts = xs.shape[0]
    n_p = tp // bt
    return pl.pallas_call(
        functools.partial(_norm_in_kernel, n_p=n_p, ts=ts),
        grid=(n_p + 1,),
        in_specs=[
            pl.BlockSpec((bt, d), lambda i: (jnp.minimum(i, n_p - 1), 0)),
            pl.BlockSpec((ts, d), lambda i: (0, 0)),
            pl.BlockSpec((1, d), lambda i: (0, 0)),
        ],
        out_specs=pl.BlockSpec((bt, d), lambda i: (i, 0)),
        out_shape=jax.ShapeDtypeStruct((tp + ts, d), BF16),
        compiler_params=_params(1),
        name="norm_in",
    )(xp, xs, w.reshape(1, d))


def _norm_mid_kernel(x_ref, w_ref, o_ref):
    o_ref[...] = _rms(x_ref[...], w_ref[...]).astype(o_ref.dtype)


def _norm_mid(x, w, *, bt):
    t, d = x.shape
    return pl.pallas_call(
        _norm_mid_kernel,
        grid=(t // bt,),
        in_specs=[pl.BlockSpec((bt, d), lambda i: (i, 0)), pl.BlockSpec((1, d), lambda i: (0, 0))],
        out_specs=pl.BlockSpec((bt, d), lambda i: (i, 0)),
        out_shape=jax.ShapeDtypeStruct((t, d), BF16),
        compiler_params=_params(1),
        name="norm_mid",
    )(x, w.reshape(1, d))


def _norm_out_kernel(x_ref, y_ref, w_ref, op_ref, os_ref, *, n_p, ts):
    i = pl.program_id(0)

    @pl.when(i < n_p)
    def _():
        op_ref[...] = _rms(x_ref[...] + y_ref[...], w_ref[...])

    @pl.when(i == n_p)
    def _():
        os_ref[...] = _rms(x_ref[0:ts, :] + y_ref[0:ts, :], w_ref[...])


def _norm_out(x, y, w, *, tp, ts, bt):
    d = x.shape[1]
    n_p = tp // bt
    return pl.pallas_call(
        functools.partial(_norm_out_kernel, n_p=n_p, ts=ts),
        grid=(n_p + 1,),
        in_specs=[
            pl.BlockSpec((bt, d), lambda i: (i, 0)),
            pl.BlockSpec((bt, d), lambda i: (i, 0)),
            pl.BlockSpec((1, d), lambda i: (0, 0)),
        ],
        out_specs=[
            pl.BlockSpec((bt, d), lambda i: (jnp.minimum(i, n_p - 1), 0)),
            pl.BlockSpec((ts, d), lambda i: (0, 0)),
        ],
        out_shape=[jax.ShapeDtypeStruct((tp, d), F32), jax.ShapeDtypeStruct((ts, d), F32)],
        compiler_params=_params(1),
        name="norm_out",
    )(x, y, w.reshape(1, d))


def _apply_act(acc, act):
    if act == "silu":
        return _silu(acc)
    if act == "sigmoid":
        return jax.nn.sigmoid(acc)
    if act == "softplus":
        return jax.nn.softplus(acc)
    assert act == "none"
    return acc


def _weight_spec(kdim, bn, col0):
    if col0 % bn == 0:
        return pl.BlockSpec((kdim, bn), lambda j, i: (0, col0 // bn + j))
    assert col0 % LANES == 0 and bn % LANES == 0
    return pl.BlockSpec((pl.Element(kdim), pl.Element(bn)),
                        lambda j, i: (0, (col0 // LANES + j * (bn // LANES)) * LANES))


def _resident_weight(w_ref, wbf_ref, i):
    if wbf_ref is None:
        return w_ref

    @pl.when(i == 0)
    def _():
        wbf_ref[...] = w_ref[...].astype(BF16)

    return wbf_ref


def _weight_scratch(w, kdim, bn):
    return [] if w.dtype == BF16 else [pltpu.VMEM((kdim, bn), BF16)]


def _mm_act_kernel(*refs, n_p, bm, ts, act, has_bias, has_resid, has_wbf):
    x_ref, w_ref = refs[0], refs[1]
    k = 2
    b_ref = None
    if has_bias:
        b_ref = refs[k]
        k += 1
    rp_ref = rs_ref = None
    if has_resid:
        rp_ref, rs_ref = refs[k], refs[k + 1]
        k += 2
    o_ref = refs[k]
    i = pl.program_id(1)
    w_ref = _resident_weight(w_ref, refs[k + 1] if has_wbf else None, i)

    def compute(rows, r_ref):
        for r0, r1 in _row_chunks(rows):
            acc = _dot(x_ref[r0:r1, :], w_ref[...])
            if has_bias:
                acc = acc + b_ref[...]
            acc = _apply_act(acc, act)
            if has_resid:
                acc = acc + r_ref[r0:r1, :]
            o_ref[r0:r1, :] = acc.astype(o_ref.dtype)

    @pl.when(i < n_p)
    def _():
        compute(bm, rp_ref)

    @pl.when(i == n_p)
    def _():
        compute(ts, rs_ref)


def _mm_act(x, w, *, col0, n, tp, ts, bm, bn, act, out_dtype, bias=None, resid=None, name):
    t, kdim = x.shape
    n_p = tp // bm
    assert tp % bm == 0 and ts <= bm and n % bn == 0
    in_specs = [
        pl.BlockSpec((bm, kdim), lambda j, i: (i, 0)),
        _weight_spec(kdim, bn, col0),
    ]
    args = [x, w]
    if bias is not None:
        in_specs.append(pl.BlockSpec((1, bn), lambda j, i: (0, j)))
        args.append(bias.reshape(1, n))
    if resid is not None:
        in_specs.append(pl.BlockSpec((bm, bn), lambda j, i: (jnp.minimum(i, n_p - 1), j)))
        in_specs.append(pl.BlockSpec((ts, bn), lambda j, i: (0, j)))
        args.extend(resid)
    return pl.pallas_call(
        functools.partial(_mm_act_kernel, n_p=n_p, bm=bm, ts=ts, act=act,
                          has_bias=bias is not None, has_resid=resid is not None,
                          has_wbf=w.dtype != BF16),
        grid=(n // bn, n_p + 1),
        in_specs=in_specs,
        out_specs=pl.BlockSpec((bm, bn), lambda j, i: (i, j)),
        out_shape=jax.ShapeDtypeStruct((t, n), out_dtype),
        scratch_shapes=_weight_scratch(w, kdim, bn),
        compiler_params=_params(2),
        name=name,
    )(*args)


_CARRY = 8


def _conv_reset(ubuf, first_in_seq):
    @pl.when(first_in_seq)
    def _():
        ubuf[0:_CARRY, :] = jnp.zeros((_CARRY, ubuf.shape[1]), F32)


def _conv_rows(ubuf, cw_ref, u, r0, *, taps):
    rc = u.shape[0]
    ubuf[_CARRY + r0:_CARRY + r0 + rc, :] = u
    return _conv_taps(ubuf, cw_ref, u, r0, taps=taps)


def _conv_taps(ubuf, cw_ref, u, r0, *, taps):
    rc = u.shape[0]
    ext = ubuf[r0:r0 + _CARRY + rc, :]
    y = None
    for k in range(taps - 1):
        shifted = pltpu.roll(ext, shift=taps - 1 - k, axis=0)[_CARRY:_CARRY + rc, :]
        term = shifted * cw_ref[k:k + 1, :]
        y = term if y is None else y + term
    y = y + u * cw_ref[taps - 1:taps, :]
    return y


def _sc_kernel(x_ref, wb_ref, wc_ref, wx_ref, cw_ref, s0_ref, s1_ref,
               g_ref, tail_ref, us_ref, ubuf, wcat, *, n_p, bm, ts, tps):
    i = pl.program_id(1)
    bn = wb_ref.shape[1]

    @pl.when(i == 0)
    def _():
        wcat[:, 0:bn] = wb_ref[...].astype(BF16)
        wcat[:, bn:2 * bn] = wc_ref[...].astype(BF16)
        wcat[:, 2 * bn:3 * bn] = wx_ref[...].astype(BF16)

    @pl.when(i < n_p)
    def _():
        _conv_reset(ubuf, (i % tps) == 0)
        for r0, r1 in _row_chunks(bm):
            p = _dot(x_ref[r0:r1, :], wcat[...])
            u = p[:, bn:2 * bn] * p[:, 2 * bn:3 * bn]
            y = _conv_rows(ubuf, cw_ref, u, r0, taps=3)
            g_ref[r0:r1, :] = (p[:, 0:bn] * y).astype(g_ref.dtype)
        tail_ref[0] = ubuf[_CARRY + bm - 2:_CARRY + bm, :]
        ubuf[0:_CARRY, :] = ubuf[bm:bm + _CARRY, :]

    @pl.when(i == n_p)
    def _():
        p = _dot(x_ref[0:ts, :], wcat[...])
        u = p[:, bn:2 * bn] * p[:, 2 * bn:3 * bn]
        us_ref[...] = u
        y = s0_ref[...] * cw_ref[0:1, :]
        y = y + s1_ref[...] * cw_ref[1:2, :]
        y = y + u * cw_ref[2:3, :]
        g_ref[0:ts, :] = (p[:, 0:bn] * y).astype(g_ref.dtype)


def _sc_branch(h, w_in, conv_w, state, *, d_sc, tp, ts, seq, bm, bn):
    t, kdim = h.shape
    n_p = tp // bm
    tps = seq // bm
    nb = tp // seq
    nj = d_sc // bn
    st = state.reshape(ts, 2 * d_sc)
    w_spec = lambda off: pl.BlockSpec((kdim, bn), lambda j, i: (0, off + j))
    return pl.pallas_call(
        functools.partial(_sc_kernel, n_p=n_p, bm=bm, ts=ts, tps=tps),
        grid=(nj, n_p + 1),
        in_specs=[
            pl.BlockSpec((bm, kdim), lambda j, i: (i, 0)),
            w_spec(0), w_spec(nj), w_spec(2 * nj),
            pl.BlockSpec((3, bn), lambda j, i: (0, j)),
            pl.BlockSpec((ts, bn), lambda j, i: (0, j)),
            pl.BlockSpec((ts, bn), lambda j, i: (0, nj + j)),
        ],
        out_specs=[
            pl.BlockSpec((bm, bn), lambda j, i: (i, j)),
            pl.BlockSpec((1, 2, bn), lambda j, i: (jnp.minimum(i, n_p - 1) // tps, 0, j)),
            pl.BlockSpec((ts, bn), lambda j, i: (0, j)),
        ],
        out_shape=[
            jax.ShapeDtypeStruct((t, d_sc), BF16),
            jax.ShapeDtypeStruct((nb, 2, d_sc), F32),
            jax.ShapeDtypeStruct((ts, d_sc), F32),
        ],
        scratch_shapes=[pltpu.VMEM((_CARRY + bm, bn), F32), pltpu.VMEM((kdim, 3 * bn), BF16)],
        compiler_params=_params(2),
        name="sc_branch",
    )(h, w_in, w_in, w_in, conv_w, st, st)


def _xbc_kernel(x_ref, w_ref, cw_ref, cb_ref, s0_ref, s1_ref, s2_ref,
                a_ref, tail_ref, rs_ref, ubuf, *wbf, n_p, bm, ts, tps):
    i = pl.program_id(1)
    w_ref = _resident_weight(w_ref, wbf[0] if wbf else None, i)

    @pl.when(i < n_p)
    def _():
        _conv_reset(ubuf, (i % tps) == 0)

        def epilogue(r0, r1):
            raw = ubuf[_CARRY + r0:_CARRY + r1, :]
            y = _conv_taps(ubuf, cw_ref, raw, r0, taps=4)
            a_ref[r0:r1, :] = _silu(y + cb_ref[...]).astype(a_ref.dtype)

        chunks = _row_chunks(bm)
        for ci, (r0, r1) in enumerate(chunks):
            ubuf[_CARRY + r0:_CARRY + r1, :] = _dot(x_ref[r0:r1, :], w_ref[...])
            if ci > 0:
                epilogue(*chunks[ci - 1])
        epilogue(*chunks[-1])
        tail_ref[0] = ubuf[_CARRY + bm - 3:_CARRY + bm, :]
        ubuf[0:_CARRY, :] = ubuf[bm:bm + _CARRY, :]

    @pl.when(i == n_p)
    def _():
        r = _dot(x_ref[0:ts, :], w_ref[...])
        rs_ref[...] = r
        y = s0_ref[...] * cw_ref[0:1, :]
        y = y + s1_ref[...] * cw_ref[1:2, :]
        y = y + s2_ref[...] * cw_ref[2:3, :]
        y = y + r * cw_ref[3:4, :]
        a_ref[0:ts, :] = _silu(y + cb_ref[...]).astype(a_ref.dtype)


def _xbc_branch(h, w_in, conv_w, conv_b, state, *, col0, n, tp, ts, seq, bm, bn):
    t, kdim = h.shape
    n_p = tp // bm
    tps = seq // bm
    nb = tp // seq
    nj = n // bn
    cb = col0 // bn
    assert col0 % bn == 0
    st = state.reshape(ts, 3 * n)
    return pl.pallas_call(
        functools.partial(_xbc_kernel, n_p=n_p, bm=bm, ts=ts, tps=tps),
        grid=(nj, n_p + 1),
        in_specs=[
            pl.BlockSpec((bm, kdim), lambda j, i: (i, 0)),
            pl.BlockSpec((kdim, bn), lambda j, i: (0, cb + j)),
            pl.BlockSpec((4, bn), lambda j, i: (0, j)),
            pl.BlockSpec((1, bn), lambda j, i: (0, j)),
            pl.BlockSpec((ts, bn), lambda j, i: (0, j)),
            pl.BlockSpec((ts, bn), lambda j, i: (0, nj + j)),
            pl.BlockSpec((ts, bn), lambda j, i: (0, 2 * nj + j)),
        ],
        out_specs=[
            pl.BlockSpec((bm, bn), lambda j, i: (i, j)),
            pl.BlockSpec((1, 3, bn), lambda j, i: (jnp.minimum(i, n_p - 1) // tps, 0, j)),
            pl.BlockSpec((ts, bn), lambda j, i: (0, j)),
        ],
        out_shape=[
            jax.ShapeDtypeStruct((t, n), BF16),
            jax.ShapeDtypeStruct((nb, 3, n), F32),
            jax.ShapeDtypeStruct((ts, n), F32),
        ],
        scratch_shapes=[pltpu.VMEM((_CARRY + bm, bn), F32)] + _weight_scratch(w_in, kdim, bn),
        compiler_params=_params(2),
        name="xbc_branch",
    )(h, w_in, conv_w, conv_b.reshape(1, n), st, st, st)


def _expand_heads(vals, rep2_ref):
    hi, lo = _split_hi_lo(vals)
    return _dot(jnp.concatenate([hi, lo], axis=1), rep2_ref[...])


def _ssd_prompt_kernel(xs_ref, b_ref, c_ref, dt_ref, z_ref, alog_ref, dexp_ref, nw_ref, rep2_ref,
                       ym_ref, st_out_ref,
                       st_ref, dte_ref, eae_ref, dee_ref, acg_ref, act_ref, y_ref,
                       *, n_chunks, groups, hpg, hd, ns):
    q = SSD_CHUNK
    c = pl.program_id(1)
    gw = hpg * hd

    @pl.when(c == 0)
    def _():
        st_ref[...] = jnp.zeros(st_ref.shape, F32)

    dt = dt_ref[...]
    a = -jnp.exp(alog_ref[...])
    da = dt * a
    row = lax.broadcasted_iota(jnp.int32, (q, q), 0)
    col = lax.broadcasted_iota(jnp.int32, (q, q), 1)
    tril = (row >= col)
    hi, lo = _split_hi_lo(da)
    trilb = jnp.where(tril, 1.0, 0.0).astype(BF16)
    mid = (da - hi.astype(F32) - lo.astype(F32)).astype(BF16)
    acum = _dot(trilb, hi) + _dot(trilb, lo) + _dot(trilb, mid)
    a_last = acum[q - 1:q, :]
    dte_ref[...] = _expand_heads(dt, rep2_ref)
    eae_ref[...] = _expand_heads(jnp.exp(acum), rep2_ref)
    dee_ref[...] = _expand_heads(jnp.exp(a_last - acum), rep2_ref)
    act_ref[...] = acum.T
    acg_ref[0] = acum
    for g in range(1, groups):
        acg_ref[g] = pltpu.roll(acum, shift=acum.shape[1] - g * hpg, axis=1)

    lane = lax.broadcasted_iota(jnp.int32, (q, 2 * hd), 1)
    lo_half = lane < hd

    def group_body(g, carry):
        off = pl.multiple_of(g * gw, gw)
        noff = pl.multiple_of(g * ns, ns)
        xs_g = xs_ref[:, pl.ds(off, gw)].astype(F32)
        b_g = b_ref[:, pl.ds(noff, ns)]
        c_g = c_ref[:, pl.ds(noff, ns)]
        bt = b_g.astype(F32).T.astype(BF16)
        cb = _dot(c_g, bt)
        xdt = xs_g * dte_ref[:, pl.ds(off, gw)]
        xdt_b = xdt.astype(BF16)
        xd_b = (xdt * dee_ref[:, pl.ds(off, gw)]).astype(BF16)
        eae_g = eae_ref[:, pl.ds(off, gw)]
        st_g = st_ref[g]
        y_off = _dot(c_g, st_g.astype(BF16)) * eae_g
        st_ref[g] = st_g * eae_g[q - 1:q, :] + _dot(bt, xd_b)
        ac = acg_ref[g]
        hoff = pl.multiple_of(g * hpg, hpg)
        ac_t = act_ref[pl.ds(hoff, hpg), :]
        for pr in range(hpg // 2):
            ms = []
            for r in (2 * pr, 2 * pr + 1):
                seg = ac[:, r:r + 1] - ac_t[r:r + 1, :]
                ms.append((jnp.where(tril, jnp.exp(seg), 0.0) * cb).astype(BF16))
            lhs = jnp.concatenate(ms, axis=1)
            xp = xdt_b[:, pr * 2 * hd:(pr + 1) * 2 * hd]
            zero = jnp.zeros_like(xp)
            rhs = jnp.concatenate([jnp.where(lo_half, xp, zero), jnp.where(lo_half, zero, xp)], axis=0)
            sl = slice(pr * 2 * hd, (pr + 1) * 2 * hd)
            lanes_pr = pl.ds(pl.multiple_of(off + pr * 2 * hd, 2 * hd), 2 * hd)
            y_pair = _dot(lhs, rhs) + y_off[:, sl] + dexp_ref[:, lanes_pr] * xs_g[:, sl]
            yz = y_pair * z_ref[:, lanes_pr].astype(F32)
            y_ref[:, lanes_pr] = yz
            carry = carry + yz * yz
        return carry

    ssq = lax.fori_loop(0, groups, group_body, jnp.zeros((q, 2 * hd), F32))
    ms = jnp.sum(ssq, axis=-1, keepdims=True) * (1.0 / (groups * gw))
    ym_ref[...] = (y_ref[...] * lax.rsqrt(ms + EPS) * nw_ref[...]).astype(ym_ref.dtype)

    @pl.when(c == n_chunks - 1)
    def _():
        for g in range(groups):
            st_out_ref[0, g * gw:(g + 1) * gw, :] = st_ref[g].T


def _ssd_prompt(act, dt, zact, a_log, d_exp, norm_w, rep2, *, nb, seq, heads, hd, groups, ns):
    q = SSD_CHUNK
    d_inner = heads * hd
    hpg = heads // groups
    gw = hpg * hd
    n_chunks = seq // q
    gn = groups * ns
    assert heads == q and hd * 2 == q and ns == q and d_inner % gn == 0
    row = lambda b, c: b * n_chunks + c
    return pl.pallas_call(
        functools.partial(_ssd_prompt_kernel, n_chunks=n_chunks, groups=groups, hpg=hpg, hd=hd, ns=ns),
        grid=(nb, n_chunks),
        in_specs=[
            pl.BlockSpec((q, d_inner), lambda b, c: (row(b, c), 0)),
            pl.BlockSpec((q, gn), lambda b, c: (row(b, c), d_inner // gn)),
            pl.BlockSpec((q, gn), lambda b, c: (row(b, c), d_inner // gn + 1)),
            pl.BlockSpec((q, heads), lambda b, c: (row(b, c), 0)),
            pl.BlockSpec((q, d_inner), lambda b, c: (row(b, c), 0)),
            pl.BlockSpec((1, heads), lambda b, c: (0, 0)),
            pl.BlockSpec((1, d_inner), lambda b, c: (0, 0)),
            pl.BlockSpec((1, d_inner), lambda b, c: (0, 0)),
            pl.BlockSpec((2 * heads, d_inner), lambda b, c: (0, 0)),
        ],
        out_specs=[
            pl.BlockSpec((q, d_inner), lambda b, c: (row(b, c), 0)),
            pl.BlockSpec((1, d_inner, ns), lambda b, c: (b, 0, 0)),
        ],
        out_shape=[
            jax.ShapeDtypeStruct((nb * seq, d_inner), BF16),
            jax.ShapeDtypeStruct((nb, d_inner, ns), F32),
        ],
        scratch_shapes=[
            pltpu.VMEM((groups, ns, gw), F32),
            pltpu.VMEM((q, d_inner), F32),
            pltpu.VMEM((q, d_inner), F32),
            pltpu.VMEM((q, d_inner), F32),
            pltpu.VMEM((groups, q, heads), F32),
            pltpu.VMEM((heads, q), F32),
            pltpu.VMEM((q, d_inner), F32),
        ],
        compiler_params=_params(2),
        name="ssd_prompt",
    )(act, act, act, dt, zact, a_log.reshape(1, heads), d_exp, norm_w.reshape(1, d_inner), rep2)


def _ssd_sample_kernel(st_ref, xs_ref, b_ref, c_ref, dt_ref, z_ref, alog_ref, dexp_ref, nw_ref, rep2_ref,
                       st_out_ref, ym_ref, *, groups, hpg, hd, ns):
    gw = hpg * hd
    d_inner = groups * gw
    xs = xs_ref[0].astype(F32)
    dt = dt_ref[0]
    a = -jnp.exp(alog_ref[...])
    pad = jnp.zeros((6, dt.shape[1]), F32)
    both = _expand_heads(jnp.concatenate([dt, jnp.exp(dt * a), pad], axis=0), rep2_ref)
    dte = both[0:1, :]
    dae = both[1:2, :]
    xdt = xs * dte
    kr = 2 * groups
    grp_of_lane = lax.broadcasted_iota(jnp.int32, (kr, d_inner), 1) // gw
    krow = lax.broadcasted_iota(jnp.int32, (kr, d_inner), 0)
    ltf = (jnp.where(grp_of_lane == krow, jnp.broadcast_to(xdt, (kr, d_inner)), 0.0)
           + jnp.where(krow == groups, jnp.broadcast_to(dae, (kr, d_inner)), 0.0))
    lt_hi, lt_lo = _split_hi_lo(ltf)
    lt = jnp.concatenate([lt_hi, lt_lo], axis=0)
    bmat = b_ref[0].astype(F32)
    rrow = lax.broadcasted_iota(jnp.int32, (groups, 2 * ns), 0)
    rlane = lax.broadcasted_iota(jnp.int32, (groups, 2 * ns), 1)
    r_top = jnp.concatenate([bmat, jnp.zeros_like(bmat)], axis=1)
    r_bot = jnp.where((rrow == 0) & (rlane >= ns), 1.0, 0.0)
    rtf = jnp.concatenate([r_top, r_bot], axis=0)
    rt = jnp.concatenate([rtf, rtf], axis=0).astype(BF16)
    both2 = lax.dot_general(lt, rt, (((0,), (0,)), ((), ())), preferred_element_type=F32)
    new = st_ref[0] * both2[:, ns:] + both2[:, :ns]
    st_out_ref[0] = new
    yg = _dot_nt(c_ref[0], new.astype(BF16))
    own = (lax.broadcasted_iota(jnp.int32, (groups, d_inner), 1) // gw
           == lax.broadcasted_iota(jnp.int32, (groups, d_inner), 0))
    y = jnp.sum(jnp.where(own, yg, 0.0), axis=0, keepdims=True)
    y = y + dexp_ref[...] * xs
    yz = y * z_ref[0].astype(F32)
    ms = jnp.mean(yz * yz, axis=-1, keepdims=True)
    ym_ref[0] = (yz * lax.rsqrt(ms + EPS) * nw_ref[...]).astype(ym_ref.dtype)


def _ssd_sample(state, xs, bmat, cmat, dt, zact, a_log, d_exp, norm_w, rep2, *, heads, hd, groups, ns):
    ts = state.shape[0]
    d_inner = heads * hd
    hpg = heads // groups
    seq_spec = lambda shape: pl.BlockSpec((1,) + shape, lambda b: (b, 0, 0))
    const = lambda shape: pl.BlockSpec(shape, lambda b: (0, 0))
    return pl.pallas_call(
        functools.partial(_ssd_sample_kernel, groups=groups, hpg=hpg, hd=hd, ns=ns),
        grid=(ts,),
        in_specs=[
            seq_spec((d_inner, ns)), seq_spec((1, d_inner)), seq_spec((groups, ns)), seq_spec((groups, ns)),
            seq_spec((1, heads)), seq_spec((1, d_inner)),
            const((1, heads)), const((1, d_inner)), const((1, d_inner)), const((2 * heads, d_inner)),
        ],
        out_specs=[seq_spec((d_inner, ns)), seq_spec((1, d_inner))],
        out_shape=[
            jax.ShapeDtypeStruct((ts, d_inner, ns), F32),
            jax.ShapeDtypeStruct((ts, 1, d_inner), F32),
        ],
        compiler_params=_params(1),
        name="ssd_sample",
    )(state, xs, bmat, cmat, dt, zact, a_log.reshape(1, heads), d_exp, norm_w.reshape(1, d_inner), rep2)


def _merge_kernel(*refs, n_p, bm, ts, has_xs, has_prev, has_wbf):
    x_ref = refs[0]
    k = 1
    xs_ref = x_ref
    if has_xs:
        xs_ref = refs[k]
        k += 1
    w_ref, ga_ref = refs[k], refs[k + 1]
    k += 2
    prev_ref = None
    if has_prev:
        prev_ref = refs[k]
        k += 1
    o_ref = refs[k]
    i = pl.program_id(1)
    w_ref = _resident_weight(w_ref, refs[k + 1] if has_wbf else None, i)

    def compute(rows, src_ref):
        for r0, r1 in _row_chunks(rows):
            acc = _dot(src_ref[r0:r1, :], w_ref[...]) * ga_ref[r0:r1, :].astype(F32)
            if has_prev:
                acc = acc + prev_ref[r0:r1, :].astype(F32)
            o_ref[r0:r1, :] = acc.astype(o_ref.dtype)

    @pl.when(i < n_p)
    def _():
        compute(bm, x_ref)

    @pl.when(i == n_p)
    def _():
        compute(ts, xs_ref)


def _merge(x, xs, w, gates, gate_col0, prev, *, tp, ts, bm, bn, out_dtype, name):
    kdim = x.shape[1]
    n = w.shape[1]
    n_p = tp // bm
    gcb = gate_col0 // bn
    assert gate_col0 % bn == 0 and n % bn == 0 and tp % bm == 0
    has_xs = xs is not None
    has_prev = prev is not None
    if has_xs:
        in_specs = [pl.BlockSpec((bm, kdim), lambda j, i: (jnp.minimum(i, n_p - 1), 0)),
                    pl.BlockSpec((ts, kdim), lambda j, i: (0, 0))]
        args = [x, xs]
    else:
        in_specs = [pl.BlockSpec((bm, kdim), lambda j, i: (i, 0))]
        args = [x]
    in_specs += [_weight_spec(kdim, bn, 0), pl.BlockSpec((bm, bn), lambda j, i: (i, gcb + j))]
    args += [w, gates]
    if has_prev:
        in_specs.append(pl.BlockSpec((bm, bn), lambda j, i: (i, j)))
        args.append(prev)
    return pl.pallas_call(
        functools.partial(_merge_kernel, n_p=n_p, bm=bm, ts=ts, has_xs=has_xs, has_prev=has_prev,
                          has_wbf=w.dtype != BF16),
        grid=(n // bn, n_p + 1),
        in_specs=in_specs,
        out_specs=pl.BlockSpec((bm, bn), lambda j, i: (i, j)),
        out_shape=jax.ShapeDtypeStruct((tp + ts, n), out_dtype),
        scratch_shapes=_weight_scratch(w, kdim, bn),
        compiler_params=_params(2),
        name=name,
    )(*args)


def _topk_rows(s, k):
    r = s.shape[0]
    rows = lax.broadcasted_iota(jnp.int32, s.shape, 0).astype(F32)
    vals, idxs = [], []
    for _ in range(k):
        m = jnp.max(s, axis=0, keepdims=True)
        idx = jnp.min(jnp.where(s == m, rows, float(r)), axis=0, keepdims=True)
        vals.append(m)
        idxs.append(idx)
        s = jnp.where(rows == idx, NEG_INF, s)
    return jnp.concatenate(vals, axis=0), jnp.concatenate(idxs, axis=0)


def _pick_rows(table, sel):
    k = table.shape[0]
    out = jnp.zeros_like(sel)
    for r in range(k):
        out = jnp.where(sel == float(r), jnp.broadcast_to(table[r:r + 1, :], sel.shape), out)
    return out


def _staircase_counts(k):
    return [k // (i + 1) for i in range(k)]


def _fold_keys_kernel(wq_ref, k1_ref, k2_ref, o_ref, *, qh, nkeys):
    hp = lax.Precision.HIGHEST
    nt = (((1,), (1,)), ((), ()))
    w = wq_ref[...]
    o_ref[:, 0:nkeys] = lax.dot_general(w[:, :qh], k1_ref[0], nt, precision=hp,
                                        preferred_element_type=F32).astype(o_ref.dtype)
    o_ref[:, nkeys:] = lax.dot_general(w[:, qh:], k2_ref[0], nt, precision=hp,
                                       preferred_element_type=F32).astype(o_ref.dtype)


def _fold_keys(wq, keys):
    d = wq.shape[0]
    _, heads, nkeys, qh = keys.shape
    return pl.pallas_call(
        functools.partial(_fold_keys_kernel, qh=qh, nkeys=nkeys),
        grid=(heads,),
        in_specs=[
            pl.BlockSpec((d, 2 * qh), lambda hh: (0, hh)),
            pl.BlockSpec((1, nkeys, qh), lambda hh: (hh, 0, 0)),
            pl.BlockSpec((1, nkeys, qh), lambda hh: (hh, 0, 0)),
        ],
        out_specs=pl.BlockSpec((d, 2 * nkeys), lambda hh: (0, hh)),
        out_shape=jax.ShapeDtypeStruct((d, heads * 2 * nkeys), BF16),
        compiler_params=_params(1),
        name="peer_fold_keys",
    )(wq, keys[0], keys[1])


def _route_kernel(h_ref, wk_ref, ia_ref, ib_ref, gt_ref, s1_ref, s2_ref, canda_ref, candb_ref,
                  *, nkeys, lanes):
    k = PEER_TOPK
    s = _dot(h_ref[...], wk_ref[...])
    s1_ref[...] = s[:, :nkeys].T
    s2_ref[...] = s[:, nkeys:].T
    counts = _staircase_counts(k)
    starts = [sum(counts[:i]) for i in range(k)]
    n_cand = sum(counts)

    def one_chunk(ci, cand_ref):
        off = ci * lanes if isinstance(ci, int) else pl.multiple_of(ci * lanes, lanes)
        v1, i1 = _topk_rows(s1_ref[:, pl.ds(off, lanes)], k)
        v2, i2 = _topk_rows(s2_ref[:, pl.ds(off, lanes)], k)
        for i in range(k):
            cand_ref[starts[i]:starts[i] + counts[i], :] = v1[i:i + 1, :] + v2[0:counts[i], :]
        cand_ref[n_cand:, :] = jnp.full((cand_ref.shape[0] - n_cand, lanes), NEG_INF, F32)
        sv, sp = _topk_rows(cand_ref[...], k)
        e = jnp.exp(sv - sv[0:1, :])
        gt_ref[0, :, pl.ds(off, lanes)] = e / jnp.sum(e, axis=0, keepdims=True)
        sel_i = jnp.zeros_like(sp)
        sel_start = jnp.zeros_like(sp)
        for i in range(1, k):
            ge = sp >= float(starts[i])
            sel_i = sel_i + jnp.where(ge, 1.0, 0.0)
            sel_start = sel_start + jnp.where(ge, float(counts[i - 1]), 0.0)
        ia_ref[0, :, pl.ds(off, lanes)] = _pick_rows(i1, sel_i)
        ib_ref[0, :, pl.ds(off, lanes)] = _pick_rows(i2, sp - sel_start)

    n_chunks = s1_ref.shape[1] // lanes

    def chunk_pair(pi, carry):
        one_chunk(2 * pi, canda_ref)
        one_chunk(2 * pi + 1, candb_ref)
        return carry

    lax.fori_loop(0, n_chunks // 2, chunk_pair, 0)
    if n_chunks % 2:
        one_chunk(n_chunks - 1, canda_ref)


def _route(h2, wk, *, heads, nkeys, tq):
    t, d = h2.shape
    k = PEER_TOPK
    out = jax.ShapeDtypeStruct((heads, k, t), F32)
    out_spec = pl.BlockSpec((1, k, tq), lambda i, hh: (hh, 0, i))
    return pl.pallas_call(
        functools.partial(_route_kernel, nkeys=nkeys, lanes=128),
        grid=(t // tq, heads),
        in_specs=[
            pl.BlockSpec((tq, d), lambda i, hh: (i, 0)),
            pl.BlockSpec((d, 2 * nkeys), lambda i, hh: (0, hh)),
        ],
        out_specs=[out_spec, out_spec, out_spec],
        out_shape=[out, out, out],
        scratch_shapes=[pltpu.VMEM((nkeys, tq), F32), pltpu.VMEM((nkeys, tq), F32)]
        + [pltpu.VMEM((-(-sum(_staircase_counts(k)) // 8) * 8, 128), F32)] * 2,
        compiler_params=_params(2),
        name="peer_route",
    )(h2, wk)


_SCATTER_UNROLL = 16
_SCATTER_PITCH = 136
_PACK_ROWS = 16


def _scatter_kernel(ia_ref, ib_ref, gt_ref, w_ref, ia_s, ib_s, gt_s, wsc, *, nkeys):
    tw = ia_ref.shape[1]
    ia_s[...] = ia_ref[...].T
    ib_s[...] = ib_ref[...].T
    gt_s[...] = gt_ref[...].T
    nsel = ia_ref.shape[0]
    key_id = lax.broadcasted_iota(jnp.int32, (nkeys, nsel), 0).astype(F32)

    def token_group(gi, carry):
        for k in range(_SCATTER_UNROLL):
            tk = gi * _SCATTER_UNROLL + k
            a_row = ia_s[pl.ds(tk, 1), :]
            b_row = ib_s[pl.ds(tk, 1), :]
            g_row = gt_s[pl.ds(tk, 1), :]
            at = jnp.where(key_id == a_row, jnp.broadcast_to(g_row, key_id.shape), 0.0).astype(BF16)
            bt = jnp.where(key_id == b_row, 1.0, 0.0).astype(BF16)
            wsc[pl.ds(pl.multiple_of(tk * _SCATTER_PITCH, 8), nkeys), :] = _dot_nt(at, bt)
        return carry

    lax.fori_loop(0, tw // _SCATTER_UNROLL, token_group, 0)

    def relayout(tg, carry):
        row0 = pl.multiple_of(tg * _PACK_ROWS, _PACK_ROWS)
        base = tg * (_PACK_ROWS * _SCATTER_PITCH)
        for a in range(nkeys):
            rows = wsc[pl.ds(base + a, _PACK_ROWS, stride=_SCATTER_PITCH), :]
            w_ref[pl.ds(row0, _PACK_ROWS), a * nkeys:(a + 1) * nkeys] = rows.astype(w_ref.dtype)
        return carry

    lax.fori_loop(0, tw // _PACK_ROWS, relayout, 0)


def _scatter(ia, ib, gt, *, nkeys, tw):
    nsel, t = ia.shape
    assert tw % _SCATTER_UNROLL == 0 and tw % _PACK_ROWS == 0
    spec = pl.BlockSpec((nsel, tw), lambda i: (0, i))
    return pl.pallas_call(
        functools.partial(_scatter_kernel, nkeys=nkeys),
        grid=(t // tw,),
        in_specs=[spec, spec, spec],
        out_specs=pl.BlockSpec((tw, nkeys * nkeys), lambda i: (i, 0)),
        out_shape=jax.ShapeDtypeStruct((t, nkeys * nkeys), BF16),
        scratch_shapes=[pltpu.VMEM((tw, nsel), F32)] * 3 + [pltpu.VMEM((tw * _SCATTER_PITCH, nkeys), F32)],
        compiler_params=_params(1),
        name="peer_scatter",
    )(ia, ib, gt)


def _experts_kernel(h_ref, u_ref, v_ref, w_ref, o_ref):
    e = pl.program_id(1)

    @pl.when(e == 0)
    def _():
        o_ref[...] = jnp.zeros(o_ref.shape, F32)

    tm = h_ref.shape[0]
    rc = EXPERT_ROW_CHUNK if tm % EXPERT_ROW_CHUNK == 0 else tm
    for r0 in range(0, tm, rc):
        s = _dot_nt(h_ref[r0:r0 + rc, :], u_ref[...])
        gelu = 0.5 * s * (1.0 + lax.erf(s * (2.0 ** -0.5)))
        act = gelu * w_ref[r0:r0 + rc, :].astype(F32)
        o_ref[r0:r0 + rc, :] += _dot(act.astype(BF16), v_ref[...])


def _experts(h2, u, v, w, *, tm, te):
    t, d = h2.shape
    ne = u.shape[0]
    return pl.pallas_call(
        _experts_kernel,
        grid=(t // tm, ne // te),
        in_specs=[
            pl.BlockSpec((tm, d), lambda i, e: (i, 0), pipeline_mode=pl.Buffered(1)),
            pl.BlockSpec((te, d), lambda i, e: (e, 0)),
            pl.BlockSpec((te, d), lambda i, e: (e, 0)),
            pl.BlockSpec((tm, te), lambda i, e: (i, e)),
        ],
        out_specs=pl.BlockSpec((tm, d), lambda i, e: (i, 0), pipeline_mode=pl.Buffered(1)),
        out_shape=jax.ShapeDtypeStruct((t, d), F32),
        compiler_params=_params(2),
        name="peer_experts",
    )(h2, u, v, w)


def _largest_divisor(n, cap, multiple):
    best = None
    for cand in range(multiple, min(n, cap) + 1, multiple):
        if n % cand == 0:
            best = cand
    assert best is not None, (n, cap, multiple)
    return best


def kernel(x_prompt, x_sample, state_shortconv, state_mamba_conv, state_ssm, ln1_w, w_in, sc_conv_w,
           sc_out_w, m_conv_w, m_conv_b, m_dt_bias, m_A_log, m_D, m_norm_w, m_out_w, w_o, ln2_w,
           peer_wq, peer_keys, peer_u, peer_v, final_norm_w):
    nb, seq, d = x_prompt.shape
    ts = x_sample.shape[0]
    assert x_sample.shape[1] == 1 and ln1_w.shape[0] == 1, "single layer, one new token per sample"
    tp = nb * seq
    t = tp + ts
    d_sc = state_shortconv.shape[-1]
    conv_dim = state_mamba_conv.shape[-1]
    _, _, heads, hd, ns = state_ssm.shape
    d_inner = heads * hd
    groups = (conv_dim - d_inner) // (2 * ns)
    gn = groups * ns
    nkeys = peer_keys.shape[3]
    c_z = 3 * d_sc
    c_xbc = c_z + d_inner
    c_dt = c_xbc + conv_dim
    c_gate = c_dt + heads

    bm = _largest_divisor(seq, 1024, 128)
    bm_small = _largest_divisor(seq, 512, 128)
    bt = _largest_divisor(seq, 256, 8)

    xp2 = x_prompt.reshape(tp, d)
    xs2 = x_sample.reshape(ts, d)
    w_in0 = w_in[0]

    h = _norm_in(xp2, xs2, ln1_w[0], bt=bt)

    g, sc_tail_p, sc_u_s = _sc_branch(h, w_in0, sc_conv_w[0], state_shortconv[0], d_sc=d_sc, tp=tp, ts=ts,
                                      seq=seq, bm=bm, bn=_largest_divisor(d_sc, 256, 128))
    act, mc_tail_p, mc_raw_s = _xbc_branch(h, w_in0, m_conv_w[0], m_conv_b[0], state_mamba_conv[0],
                                           col0=c_xbc, n=conv_dim, tp=tp, ts=ts, seq=seq, bm=bm,
                                           bn=_largest_divisor(conv_dim, 512, 128))
    zact = _mm_act(h, w_in0, col0=c_z, n=d_inner, tp=tp, ts=ts, bm=bm_small,
                   bn=_largest_divisor(d_inner, 1024, 128), act="silu", out_dtype=BF16, name="proj_z")
    dt = _mm_act(h, w_in0, col0=c_dt, n=heads, tp=tp, ts=ts, bm=bm, bn=heads, act="softplus",
                 out_dtype=F32, bias=m_dt_bias[0], name="proj_dt")
    gates = _mm_act(h, w_in0, col0=c_gate, n=2 * d, tp=tp, ts=ts, bm=bm_small,
                    bn=_largest_divisor(d, 1024, 128), act="sigmoid", out_dtype=BF16, name="proj_gates")

    d_exp = jnp.repeat(m_D[0], hd).reshape(1, d_inner)
    rep = (jnp.arange(d_inner)[None, :] // hd == jnp.arange(heads)[:, None]).astype(BF16)
    rep2 = jnp.concatenate([rep, rep], axis=0)
    ym_p, ssm_p = _ssd_prompt(act, dt, zact, m_A_log[0], d_exp, m_norm_w[0], rep2, nb=nb, seq=seq,
                              heads=heads, hd=hd, groups=groups, ns=ns)
    act_s = act[tp:]
    ssm_s, ym_s = _ssd_sample(
        state_ssm[0].reshape(ts, d_inner, ns),
        act_s[:, :d_inner].astype(F32).reshape(ts, 1, d_inner),
        act_s[:, d_inner:d_inner + gn].reshape(ts, groups, ns),
        act_s[:, d_inner + gn:].reshape(ts, groups, ns),
        dt[tp:].reshape(ts, 1, heads),
        zact[tp:].astype(F32).reshape(ts, 1, d_inner),
        m_A_log[0], d_exp, m_norm_w[0], rep2, heads=heads, hd=hd, groups=groups, ns=ns)
    bn_o = _largest_divisor(d, 512, 128)
    mix_a = _merge(g, None, sc_out_w[0], gates, 0, None, tp=tp, ts=ts, bm=bm, bn=bn_o,
                   out_dtype=F32, name="merge_a")
    mixed = _merge(ym_p, ym_s.reshape(ts, d_inner).astype(BF16), m_out_w[0].astype(BF16), gates, d, mix_a,
                   tp=tp, ts=ts, bm=bm_small, bn=bn_o, out_dtype=BF16, name="merge_b")
    x1 = _mm_act(mixed, w_o[0], col0=0, n=d, tp=tp, ts=ts, bm=bm, bn=bn_o, act="none",
                 out_dtype=F32, resid=(xp2, xs2), name="proj_o")

    tok_tile = _largest_divisor(t, 640, 128)
    h2 = _norm_mid(x1, ln2_w[0], bt=_largest_divisor(t, 256, 8))
    wk = _fold_keys(peer_wq[0], peer_keys[0])
    ia, ib, gt = _route(h2, wk, heads=peer_keys.shape[2], nkeys=nkeys, tq=_largest_divisor(t, 1664, 128))
    nsel = ia.shape[0] * ia.shape[1]
    w = _scatter(ia.reshape(nsel, t), ib.reshape(nsel, t), gt.reshape(nsel, t), nkeys=nkeys, tw=128)
    peer = _experts(h2, peer_u[0].astype(BF16), peer_v[0].astype(BF16), w,
                    tm=tok_tile, te=_largest_divisor(nkeys * nkeys, 512, 128))
    y_p, y_s = _norm_out(x1, peer, final_norm_w, tp=tp, ts=ts, bt=bt)

    new_sc_s = jnp.stack([state_shortconv[0][:, 1], sc_u_s], axis=1)
    new_mc_s = jnp.stack([state_mamba_conv[0][:, 1], state_mamba_conv[0][:, 2], mc_raw_s], axis=1)
    return (
        y_p.reshape(nb, seq, d),
        y_s.reshape(ts, 1, d),
        sc_tail_p[None],
        mc_tail_p[None],
        ssm_p.reshape(1, nb, heads, hd, ns),
        new_sc_s[None],
        new_mc_s[None],
        ssm_s.reshape(1, ts, heads, hd, ns),
    )
```

```python
import functools

import jax
import jax.numpy as jnp
from jax import lax
from jax.experimental import pallas as pl
from jax.experimental.pallas import tpu as pltpu

F32 = jnp.float32
BF16 = jnp.bfloat16
EPS = 1e-6
PEER_TOPK = 16
SSD_CHUNK = 128
VMEM_LIMIT_BYTES = 56 * 1024 * 1024
NEG_INF = float("-inf")
LANES = 128


def _params(n_grid_axes):
    return pltpu.CompilerParams(
        dimension_semantics=("arbitrary",) * n_grid_axes,
        vmem_limit_bytes=VMEM_LIMIT_BYTES,
    )


def _dot(a, b):
    return jnp.dot(a, b, preferred_element_type=F32)


def _dot_nt(a, b):
    return lax.dot_general(a, b, (((1,), (1,)), ((), ())), preferred_element_type=F32)


def _silu(x):
    return x * jax.nn.sigmoid(x)


ROW_CHUNK = 256
EXPERT_ROW_CHUNK = 640


def _row_chunks(rows):
    rc = ROW_CHUNK if rows % ROW_CHUNK == 0 else rows
    return [(r0, r0 + rc) for r0 in range(0, rows, rc)]


def _split_hi_lo(x):
    hi = x.astype(BF16)
    lo = (x - hi.astype(F32)).astype(BF16)
    return hi, lo


def _rms(x, w):
    ms = jnp.mean(x * x, axis=-1, keepdims=True)
    return x * lax.rsqrt(ms + EPS) * w


def _norm_in_kernel(xp_ref, xs_ref, w_ref, o_ref, *, n_p, ts):
    i = pl.program_id(0)

    @pl.when(i < n_p)
    def _():
        o_ref[...] = _rms(xp_ref[...], w_ref[...]).astype(o_ref.dtype)

    @pl.when(i == n_p)
    def _():
        o_ref[0:ts, :] = _rms(xs_ref[...], w_ref[...]).astype(o_ref.dtype)


def _norm_in(xp, xs, w, *, bt):
    tp, d = xp.shape
    ts = xs.shape[0]
    n_p = tp // bt
    return pl.pallas_call(
        functools.partial(_norm_in_kernel, n_p=n_p, ts=ts),
        grid=(n_p + 1,),
        in_specs=[
            pl.BlockSpec((bt, d), lambda i: (jnp.minimum(i, n_p - 1), 0)),
            pl.BlockSpec((ts, d), lambda i: (0, 0)),
            pl.BlockSpec((1, d), lambda i: (0, 0)),
        ],
        out_specs=pl.BlockSpec((bt, d), lambda i: (i, 0)),
        out_shape=jax.ShapeDtypeStruct((tp + ts, d), BF16),
        compiler_params=_params(1),
        name="norm_in",
    )(xp, xs, w.reshape(1, d))


def _norm_mid_kernel(x_ref, w_ref, o_ref):
    o_ref[...] = _rms(x_ref[...], w_ref[...]).astype(o_ref.dtype)


def _norm_mid(x, w, *, bt):
    t, d = x.shape
    return pl.pallas_call(
        _norm_mid_kernel,
        grid=(t // bt,),
        in_specs=[pl.BlockSpec((bt, d), lambda i: (i, 0)), pl.BlockSpec((1, d), lambda i: (0, 0))],
        out_specs=pl.BlockSpec((bt, d), lambda i: (i, 0)),
        out_shape=jax.ShapeDtypeStruct((t, d), BF16),
        compiler_params=_params(1),
        name="norm_mid",
    )(x, w.reshape(1, d))


def _norm_out_kernel(x_ref, y_ref, w_ref, op_ref, os_ref, *, n_p, ts):
    i = pl.program_id(0)

    @pl.when(i < n_p)
    def _():
        op_ref[...] = _rms(x_ref[...] + y_ref[...], w_ref[...])

    @pl.when(i == n_p)
    def _():
        os_ref[...] = _rms(x_ref[0:ts, :] + y_ref[0:ts, :], w_ref[...])


def _norm_out(x, y, w, *, tp, ts, bt):
    d = x.shape[1]
    n_p = tp // bt
    return pl.pallas_call(
        functools.partial(_norm_out_kernel, n_p=n_p, ts=ts),
        grid=(n_p + 1,),
        in_specs=[
            pl.BlockSpec((bt, d), lambda i: (i, 0)),
            pl.BlockSpec((bt, d), lambda i: (i, 0)),
            pl.BlockSpec((1, d), lambda i: (0, 0)),
        ],
        out_specs=[
            pl.BlockSpec((bt, d), lambda i: (jnp.minimum(i, n_p - 1), 0)),
            pl.BlockSpec((ts, d), lambda i: (0, 0)),
        ],
        out_shape=[jax.ShapeDtypeStruct((tp, d), F32), jax.ShapeDtypeStruct((ts, d), F32)],
        compiler_params=_params(1),
        name="norm_out",
    )(x, y, w.reshape(1, d))


def _apply_act(acc, act):
    if act == "silu":
        return _silu(acc)
    if act == "sigmoid":
        return jax.nn.sigmoid(acc)
    if act == "softplus":
        return jax.nn.softplus(acc)
    assert act == "none"
    return acc


def _weight_spec(kdim, bn, col0):
    if col0 % bn == 0:
        return pl.BlockSpec((kdim, bn), lambda j, i: (0, col0 // bn + j))
    assert col0 % LANES == 0 and bn % LANES == 0
    return pl.BlockSpec((pl.Element(kdim), pl.Element(bn)),
                        lambda j, i: (0, (col0 // LANES + j * (bn // LANES)) * LANES))


def _resident_weight(w_ref, wbf_ref, i):
    if wbf_ref is None:
        return w_ref

    @pl.when(i == 0)
    def _():
        wbf_ref[...] = w_ref[...].astype(BF16)

    return wbf_ref


def _weight_scratch(w, kdim, bn):
    return [] if w.dtype == BF16 else [pltpu.VMEM((kdim, bn), BF16)]


def _mm_act_kernel(*refs, n_p, bm, ts, act, has_bias, has_resid, has_wbf):
    x_ref, w_ref = refs[0], refs[1]
    k = 2
    b_ref = None
    if has_bias:
        b_ref = refs[k]
        k += 1
    rp_ref = rs_ref = None
    if has_resid:
        rp_ref, rs_ref = refs[k], refs[k + 1]
        k += 2
    o_ref = refs[k]
    i = pl.program_id(1)
    w_ref = _resident_weight(w_ref, refs[k + 1] if has_wbf else None, i)

    def compute(rows, r_ref):
        for r0, r1 in _row_chunks(rows):
            acc = _dot(x_ref[r0:r1, :], w_ref[...])
            if has_bias:
                acc = acc + b_ref[...]
            acc = _apply_act(acc, act)
            if has_resid:
                acc = acc + r_ref[r0:r1, :]
            o_ref[r0:r1, :] = acc.astype(o_ref.dtype)

    @pl.when(i < n_p)
    def _():
        compute(bm, rp_ref)

    @pl.when(i == n_p)
    def _():
        compute(ts, rs_ref)


def _mm_act(x, w, *, col0, n, tp, ts, bm, bn, act, out_dtype, bias=None, resid=None, name):
    t, kdim = x.shape
    n_p = tp // bm
    assert tp % bm == 0 and ts <= bm and n % bn == 0
    in_specs = [
        pl.BlockSpec((bm, kdim), lambda j, i: (i, 0)),
        _weight_spec(kdim, bn, col0),
    ]
    args = [x, w]
    if bias is not None:
        in_specs.append(pl.BlockSpec((1, bn), lambda j, i: (0, j)))
        args.append(bias.reshape(1, n))
    if resid is not None:
        in_specs.append(pl.BlockSpec((bm, bn), lambda j, i: (jnp.minimum(i, n_p - 1), j)))
        in_specs.append(pl.BlockSpec((ts, bn), lambda j, i: (0, j)))
        args.extend(resid)
    return pl.pallas_call(
        functools.partial(_mm_act_kernel, n_p=n_p, bm=bm, ts=ts, act=act,
                          has_bias=bias is not None, has_resid=resid is not None,
                          has_wbf=w.dtype != BF16),
        grid=(n // bn, n_p + 1),
        in_specs=in_specs,
        out_specs=pl.BlockSpec((bm, bn), lambda j, i: (i, j)),
        out_shape=jax.ShapeDtypeStruct((t, n), out_dtype),
        scratch_shapes=_weight_scratch(w, kdim, bn),
        compiler_params=_params(2),
        name=name,
    )(*args)


_CARRY = 8


def _conv_reset(ubuf, first_in_seq):
    @pl.when(first_in_seq)
    def _():
        ubuf[0:_CARRY, :] = jnp.zeros((_CARRY, ubuf.shape[1]), F32)


def _conv_rows(ubuf, cw_ref, u, r0, *, taps):
    rc = u.shape[0]
    ubuf[_CARRY + r0:_CARRY + r0 + rc, :] = u
    return _conv_taps(ubuf, cw_ref, u, r0, taps=taps)


def _conv_taps(ubuf, cw_ref, u, r0, *, taps):
    rc = u.shape[0]
    ext = ubuf[r0:r0 + _CARRY + rc, :]
    y = None
    for k in range(taps - 1):
        shifted = pltpu.roll(ext, shift=taps - 1 - k, axis=0)[_CARRY:_CARRY + rc, :]
        term = shifted * cw_ref[k:k + 1, :]
        y = term if y is None else y + term
    y = y + u * cw_ref[taps - 1:taps, :]
    return y


def _sc_kernel(x_ref, wb_ref, wc_ref, wx_ref, cw_ref, s0_ref, s1_ref,
               g_ref, tail_ref, us_ref, ubuf, wcat, *, n_p, bm, ts, tps):
    i = pl.program_id(1)
    bn = wb_ref.shape[1]

    @pl.when(i == 0)
    def _():
        wcat[:, 0:bn] = wb_ref[...].astype(BF16)
        wcat[:, bn:2 * bn] = wc_ref[...].astype(BF16)
        wcat[:, 2 * bn:3 * bn] = wx_ref[...].astype(BF16)

    @pl.when(i < n_p)
    def _():
        _conv_reset(ubuf, (i % tps) == 0)
        for r0, r1 in _row_chunks(bm):
            p = _dot(x_ref[r0:r1, :], wcat[...])
            u = p[:, bn:2 * bn] * p[:, 2 * bn:3 * bn]
            y = _conv_rows(ubuf, cw_ref, u, r0, taps=3)
            g_ref[r0:r1, :] = (p[:, 0:bn] * y).astype(g_ref.dtype)
        tail_ref[0] = ubuf[_CARRY + bm - 2:_CARRY + bm, :]
        ubuf[0:_CARRY, :] = ubuf[bm:bm + _CARRY, :]

    @pl.when(i == n_p)
    def _():
        p = _dot(x_ref[0:ts, :], wcat[...])
        u = p[:, bn:2 * bn] * p[:, 2 * bn:3 * bn]
        us_ref[...] = u
        y = s0_ref[...] * cw_ref[0:1, :]
        y = y + s1_ref[...] * cw_ref[1:2, :]
        y = y + u * cw_ref[2:3, :]
        g_ref[0:ts, :] = (p[:, 0:bn] * y).astype(g_ref.dtype)


def _sc_branch(h, w_in, conv_w, state, *, d_sc, tp, ts, seq, bm, bn):
    t, kdim = h.shape
    n_p = tp // bm
    tps = seq // bm
    nb = tp // seq
    nj = d_sc // bn
    st = state.reshape(ts, 2 * d_sc)
    w_spec = lambda off: pl.BlockSpec((kdim, bn), lambda j, i: (0, off + j))
    return pl.pallas_call(
        functools.partial(_sc_kernel, n_p=n_p, bm=bm, ts=ts, tps=tps),
        grid=(nj, n_p + 1),
        in_specs=[
            pl.BlockSpec((bm, kdim), lambda j, i: (i, 0)),
            w_spec(0), w_spec(nj), w_spec(2 * nj),
            pl.BlockSpec((3, bn), lambda j, i: (0, j)),
            pl.BlockSpec((ts, bn), lambda j, i: (0, j)),
            pl.BlockSpec((ts, bn), lambda j, i: (0, nj + j)),
        ],
        out_specs=[
            pl.BlockSpec((bm, bn), lambda j, i: (i, j)),
            pl.BlockSpec((1, 2, bn), lambda j, i: (jnp.minimum(i, n_p - 1) // tps, 0, j)),
            pl.BlockSpec((ts, bn), lambda j, i: (0, j)),
        ],
        out_shape=[
            jax.ShapeDtypeStruct((t, d_sc), BF16),
            jax.ShapeDtypeStruct((nb, 2, d_sc), F32),
            jax.ShapeDtypeStruct((ts, d_sc), F32),
        ],
        scratch_shapes=[pltpu.VMEM((_CARRY + bm, bn), F32), pltpu.VMEM((kdim, 3 * bn), BF16)],
        compiler_params=_params(2),
        name="sc_branch",
    )(h, w_in, w_in, w_in, conv_w, st, st)


def _xbc_kernel(x_ref, w_ref, cw_ref, cb_ref, s0_ref, s1_ref, s2_ref,
                a_ref, tail_ref, rs_ref, ubuf, *wbf, n_p, bm, ts, tps):
    i = pl.program_id(1)
    w_ref = _resident_weight(w_ref, wbf[0] if wbf else None, i)

    @pl.when(i < n_p)
    def _():
        _conv_reset(ubuf, (i % tps) == 0)

        def epilogue(r0, r1):
            raw = ubuf[_CARRY + r0:_CARRY + r1, :]
            y = _conv_taps(ubuf, cw_ref, raw, r0, taps=4)
            a_ref[r0:r1, :] = _silu(y + cb_ref[...]).astype(a_ref.dtype)

        chunks = _row_chunks(bm)
        for ci, (r0, r1) in enumerate(chunks):
            ubuf[_CARRY + r0:_CARRY + r1, :] = _dot(x_ref[r0:r1, :], w_ref[...])
            if ci > 0:
                epilogue(*chunks[ci - 1])
        epilogue(*chunks[-1])
        tail_ref[0] = ubuf[_CARRY + bm - 3:_CARRY + bm, :]
        ubuf[0:_CARRY, :] = ubuf[bm:bm + _CARRY, :]

    @pl.when(i == n_p)
    def _():
        r = _dot(x_ref[0:ts, :], w_ref[...])
        rs_ref[...] = r
        y = s0_ref[...] * cw_ref[0:1, :]
        y = y + s1_ref[...] * cw_ref[1:2, :]
        y = y + s2_ref[...] * cw_ref[2:3, :]
        y = y + r * cw_ref[3:4, :]
        a_ref[0:ts, :] = _silu(y + cb_ref[...]).astype(a_ref.dtype)


def _xbc_branch(h, w_in, conv_w, conv_b, state, *, col0, n, tp, ts, seq, bm, bn):
    t, kdim = h.shape
    n_p = tp // bm
    tps = seq // bm
    nb = tp // seq
    nj = n // bn
    cb = col0 // bn
    assert col0 % bn == 0
    st = state.reshape(ts, 3 * n)
    return pl.pallas_call(
        functools.partial(_xbc_kernel, n_p=n_p, bm=bm, ts=ts, tps=tps),
        grid=(nj, n_p + 1),
        in_specs=[
            pl.BlockSpec((bm, kdim), lambda j, i: (i, 0)),
            pl.BlockSpec((kdim, bn), lambda j, i: (0, cb + j)),
            pl.BlockSpec((4, bn), lambda j, i: (0, j)),
            pl.BlockSpec((1, bn), lambda j, i: (0, j)),
            pl.BlockSpec((ts, bn), lambda j, i: (0, j)),
            pl.BlockSpec((ts, bn), lambda j, i: (0, nj + j)),
            pl.BlockSpec((ts, bn), lambda j, i: (0, 2 * nj + j)),
        ],
        out_specs=[
            pl.BlockSpec((bm, bn), lambda j, i: (i, j)),
            pl.BlockSpec((1, 3, bn), lambda j, i: (jnp.minimum(i, n_p - 1) // tps, 0, j)),
            pl.BlockSpec((ts, bn), lambda j, i: (0, j)),
        ],
        out_shape=[
            jax.ShapeDtypeStruct((t, n), BF16),
            jax.ShapeDtypeStruct((nb, 3, n), F32),
            jax.ShapeDtypeStruct((ts, n), F32),
        ],
        scratch_shapes=[pltpu.VMEM((_CARRY + bm, bn), F32)] + _weight_scratch(w_in, kdim, bn),
        compiler_params=_params(2),
        name="xbc_branch",
    )(h, w_in, conv_w, conv_b.reshape(1, n), st, st, st)


def _expand_heads(vals, rep2_ref):
    hi, lo = _split_hi_lo(vals)
    return _dot(jnp.concatenate([hi, lo], axis=1), rep2_ref[...])


def _ssd_prompt_kernel(xs_ref, b_ref, c_ref, dt_ref, z_ref, alog_ref, dexp_ref, nw_ref, rep2_ref,
                       ym_ref, st_out_ref,
                       st_ref, dte_ref, eae_ref, dee_ref, acg_ref, act_ref, y_ref,
                       *, n_chunks, groups, hpg, hd, ns):
    q = SSD_CHUNK
    c = pl.program_id(1)
    gw = hpg * hd

    @pl.when(c == 0)
    def _():
        st_ref[...] = jnp.zeros(st_ref.shape, F32)

    dt = dt_ref[...]
    a = -jnp.exp(alog_ref[...])
    da = dt * a
    row = lax.broadcasted_iota(jnp.int32, (q, q), 0)
    col = lax.broadcasted_iota(jnp.int32, (q, q), 1)
    tril = (row >= col)
    hi, lo = _split_hi_lo(da)
    trilb = jnp.where(tril, 1.0, 0.0).astype(BF16)
    mid = (da - hi.astype(F32) - lo.astype(F32)).astype(BF16)
    acum = _dot(trilb, hi) + _dot(trilb, lo) + _dot(trilb, mid)
    a_last = acum[q - 1:q, :]
    dte_ref[...] = _expand_heads(dt, rep2_ref)
    eae_ref[...] = _expand_heads(jnp.exp(acum), rep2_ref)
    dee_ref[...] = _expand_heads(jnp.exp(a_last - acum), rep2_ref)
    act_ref[...] = acum.T
    acg_ref[0] = acum
    for g in range(1, groups):
        acg_ref[g] = pltpu.roll(acum, shift=acum.shape[1] - g * hpg, axis=1)

    lane = lax.broadcasted_iota(jnp.int32, (q, 2 * hd), 1)
    lo_half = lane < hd

    def group_body(g, carry):
        off = pl.multiple_of(g * gw, gw)
        noff = pl.multiple_of(g * ns, ns)
        xs_g = xs_ref[:, pl.ds(off, gw)].astype(F32)
        b_g = b_ref[:, pl.ds(noff, ns)]
        c_g = c_ref[:, pl.ds(noff, ns)]
        bt = b_g.astype(F32).T.astype(BF16)
        cb = _dot(c_g, bt)
        xdt = xs_g * dte_ref[:, pl.ds(off, gw)]
        xdt_b = xdt.astype(BF16)
        xd_b = (xdt * dee_ref[:, pl.ds(off, gw)]).astype(BF16)
        eae_g = eae_ref[:, pl.ds(off, gw)]
        st_g = st_ref[g]
        y_off = _dot(c_g, st_g.astype(BF16)) * eae_g
        st_ref[g] = st_g * eae_g[q - 1:q, :] + _dot(bt, xd_b)
        ac = acg_ref[g]
        hoff = pl.multiple_of(g * hpg, hpg)
        ac_t = act_ref[pl.ds(hoff, hpg), :]
        for pr in range(hpg // 2):
            ms = []
            for r in (2 * pr, 2 * pr + 1):
                seg = ac[:, r:r + 1] - ac_t[r:r + 1, :]
                ms.append((jnp.where(tril, jnp.exp(seg), 0.0) * cb).astype(BF16))
            lhs = jnp.concatenate(ms, axis=1)
            xp = xdt_b[:, pr * 2 * hd:(pr + 1) * 2 * hd]
            zero = jnp.zeros_like(xp)
            rhs = jnp.concatenate([jnp.where(lo_half, xp, zero), jnp.where(lo_half, zero, xp)], axis=0)
            sl = slice(pr * 2 * hd, (pr + 1) * 2 * hd)
            lanes_pr = pl.ds(pl.multiple_of(off + pr * 2 * hd, 2 * hd), 2 * hd)
            y_pair = _dot(lhs, rhs) + y_off[:, sl] + dexp_ref[:, lanes_pr] * xs_g[:, sl]
            yz = y_pair * z_ref[:, lanes_pr].astype(F32)
            y_ref[:, lanes_pr] = yz
            carry = carry + yz * yz
        return carry

    ssq = lax.fori_loop(0, groups, group_body, jnp.zeros((q, 2 * hd), F32))
    ms = jnp.sum(ssq, axis=-1, keepdims=True) * (1.0 / (groups * gw))
    ym_ref[...] = (y_ref[...] * lax.rsqrt(ms + EPS) * nw_ref[...]).astype(ym_ref.dtype)

    @pl.when(c == n_chunks - 1)
    def _():
        for g in range(groups):
            st_out_ref[0, g * gw:(g + 1) * gw, :] = st_ref[g].T


def _ssd_prompt(act, dt, zact, a_log, d_exp, norm_w, rep2, *, nb, seq, heads, hd, groups, ns):
    q = SSD_CHUNK
    d_inner = heads * hd
    hpg = heads // groups
    gw = hpg * hd
    n_chunks = seq // q
    gn = groups * ns
    assert heads == q and hd * 2 == q and ns == q and d_inner % gn == 0
    row = lambda b, c: b * n_chunks + c
    return pl.pallas_call(
        functools.partial(_ssd_prompt_kernel, n_chunks=n_chunks, groups=groups, hpg=hpg, hd=hd, ns=ns),
        grid=(nb, n_chunks),
        in_specs=[
            pl.BlockSpec((q, d_inner), lambda b, c: (row(b, c), 0)),
            pl.BlockSpec((q, gn), lambda b, c: (row(b, c), d_inner // gn)),
            pl.BlockSpec((q, gn), lambda b, c: (row(b, c), d_inner // gn + 1)),
            pl.BlockSpec((q, heads), lambda b, c: (row(b, c), 0)),
            pl.BlockSpec((q, d_inner), lambda b, c: (row(b, c), 0)),
            pl.BlockSpec((1, heads), lambda b, c: (0, 0)),
            pl.BlockSpec((1, d_inner), lambda b, c: (0, 0)),
            pl.BlockSpec((1, d_inner), lambda b, c: (0, 0)),
            pl.BlockSpec((2 * heads, d_inner), lambda b, c: (0, 0)),
        ],
        out_specs=[
            pl.BlockSpec((q, d_inner), lambda b, c: (row(b, c), 0)),
            pl.BlockSpec((1, d_inner, ns), lambda b, c: (b, 0, 0)),
        ],
        out_shape=[
            jax.ShapeDtypeStruct((nb * seq, d_inner), BF16),
            jax.ShapeDtypeStruct((nb, d_inner, ns), F32),
        ],
        scratch_shapes=[
            pltpu.VMEM((groups, ns, gw), F32),
            pltpu.VMEM((q, d_inner), F32),
            pltpu.VMEM((q, d_inner), F32),
            pltpu.VMEM((q, d_inner), F32),
            pltpu.VMEM((groups, q, heads), F32),
            pltpu.VMEM((heads, q), F32),
            pltpu.VMEM((q, d_inner), F32),
        ],
        compiler_params=_params(2),
        name="ssd_prompt",
    )(act, act, act, dt, zact, a_log.reshape(1, heads), d_exp, norm_w.reshape(1, d_inner), rep2)


_SAMPLE_SEQS_PER_STEP = 2


def _ssd_sample_kernel(st_ref, xs_ref, b_ref, c_ref, dt_ref, z_ref, alog_ref, dexp_ref, nw_ref, rep2_ref,
                       st_out_ref, ym_ref, *, groups, hpg, hd, ns):
    for s in range(st_ref.shape[0]):
        _ssd_sample_one(st_ref.at[s], xs_ref.at[s], b_ref.at[s], c_ref.at[s], dt_ref.at[s], z_ref.at[s],
                        alog_ref, dexp_ref, nw_ref, rep2_ref, st_out_ref.at[s], ym_ref.at[s],
                        groups=groups, hpg=hpg, hd=hd, ns=ns)


def _ssd_sample_one(st_ref, xs_ref, b_ref, c_ref, dt_ref, z_ref, alog_ref, dexp_ref, nw_ref, rep2_ref,
                    st_out_ref, ym_ref, *, groups, hpg, hd, ns):
    gw = hpg * hd
    d_inner = groups * gw
    xs = xs_ref[...].astype(F32)
    dt = dt_ref[...]
    a = -jnp.exp(alog_ref[...])
    pad = jnp.zeros((6, dt.shape[1]), F32)
    both = _expand_heads(jnp.concatenate([dt, jnp.exp(dt * a), pad], axis=0), rep2_ref)
    dte = both[0:1, :]
    dae = both[1:2, :]
    xdt = xs * dte
    kr = 2 * groups
    grp_of_lane = lax.broadcasted_iota(jnp.int32, (kr, d_inner), 1) // gw
    krow = lax.broadcasted_iota(jnp.int32, (kr, d_inner), 0)
    ltf = (jnp.where(grp_of_lane == krow, jnp.broadcast_to(xdt, (kr, d_inner)), 0.0)
           + jnp.where(krow == groups, jnp.broadcast_to(dae, (kr, d_inner)), 0.0))
    lt_hi, lt_lo = _split_hi_lo(ltf)
    lt = jnp.concatenate([lt_hi, lt_lo], axis=0)
    bmat = b_ref[...].astype(F32)
    rrow = lax.broadcasted_iota(jnp.int32, (groups, 2 * ns), 0)
    rlane = lax.broadcasted_iota(jnp.int32, (groups, 2 * ns), 1)
    r_top = jnp.concatenate([bmat, jnp.zeros_like(bmat)], axis=1)
    r_bot = jnp.where((rrow == 0) & (rlane >= ns), 1.0, 0.0)
    rtf = jnp.concatenate([r_top, r_bot], axis=0)
    rt = jnp.concatenate([rtf, rtf], axis=0).astype(BF16)
    both2 = lax.dot_general(lt, rt, (((0,), (0,)), ((), ())), preferred_element_type=F32)
    new = st_ref[...] * both2[:, ns:] + both2[:, :ns]
    st_out_ref[...] = new
    yg = _dot_nt(c_ref[...], new.astype(BF16))
    own = (lax.broadcasted_iota(jnp.int32, (groups, d_inner), 1) // gw
           == lax.broadcasted_iota(jnp.int32, (groups, d_inner), 0))
    y = jnp.sum(jnp.where(own, yg, 0.0), axis=0, keepdims=True)
    y = y + dexp_ref[...] * xs
    yz = y * z_ref[...].astype(F32)
    ms = jnp.mean(yz * yz, axis=-1, keepdims=True)
    ym_ref[...] = (yz * lax.rsqrt(ms + EPS) * nw_ref[...]).astype(ym_ref.dtype)


def _ssd_sample(state, xs, bmat, cmat, dt, zact, a_log, d_exp, norm_w, rep2, *, heads, hd, groups, ns):
    ts = state.shape[0]
    d_inner = heads * hd
    hpg = heads // groups
    spb = _SAMPLE_SEQS_PER_STEP if ts % _SAMPLE_SEQS_PER_STEP == 0 else 1
    seq_spec = lambda shape: pl.BlockSpec((spb,) + shape, lambda b: (b, 0, 0))
    const = lambda shape: pl.BlockSpec(shape, lambda b: (0, 0))
    return pl.pallas_call(
        functools.partial(_ssd_sample_kernel, groups=groups, hpg=hpg, hd=hd, ns=ns),
        grid=(ts // spb,),
        in_specs=[
            seq_spec((d_inner, ns)), seq_spec((1, d_inner)), seq_spec((groups, ns)), seq_spec((groups, ns)),
            seq_spec((1, heads)), seq_spec((1, d_inner)),
            const((1, heads)), const((1, d_inner)), const((1, d_inner)), const((2 * heads, d_inner)),
        ],
        out_specs=[seq_spec((d_inner, ns)), seq_spec((1, d_inner))],
        out_shape=[
            jax.ShapeDtypeStruct((ts, d_inner, ns), F32),
            jax.ShapeDtypeStruct((ts, 1, d_inner), F32),
        ],
        compiler_params=_params(1),
        name="ssd_sample",
    )(state, xs, bmat, cmat, dt, zact, a_log.reshape(1, heads), d_exp, norm_w.reshape(1, d_inner), rep2)


def _merge_kernel(*refs, n_p, bm, ts, has_xs, has_prev, has_wbf):
    x_ref = refs[0]
    k = 1
    xs_ref = x_ref
    if has_xs:
        xs_ref = refs[k]
        k += 1
    w_ref, ga_ref = refs[k], refs[k + 1]
    k += 2
    prev_ref = None
    if has_prev:
        prev_ref = refs[k]
        k += 1
    o_ref = refs[k]
    i = pl.program_id(1)
    w_ref = _resident_weight(w_ref, refs[k + 1] if has_wbf else None, i)

    def compute(rows, src_ref):
        for r0, r1 in _row_chunks(rows):
            acc = _dot(src_ref[r0:r1, :], w_ref[...]) * ga_ref[r0:r1, :].astype(F32)
            if has_prev:
                acc = acc + prev_ref[r0:r1, :].astype(F32)
            o_ref[r0:r1, :] = acc.astype(o_ref.dtype)

    @pl.when(i < n_p)
    def _():
        compute(bm, x_ref)

    @pl.when(i == n_p)
    def _():
        compute(ts, xs_ref)


def _merge(x, xs, w, gates, gate_col0, prev, *, tp, ts, bm, bn, out_dtype, name):
    kdim = x.shape[1]
    n = w.shape[1]
    n_p = tp // bm
    gcb = gate_col0 // bn
    assert gate_col0 % bn == 0 and n % bn == 0 and tp % bm == 0
    has_xs = xs is not None
    has_prev = prev is not None
    if has_xs:
        in_specs = [pl.BlockSpec((bm, kdim), lambda j, i: (jnp.minimum(i, n_p - 1), 0)),
                    pl.BlockSpec((ts, kdim), lambda j, i: (0, 0))]
        args = [x, xs]
    else:
        in_specs = [pl.BlockSpec((bm, kdim), lambda j, i: (i, 0))]
        args = [x]
    in_specs += [_weight_spec(kdim, bn, 0), pl.BlockSpec((bm, bn), lambda j, i: (i, gcb + j))]
    args += [w, gates]
    if has_prev:
        in_specs.append(pl.BlockSpec((bm, bn), lambda j, i: (i, j)))
        args.append(prev)
    return pl.pallas_call(
        functools.partial(_merge_kernel, n_p=n_p, bm=bm, ts=ts, has_xs=has_xs, has_prev=has_prev,
                          has_wbf=w.dtype != BF16),
        grid=(n // bn, n_p + 1),
        in_specs=in_specs,
        out_specs=pl.BlockSpec((bm, bn), lambda j, i: (i, j)),
        out_shape=jax.ShapeDtypeStruct((tp + ts, n), out_dtype),
        scratch_shapes=_weight_scratch(w, kdim, bn),
        compiler_params=_params(2),
        name=name,
    )(*args)


def _topk_rows(s, k):
    r = s.shape[0]
    rows = lax.broadcasted_iota(jnp.int32, s.shape, 0).astype(F32)
    vals, idxs = [], []
    for _ in range(k):
        m = jnp.max(s, axis=0, keepdims=True)
        idx = jnp.min(jnp.where(s == m, rows, float(r)), axis=0, keepdims=True)
        vals.append(m)
        idxs.append(idx)
        s = jnp.where(rows == idx, NEG_INF, s)
    return jnp.concatenate(vals, axis=0), jnp.concatenate(idxs, axis=0)


def _pick_rows(table, sel):
    k = table.shape[0]
    out = jnp.zeros_like(sel)
    for r in range(k):
        out = jnp.where(sel == float(r), jnp.broadcast_to(table[r:r + 1, :], sel.shape), out)
    return out


def _staircase_counts(k):
    return [k // (i + 1) for i in range(k)]


def _fold_keys_kernel(wq_ref, k1_ref, k2_ref, o_ref, *, qh, nkeys):
    hp = lax.Precision.HIGHEST
    nt = (((1,), (1,)), ((), ()))
    w = wq_ref[...]
    o_ref[:, 0:nkeys] = lax.dot_general(w[:, :qh], k1_ref[0], nt, precision=hp,
                                        preferred_element_type=F32).astype(o_ref.dtype)
    o_ref[:, nkeys:] = lax.dot_general(w[:, qh:], k2_ref[0], nt, precision=hp,
                                       preferred_element_type=F32).astype(o_ref.dtype)


def _fold_keys(wq, keys):
    d = wq.shape[0]
    _, heads, nkeys, qh = keys.shape
    return pl.pallas_call(
        functools.partial(_fold_keys_kernel, qh=qh, nkeys=nkeys),
        grid=(heads,),
        in_specs=[
            pl.BlockSpec((d, 2 * qh), lambda hh: (0, hh)),
            pl.BlockSpec((1, nkeys, qh), lambda hh: (hh, 0, 0)),
            pl.BlockSpec((1, nkeys, qh), lambda hh: (hh, 0, 0)),
        ],
        out_specs=pl.BlockSpec((d, 2 * nkeys), lambda hh: (0, hh)),
        out_shape=jax.ShapeDtypeStruct((d, heads * 2 * nkeys), BF16),
        compiler_params=_params(1),
        name="peer_fold_keys",
    )(wq, keys[0], keys[1])


_ROUTE_CHUNKS_PER_ITER = 3


def _route_kernel(h_ref, wk_ref, ia_ref, ib_ref, gt_ref, s1_ref, s2_ref, *cand_refs, nkeys, lanes):
    k = PEER_TOPK
    s = _dot(h_ref[...], wk_ref[...])
    s1_ref[...] = s[:, :nkeys].T
    s2_ref[...] = s[:, nkeys:].T
    counts = _staircase_counts(k)
    starts = [sum(counts[:i]) for i in range(k)]
    n_cand = sum(counts)

    def one_chunk(ci, cand_ref):
        off = ci * lanes if isinstance(ci, int) else pl.multiple_of(ci * lanes, lanes)
        v1, i1 = _topk_rows(s1_ref[:, pl.ds(off, lanes)], k)
        v2, i2 = _topk_rows(s2_ref[:, pl.ds(off, lanes)], k)
        for i in range(k):
            cand_ref[starts[i]:starts[i] + counts[i], :] = v1[i:i + 1, :] + v2[0:counts[i], :]
        cand_ref[n_cand:, :] = jnp.full((cand_ref.shape[0] - n_cand, lanes), NEG_INF, F32)
        sv, sp = _topk_rows(cand_ref[...], k)
        e = jnp.exp(sv - sv[0:1, :])
        gt_ref[0, :, pl.ds(off, lanes)] = e / jnp.sum(e, axis=0, keepdims=True)
        sel_i = jnp.zeros_like(sp)
        sel_start = jnp.zeros_like(sp)
        for i in range(1, k):
            ge = sp >= float(starts[i])
            sel_i = sel_i + jnp.where(ge, 1.0, 0.0)
            sel_start = sel_start + jnp.where(ge, float(counts[i - 1]), 0.0)
        ia_ref[0, :, pl.ds(off, lanes)] = _pick_rows(i1, sel_i)
        ib_ref[0, :, pl.ds(off, lanes)] = _pick_rows(i2, sp - sel_start)

    n_chunks = s1_ref.shape[1] // lanes
    per_iter = len(cand_refs)

    def chunk_group(gi, carry):
        for c, cand_ref in enumerate(cand_refs):
            one_chunk(per_iter * gi + c, cand_ref)
        return carry

    lax.fori_loop(0, n_chunks // per_iter, chunk_group, 0)
    for c in range(n_chunks % per_iter):
        one_chunk(n_chunks - n_chunks % per_iter + c, cand_refs[c])


def _route(h2, wk, *, heads, nkeys, tq):
    t, d = h2.shape
    k = PEER_TOPK
    out = jax.ShapeDtypeStruct((heads, k, t), F32)
    out_spec = pl.BlockSpec((1, k, tq), lambda i, hh: (hh, 0, i))
    return pl.pallas_call(
        functools.partial(_route_kernel, nkeys=nkeys, lanes=128),
        grid=(t // tq, heads),
        in_specs=[
            pl.BlockSpec((tq, d), lambda i, hh: (i, 0)),
            pl.BlockSpec((d, 2 * nkeys), lambda i, hh: (0, hh)),
        ],
        out_specs=[out_spec, out_spec, out_spec],
        out_shape=[out, out, out],
        scratch_shapes=[pltpu.VMEM((nkeys, tq), F32), pltpu.VMEM((nkeys, tq), F32)]
        + [pltpu.VMEM((-(-sum(_staircase_counts(k)) // 8) * 8, 128), F32)] * _ROUTE_CHUNKS_PER_ITER,
        compiler_params=_params(2),
        name="peer_route",
    )(h2, wk)


_SCATTER_UNROLL = 16
_SCATTER_PITCH = 136
_PACK_ROWS = 16


def _scatter_kernel(ia_ref, ib_ref, gt_ref, w_ref, ia_s, ib_s, gt_s, wsc, *, nkeys):
    tw = ia_ref.shape[1]
    ia_s[...] = ia_ref[...].T
    ib_s[...] = ib_ref[...].T
    gt_s[...] = gt_ref[...].T
    nsel = ia_ref.shape[0]
    key_id = lax.broadcasted_iota(jnp.int32, (nkeys, nsel), 0).astype(F32)

    def token_group(gi, carry):
        for k in range(_SCATTER_UNROLL):
            tk = gi * _SCATTER_UNROLL + k
            a_row = ia_s[pl.ds(tk, 1), :]
            b_row = ib_s[pl.ds(tk, 1), :]
            g_row = gt_s[pl.ds(tk, 1), :]
            at = jnp.where(key_id == a_row, jnp.broadcast_to(g_row, key_id.shape), 0.0).astype(BF16)
            bt = jnp.where(key_id == b_row, 1.0, 0.0).astype(BF16)
            wsc[pl.ds(pl.multiple_of(tk * _SCATTER_PITCH, 8), nkeys), :] = _dot_nt(at, bt)
        return carry

    lax.fori_loop(0, tw // _SCATTER_UNROLL, token_group, 0)

    def relayout(tg, carry):
        row0 = pl.multiple_of(tg * _PACK_ROWS, _PACK_ROWS)
        base = tg * (_PACK_ROWS * _SCATTER_PITCH)
        for a in range(nkeys):
            rows = wsc[pl.ds(base + a, _PACK_ROWS, stride=_SCATTER_PITCH), :]
            w_ref[pl.ds(row0, _PACK_ROWS), a * nkeys:(a + 1) * nkeys] = rows.astype(w_ref.dtype)
        return carry

    lax.fori_loop(0, tw // _PACK_ROWS, relayout, 0)


def _scatter(ia, ib, gt, *, nkeys, tw):
    nsel, t = ia.shape
    assert tw % _SCATTER_UNROLL == 0 and tw % _PACK_ROWS == 0
    spec = pl.BlockSpec((nsel, tw), lambda i: (0, i))
    return pl.pallas_call(
        functools.partial(_scatter_kernel, nkeys=nkeys),
        grid=(t // tw,),
        in_specs=[spec, spec, spec],
        out_specs=pl.BlockSpec((tw, nkeys * nkeys), lambda i: (i, 0)),
        out_shape=jax.ShapeDtypeStruct((t, nkeys * nkeys), BF16),
        scratch_shapes=[pltpu.VMEM((tw, nsel), F32)] * 3 + [pltpu.VMEM((tw * _SCATTER_PITCH, nkeys), F32)],
        compiler_params=_params(1),
        name="peer_scatter",
    )(ia, ib, gt)


def _experts_kernel(h_ref, u_ref, v_ref, w_ref, o_ref):
    e = pl.program_id(1)

    @pl.when(e == 0)
    def _():
        o_ref[...] = jnp.zeros(o_ref.shape, F32)

    tm = h_ref.shape[0]
    rc = EXPERT_ROW_CHUNK if tm % EXPERT_ROW_CHUNK == 0 else tm
    for r0 in range(0, tm, rc):
        s = _dot_nt(h_ref[r0:r0 + rc, :], u_ref[...])
        gelu = 0.5 * s * (1.0 + lax.erf(s * (2.0 ** -0.5)))
        act = gelu * w_ref[r0:r0 + rc, :].astype(F32)
        o_ref[r0:r0 + rc, :] += _dot(act.astype(BF16), v_ref[...])


def _experts(h2, u, v, w, *, tm, te):
    t, d = h2.shape
    ne = u.shape[0]
    return pl.pallas_call(
        _experts_kernel,
        grid=(t // tm, ne // te),
        in_specs=[
            pl.BlockSpec((tm, d), lambda i, e: (i, 0), pipeline_mode=pl.Buffered(1)),
            pl.BlockSpec((te, d), lambda i, e: (e, 0)),
            pl.BlockSpec((te, d), lambda i, e: (e, 0)),
            pl.BlockSpec((tm, te), lambda i, e: (i, e)),
        ],
        out_specs=pl.BlockSpec((tm, d), lambda i, e: (i, 0), pipeline_mode=pl.Buffered(1)),
        out_shape=jax.ShapeDtypeStruct((t, d), F32),
        compiler_params=_params(2),
        name="peer_experts",
    )(h2, u, v, w)


def _largest_divisor(n, cap, multiple):
    best = None
    for cand in range(multiple, min(n, cap) + 1, multiple):
        if n % cand == 0:
            best = cand
    assert best is not None, (n, cap, multiple)
    return best


def kernel(x_prompt, x_sample, state_shortconv, state_mamba_conv, state_ssm, ln1_w, w_in, sc_conv_w,
           sc_out_w, m_conv_w, m_conv_b, m_dt_bias, m_A_log, m_D, m_norm_w, m_out_w, w_o, ln2_w,
           peer_wq, peer_keys, peer_u, peer_v, final_norm_w):
    nb, seq, d = x_prompt.shape
    ts = x_sample.shape[0]
    assert x_sample.shape[1] == 1 and ln1_w.shape[0] == 1, "single layer, one new token per sample"
    tp = nb * seq
    t = tp + ts
    d_sc = state_shortconv.shape[-1]
    conv_dim = state_mamba_conv.shape[-1]
    _, _, heads, hd, ns = state_ssm.shape
    d_inner = heads * hd
    groups = (conv_dim - d_inner) // (2 * ns)
    gn = groups * ns
    nkeys = peer_keys.shape[3]
    c_z = 3 * d_sc
    c_xbc = c_z + d_inner
    c_dt = c_xbc + conv_dim
    c_gate = c_dt + heads

    bm = _largest_divisor(seq, 1024, 128)
    bm_small = _largest_divisor(seq, 512, 128)
    bt = _largest_divisor(seq, 256, 8)

    xp2 = x_prompt.reshape(tp, d)
    xs2 = x_sample.reshape(ts, d)
    w_in0 = w_in[0]

    h = _norm_in(xp2, xs2, ln1_w[0], bt=bt)

    g, sc_tail_p, sc_u_s = _sc_branch(h, w_in0, sc_conv_w[0], state_shortconv[0], d_sc=d_sc, tp=tp, ts=ts,
                                      seq=seq, bm=bm, bn=_largest_divisor(d_sc, 256, 128))
    act, mc_tail_p, mc_raw_s = _xbc_branch(h, w_in0, m_conv_w[0], m_conv_b[0], state_mamba_conv[0],
                                           col0=c_xbc, n=conv_dim, tp=tp, ts=ts, seq=seq, bm=bm,
                                           bn=_largest_divisor(conv_dim, 512, 128))
    zact = _mm_act(h, w_in0, col0=c_z, n=d_inner, tp=tp, ts=ts, bm=bm_small,
                   bn=_largest_divisor(d_inner, 1024, 128), act="silu", out_dtype=BF16, name="proj_z")
    dt = _mm_act(h, w_in0, col0=c_dt, n=heads, tp=tp, ts=ts, bm=bm, bn=heads, act="softplus",
                 out_dtype=F32, bias=m_dt_bias[0], name="proj_dt")
    gates = _mm_act(h, w_in0, col0=c_gate, n=2 * d, tp=tp, ts=ts, bm=bm_small,
                    bn=_largest_divisor(d, 1024, 128), act="sigmoid", out_dtype=BF16, name="proj_gates")

    d_exp = jnp.repeat(m_D[0], hd).reshape(1, d_inner)
    rep = (jnp.arange(d_inner)[None, :] // hd == jnp.arange(heads)[:, None]).astype(BF16)
    rep2 = jnp.concatenate([rep, rep], axis=0)
    ym_p, ssm_p = _ssd_prompt(act, dt, zact, m_A_log[0], d_exp, m_norm_w[0], rep2, nb=nb, seq=seq,
                              heads=heads, hd=hd, groups=groups, ns=ns)
    act_s = act[tp:]
    ssm_s, ym_s = _ssd_sample(
        state_ssm[0].reshape(ts, d_inner, ns),
        act_s[:, :d_inner].astype(F32).reshape(ts, 1, d_inner),
        act_s[:, d_inner:d_inner + gn].reshape(ts, groups, ns),
        act_s[:, d_inner + gn:].reshape(ts, groups, ns),
        dt[tp:].reshape(ts, 1, heads),
        zact[tp:].astype(F32).reshape(ts, 1, d_inner),
        m_A_log[0], d_exp, m_norm_w[0], rep2, heads=heads, hd=hd, groups=groups, ns=ns)
    bn_o = _largest_divisor(d, 512, 128)
    mix_a = _merge(g, None, sc_out_w[0], gates, 0, None, tp=tp, ts=ts, bm=bm, bn=bn_o,
                   out_dtype=F32, name="merge_a")
    mixed = _merge(ym_p, ym_s.reshape(ts, d_inner).astype(BF16), m_out_w[0].astype(BF16), gates, d, mix_a,
                   tp=tp, ts=ts, bm=bm_small, bn=bn_o, out_dtype=BF16, name="merge_b")
    x1 = _mm_act(mixed, w_o[0], col0=0, n=d, tp=tp, ts=ts, bm=bm, bn=bn_o, act="none",
                 out_dtype=F32, resid=(xp2, xs2), name="proj_o")

    tok_tile = _largest_divisor(t, 640, 128)
    h2 = _norm_mid(x1, ln2_w[0], bt=_largest_divisor(t, 256, 8))
    wk = _fold_keys(peer_wq[0], peer_keys[0])
    ia, ib, gt = _route(h2, wk, heads=peer_keys.shape[2], nkeys=nkeys, tq=_largest_divisor(t, 1664, 128))
    nsel = ia.shape[0] * ia.shape[1]
    w = _scatter(ia.reshape(nsel, t), ib.reshape(nsel, t), gt.reshape(nsel, t), nkeys=nkeys, tw=128)
    peer = _experts(h2, peer_u[0].astype(BF16), peer_v[0].astype(BF16), w,
                    tm=tok_tile, te=_largest_divisor(nkeys * nkeys, 512, 128))
    y_p, y_s = _norm_out(x1, peer, final_norm_w, tp=tp, ts=ts, bt=bt)

    new_sc_s = jnp.stack([state_shortconv[0][:, 1], sc_u_s], axis=1)
    new_mc_s = jnp.stack([state_mamba_conv[0][:, 1], state_mamba_conv[0][:, 2], mc_raw_s], axis=1)
    return (
        y_p.reshape(nb, seq, d),
        y_s.reshape(ts, 1, d),
        sc_tail_p[None],
        mc_tail_p[None],
        ssm_p.reshape(1, nb, heads, hd, ns),
        new_sc_s[None],
        new_mc_s[None],
        ssm_s.reshape(1, ts, heads, hd, ns),
    )
```

```python
import functools

import jax
import jax.numpy as jnp
from jax import lax
from jax.experimental import pallas as pl
from jax.experimental.pallas import tpu as pltpu

F32 = jnp.float32
BF16 = jnp.bfloat16
EPS = 1e-6
PEER_TOPK = 16
SSD_CHUNK = 128
VMEM_LIMIT_BYTES = 56 * 1024 * 1024
NEG_INF = float("-inf")
LANES = 128


def _params(n_grid_axes):
    return pltpu.CompilerParams(
        dimension_semantics=("arbitrary",) * n_grid_axes,
        vmem_limit_bytes=VMEM_LIMIT_BYTES,
    )


def _dot(a, b):
    return jnp.dot(a, b, preferred_element_type=F32)


def _dot_nt(a, b):
    return lax.dot_general(a, b, (((1,), (1,)), ((), ())), preferred_element_type=F32)


def _silu(x):
    return x * jax.nn.sigmoid(x)


ROW_CHUNK = 256
EXPERT_ROW_CHUNK = 640


def _row_chunks(rows):
    rc = ROW_CHUNK if rows % ROW_CHUNK == 0 else rows
    return [(r0, r0 + rc) for r0 in range(0, rows, rc)]


def _split_hi_lo(x):
    hi = x.astype(BF16)
    lo = (x - hi.astype(F32)).astype(BF16)
    return hi, lo


def _rms(x, w):
    ms = jnp.mean(x * x, axis=-1, keepdims=True)
    return x * lax.rsqrt(ms + EPS) * w


def _norm_in_kernel(xp_ref, xs_ref, w_ref, o_ref, *, n_p, ts):
    i = pl.program_id(0)

    @pl.when(i < n_p)
    def _():
        o_ref[...] = _rms(xp_ref[...], w_ref[...]).astype(o_ref.dtype)

    @pl.when(i == n_p)
    def _():
        o_ref[0:ts, :] = _rms(xs_ref[...], w_ref[...]).astype(o_ref.dtype)


def _norm_in(xp, xs, w, *, bt):
    tp, d = xp.shape
    ts = xs.shape[0]
    n_p = tp // bt
    return pl.pallas_call(
        functools.partial(_norm_in_kernel, n_p=n_p, ts=ts),
        grid=(n_p + 1,),
        in_specs=[
            pl.BlockSpec((bt, d), lambda i: (jnp.minimum(i, n_p - 1), 0)),
            pl.BlockSpec((ts, d), lambda i: (0, 0)),
            pl.BlockSpec((1, d), lambda i: (0, 0)),
        ],
        out_specs=pl.BlockSpec((bt, d), lambda i: (i, 0)),
        out_shape=jax.ShapeDtypeStruct((tp + ts, d), BF16),
        compiler_params=_params(1),
        name="norm_in",
    )(xp, xs, w.reshape(1, d))


def _norm_mid_kernel(x_ref, w_ref, o_ref):
    o_ref[...] = _rms(x_ref[...], w_ref[...]).astype(o_ref.dtype)


def _norm_mid(x, w, *, bt):
    t, d = x.shape
    return pl.pallas_call(
        _norm_mid_kernel,
        grid=(t // bt,),
        in_specs=[pl.BlockSpec((bt, d), lambda i: (i, 0)), pl.BlockSpec((1, d), lambda i: (0, 0))],
        out_specs=pl.BlockSpec((bt, d), lambda i: (i, 0)),
        out_shape=jax.ShapeDtypeStruct((t, d), BF16),
        compiler_params=_params(1),
        name="norm_mid",
    )(x, w.reshape(1, d))


def _norm_out_kernel(x_ref, y_ref, w_ref, op_ref, os_ref, *, n_p, ts):
    i = pl.program_id(0)

    @pl.when(i < n_p)
    def _():
        op_ref[...] = _rms(x_ref[...] + y_ref[...], w_ref[...])

    @pl.when(i == n_p)
    def _():
        os_ref[...] = _rms(x_ref[0:ts, :] + y_ref[0:ts, :], w_ref[...])


def _norm_out(x, y, w, *, tp, ts, bt):
    d = x.shape[1]
    n_p = tp // bt
    return pl.pallas_call(
        functools.partial(_norm_out_kernel, n_p=n_p, ts=ts),
        grid=(n_p + 1,),
        in_specs=[
            pl.BlockSpec((bt, d), lambda i: (i, 0)),
            pl.BlockSpec((bt, d), lambda i: (i, 0)),
            pl.BlockSpec((1, d), lambda i: (0, 0)),
        ],
        out_specs=[
            pl.BlockSpec((bt, d), lambda i: (jnp.minimum(i, n_p - 1), 0)),
            pl.BlockSpec((ts, d), lambda i: (0, 0)),
        ],
        out_shape=[jax.ShapeDtypeStruct((tp, d), F32), jax.ShapeDtypeStruct((ts, d), F32)],
        compiler_params=_params(1),
        name="norm_out",
    )(x, y, w.reshape(1, d))


def _apply_act(acc, act):
    if act == "silu":
        return _silu(acc)
    if act == "sigmoid":
        return jax.nn.sigmoid(acc)
    if act == "softplus":
        return jax.nn.softplus(acc)
    assert act == "none"
    return acc


def _weight_spec(kdim, bn, col0):
    if col0 % bn == 0:
        return pl.BlockSpec((kdim, bn), lambda j, i: (0, col0 // bn + j))
    assert col0 % LANES == 0 and bn % LANES == 0
    return pl.BlockSpec((pl.Element(kdim), pl.Element(bn)),
                        lambda j, i: (0, (col0 // LANES + j * (bn // LANES)) * LANES))


def _resident_weight(w_ref, wbf_ref, i):
    if wbf_ref is None:
        return w_ref

    @pl.when(i == 0)
    def _():
        wbf_ref[...] = w_ref[...].astype(BF16)

    return wbf_ref


def _weight_scratch(w, kdim, bn):
    return [] if w.dtype == BF16 else [pltpu.VMEM((kdim, bn), BF16)]


def _mm_act_kernel(*refs, n_p, bm, ts, act, has_bias, has_resid, has_wbf):
    x_ref, w_ref = refs[0], refs[1]
    k = 2
    b_ref = None
    if has_bias:
        b_ref = refs[k]
        k += 1
    rp_ref = rs_ref = None
    if has_resid:
        rp_ref, rs_ref = refs[k], refs[k + 1]
        k += 2
    o_ref = refs[k]
    i = pl.program_id(1)
    w_ref = _resident_weight(w_ref, refs[k + 1] if has_wbf else None, i)

    def compute(rows, r_ref):
        for r0, r1 in _row_chunks(rows):
            acc = _dot(x_ref[r0:r1, :], w_ref[...])
            if has_bias:
                acc = acc + b_ref[...]
            acc = _apply_act(acc, act)
            if has_resid:
                acc = acc + r_ref[r0:r1, :]
            o_ref[r0:r1, :] = acc.astype(o_ref.dtype)

    @pl.when(i < n_p)
    def _():
        compute(bm, rp_ref)

    @pl.when(i == n_p)
    def _():
        compute(ts, rs_ref)


def _mm_act(x, w, *, col0, n, tp, ts, bm, bn, act, out_dtype, bias=None, resid=None, name):
    t, kdim = x.shape
    n_p = tp // bm
    assert tp % bm == 0 and ts <= bm and n % bn == 0
    in_specs = [
        pl.BlockSpec((bm, kdim), lambda j, i: (i, 0)),
        _weight_spec(kdim, bn, col0),
    ]
    args = [x, w]
    if bias is not None:
        in_specs.append(pl.BlockSpec((1, bn), lambda j, i: (0, j)))
        args.append(bias.reshape(1, n))
    if resid is not None:
        in_specs.append(pl.BlockSpec((bm, bn), lambda j, i: (jnp.minimum(i, n_p - 1), j)))
        in_specs.append(pl.BlockSpec((ts, bn), lambda j, i: (0, j)))
        args.extend(resid)
    return pl.pallas_call(
        functools.partial(_mm_act_kernel, n_p=n_p, bm=bm, ts=ts, act=act,
                          has_bias=bias is not None, has_resid=resid is not None,
                          has_wbf=w.dtype != BF16),
        grid=(n // bn, n_p + 1),
        in_specs=in_specs,
        out_specs=pl.BlockSpec((bm, bn), lambda j, i: (i, j)),
        out_shape=jax.ShapeDtypeStruct((t, n), out_dtype),
        scratch_shapes=_weight_scratch(w, kdim, bn),
        compiler_params=_params(2),
        name=name,
    )(*args)


_CARRY = 8


def _conv_reset(ubuf, first_in_seq):
    @pl.when(first_in_seq)
    def _():
        ubuf[0:_CARRY, :] = jnp.zeros((_CARRY, ubuf.shape[1]), F32)


def _conv_rows(ubuf, cw_ref, u, r0, *, taps):
    rc = u.shape[0]
    ubuf[_CARRY + r0:_CARRY + r0 + rc, :] = u
    return _conv_taps(ubuf, cw_ref, u, r0, taps=taps)


def _conv_taps(ubuf, cw_ref, u, r0, *, taps):
    rc = u.shape[0]
    ext = ubuf[r0:r0 + _CARRY + rc, :]
    y = None
    for k in range(taps - 1):
        shifted = pltpu.roll(ext, shift=taps - 1 - k, axis=0)[_CARRY:_CARRY + rc, :]
        term = shifted * cw_ref[k:k + 1, :]
        y = term if y is None else y + term
    y = y + u * cw_ref[taps - 1:taps, :]
    return y


def _sc_kernel(x_ref, wb_ref, wc_ref, wx_ref, cw_ref, s0_ref, s1_ref,
               g_ref, tail_ref, us_ref, ubuf, wcat, *, n_p, bm, ts, tps):
    i = pl.program_id(1)
    bn = wb_ref.shape[1]

    @pl.when(i == 0)
    def _():
        wcat[:, 0:bn] = wb_ref[...].astype(BF16)
        wcat[:, bn:2 * bn] = wc_ref[...].astype(BF16)
        wcat[:, 2 * bn:3 * bn] = wx_ref[...].astype(BF16)

    @pl.when(i < n_p)
    def _():
        _conv_reset(ubuf, (i % tps) == 0)
        for r0, r1 in _row_chunks(bm):
            p = _dot(x_ref[r0:r1, :], wcat[...])
            u = p[:, bn:2 * bn] * p[:, 2 * bn:3 * bn]
            y = _conv_rows(ubuf, cw_ref, u, r0, taps=3)
            g_ref[r0:r1, :] = (p[:, 0:bn] * y).astype(g_ref.dtype)
        tail_ref[0] = ubuf[_CARRY + bm - 2:_CARRY + bm, :]
        ubuf[0:_CARRY, :] = ubuf[bm:bm + _CARRY, :]

    @pl.when(i == n_p)
    def _():
        p = _dot(x_ref[0:ts, :], wcat[...])
        u = p[:, bn:2 * bn] * p[:, 2 * bn:3 * bn]
        us_ref[...] = u
        y = s0_ref[...] * cw_ref[0:1, :]
        y = y + s1_ref[...] * cw_ref[1:2, :]
        y = y + u * cw_ref[2:3, :]
        g_ref[0:ts, :] = (p[:, 0:bn] * y).astype(g_ref.dtype)


def _sc_branch(h, w_in, conv_w, state, *, d_sc, tp, ts, seq, bm, bn):
    t, kdim = h.shape
    n_p = tp // bm
    tps = seq // bm
    nb = tp // seq
    nj = d_sc // bn
    st = state.reshape(ts, 2 * d_sc)
    w_spec = lambda off: pl.BlockSpec((kdim, bn), lambda j, i: (0, off + j))
    return pl.pallas_call(
        functools.partial(_sc_kernel, n_p=n_p, bm=bm, ts=ts, tps=tps),
        grid=(nj, n_p + 1),
        in_specs=[
            pl.BlockSpec((bm, kdim), lambda j, i: (i, 0)),
            w_spec(0), w_spec(nj), w_spec(2 * nj),
            pl.BlockSpec((3, bn), lambda j, i: (0, j)),
            pl.BlockSpec((ts, bn), lambda j, i: (0, j)),
            pl.BlockSpec((ts, bn), lambda j, i: (0, nj + j)),
        ],
        out_specs=[
            pl.BlockSpec((bm, bn), lambda j, i: (i, j)),
            pl.BlockSpec((1, 2, bn), lambda j, i: (jnp.minimum(i, n_p - 1) // tps, 0, j)),
            pl.BlockSpec((ts, bn), lambda j, i: (0, j)),
        ],
        out_shape=[
            jax.ShapeDtypeStruct((t, d_sc), BF16),
            jax.ShapeDtypeStruct((nb, 2, d_sc), F32),
            jax.ShapeDtypeStruct((ts, d_sc), F32),
        ],
        scratch_shapes=[pltpu.VMEM((_CARRY + bm, bn), F32), pltpu.VMEM((kdim, 3 * bn), BF16)],
        compiler_params=_params(2),
        name="sc_branch",
    )(h, w_in, w_in, w_in, conv_w, st, st)


def _xbc_kernel(x_ref, w_ref, cw_ref, cb_ref, s0_ref, s1_ref, s2_ref,
                a_ref, tail_ref, rs_ref, ubuf, *wbf, n_p, bm, ts, tps):
    i = pl.program_id(1)
    w_ref = _resident_weight(w_ref, wbf[0] if wbf else None, i)

    @pl.when(i < n_p)
    def _():
        _conv_reset(ubuf, (i % tps) == 0)

        def epilogue(r0, r1):
            raw = ubuf[_CARRY + r0:_CARRY + r1, :]
            y = _conv_taps(ubuf, cw_ref, raw, r0, taps=4)
            a_ref[r0:r1, :] = _silu(y + cb_ref[...]).astype(a_ref.dtype)

        chunks = _row_chunks(bm)
        for ci, (r0, r1) in enumerate(chunks):
            ubuf[_CARRY + r0:_CARRY + r1, :] = _dot(x_ref[r0:r1, :], w_ref[...])
            if ci > 0:
                epilogue(*chunks[ci - 1])
        epilogue(*chunks[-1])
        tail_ref[0] = ubuf[_CARRY + bm - 3:_CARRY + bm, :]
        ubuf[0:_CARRY, :] = ubuf[bm:bm + _CARRY, :]

    @pl.when(i == n_p)
    def _():
        r = _dot(x_ref[0:ts, :], w_ref[...])
        rs_ref[...] = r
        y = s0_ref[...] * cw_ref[0:1, :]
        y = y + s1_ref[...] * cw_ref[1:2, :]
        y = y + s2_ref[...] * cw_ref[2:3, :]
        y = y + r * cw_ref[3:4, :]
        a_ref[0:ts, :] = _silu(y + cb_ref[...]).astype(a_ref.dtype)


def _xbc_branch(h, w_in, conv_w, conv_b, state, *, col0, n, tp, ts, seq, bm, bn):
    t, kdim = h.shape
    n_p = tp // bm
    tps = seq // bm
    nb = tp // seq
    nj = n // bn
    cb = col0 // bn
    assert col0 % bn == 0
    st = state.reshape(ts, 3 * n)
    return pl.pallas_call(
        functools.partial(_xbc_kernel, n_p=n_p, bm=bm, ts=ts, tps=tps),
        grid=(nj, n_p + 1),
        in_specs=[
            pl.BlockSpec((bm, kdim), lambda j, i: (i, 0)),
            pl.BlockSpec((kdim, bn), lambda j, i: (0, cb + j)),
            pl.BlockSpec((4, bn), lambda j, i: (0, j)),
            pl.BlockSpec((1, bn), lambda j, i: (0, j)),
            pl.BlockSpec((ts, bn), lambda j, i: (0, j)),
            pl.BlockSpec((ts, bn), lambda j, i: (0, nj + j)),
            pl.BlockSpec((ts, bn), lambda j, i: (0, 2 * nj + j)),
        ],
        out_specs=[
            pl.BlockSpec((bm, bn), lambda j, i: (i, j)),
            pl.BlockSpec((1, 3, bn), lambda j, i: (jnp.minimum(i, n_p - 1) // tps, 0, j)),
            pl.BlockSpec((ts, bn), lambda j, i: (0, j)),
        ],
        out_shape=[
            jax.ShapeDtypeStruct((t, n), BF16),
            jax.ShapeDtypeStruct((nb, 3, n), F32),
            jax.ShapeDtypeStruct((ts, n), F32),
        ],
        scratch_shapes=[pltpu.VMEM((_CARRY + bm, bn), F32)] + _weight_scratch(w_in, kdim, bn),
        compiler_params=_params(2),
        name="xbc_branch",
    )(h, w_in, conv_w, conv_b.reshape(1, n), st, st, st)


def _expand_heads(vals, rep2_ref):
    hi, lo = _split_hi_lo(vals)
    return _dot(jnp.concatenate([hi, lo], axis=1), rep2_ref[...])


def _ssd_prompt_kernel(xs_ref, b_ref, c_ref, dt_ref, z_ref, alog_ref, dexp_ref, nw_ref, rep2_ref,
                       ym_ref, st_out_ref,
                       st_ref, dte_ref, eae_ref, dee_ref, acg_ref, act_ref, y_ref,
                       *, n_chunks, groups, hpg, hd, ns):
    q = SSD_CHUNK
    c = pl.program_id(1)
    gw = hpg * hd

    @pl.when(c == 0)
    def _():
        st_ref[...] = jnp.zeros(st_ref.shape, F32)

    dt = dt_ref[...]
    a = -jnp.exp(alog_ref[...])
    da = dt * a
    row = lax.broadcasted_iota(jnp.int32, (q, q), 0)
    col = lax.broadcasted_iota(jnp.int32, (q, q), 1)
    tril = (row >= col)
    hi, lo = _split_hi_lo(da)
    trilb = jnp.where(tril, 1.0, 0.0).astype(BF16)
    mid = (da - hi.astype(F32) - lo.astype(F32)).astype(BF16)
    acum = _dot(trilb, hi) + _dot(trilb, lo) + _dot(trilb, mid)
    a_last = acum[q - 1:q, :]
    dte_ref[...] = _expand_heads(dt, rep2_ref)
    eae_ref[...] = _expand_heads(jnp.exp(acum), rep2_ref)
    dee_ref[...] = _expand_heads(jnp.exp(a_last - acum), rep2_ref)
    act_ref[...] = acum.T
    acg_ref[0] = acum
    for g in range(1, groups):
        acg_ref[g] = pltpu.roll(acum, shift=acum.shape[1] - g * hpg, axis=1)

    lane = lax.broadcasted_iota(jnp.int32, (q, 2 * hd), 1)
    lo_half = lane < hd

    def group_body(g, carry):
        off = pl.multiple_of(g * gw, gw)
        noff = pl.multiple_of(g * ns, ns)
        xs_g = xs_ref[:, pl.ds(off, gw)].astype(F32)
        b_g = b_ref[:, pl.ds(noff, ns)]
        c_g = c_ref[:, pl.ds(noff, ns)]
        bt = b_g.astype(F32).T.astype(BF16)
        cb = _dot(c_g, bt)
        xdt = xs_g * dte_ref[:, pl.ds(off, gw)]
        xdt_b = xdt.astype(BF16)
        xd_b = (xdt * dee_ref[:, pl.ds(off, gw)]).astype(BF16)
        eae_g = eae_ref[:, pl.ds(off, gw)]
        st_g = st_ref[g]
        y_off = _dot(c_g, st_g.astype(BF16)) * eae_g
        st_ref[g] = st_g * eae_g[q - 1:q, :] + _dot(bt, xd_b)
        ac = acg_ref[g]
        hoff = pl.multiple_of(g * hpg, hpg)
        ac_t = act_ref[pl.ds(hoff, hpg), :]
        for pr in range(hpg // 2):
            ms = []
            for r in (2 * pr, 2 * pr + 1):
                seg = ac[:, r:r + 1] - ac_t[r:r + 1, :]
                ms.append((jnp.where(tril, jnp.exp(seg), 0.0) * cb).astype(BF16))
            lhs = jnp.concatenate(ms, axis=1)
            xp = xdt_b[:, pr * 2 * hd:(pr + 1) * 2 * hd]
            zero = jnp.zeros_like(xp)
            rhs = jnp.concatenate([jnp.where(lo_half, xp, zero), jnp.where(lo_half, zero, xp)], axis=0)
            sl = slice(pr * 2 * hd, (pr + 1) * 2 * hd)
            lanes_pr = pl.ds(pl.multiple_of(off + pr * 2 * hd, 2 * hd), 2 * hd)
            y_pair = _dot(lhs, rhs) + y_off[:, sl] + dexp_ref[:, lanes_pr] * xs_g[:, sl]
            yz = y_pair * z_ref[:, lanes_pr].astype(F32)
            y_ref[:, lanes_pr] = yz
            carry = carry + yz * yz
        return carry

    ssq = lax.fori_loop(0, groups, group_body, jnp.zeros((q, 2 * hd), F32))
    ms = jnp.sum(ssq, axis=-1, keepdims=True) * (1.0 / (groups * gw))
    ym_ref[...] = (y_ref[...] * lax.rsqrt(ms + EPS) * nw_ref[...]).astype(ym_ref.dtype)

    @pl.when(c == n_chunks - 1)
    def _():
        for g in range(groups):
            st_out_ref[0, g * gw:(g + 1) * gw, :] = st_ref[g].T


def _ssd_prompt(act, dt, zact, a_log, d_exp, norm_w, rep2, *, nb, seq, heads, hd, groups, ns):
    q = SSD_CHUNK
    d_inner = heads * hd
    hpg = heads // groups
    gw = hpg * hd
    n_chunks = seq // q
    gn = groups * ns
    assert heads == q and hd * 2 == q and ns == q and d_inner % gn == 0
    row = lambda b, c: b * n_chunks + c
    return pl.pallas_call(
        functools.partial(_ssd_prompt_kernel, n_chunks=n_chunks, groups=groups, hpg=hpg, hd=hd, ns=ns),
        grid=(nb, n_chunks),
        in_specs=[
            pl.BlockSpec((q, d_inner), lambda b, c: (row(b, c), 0)),
            pl.BlockSpec((q, gn), lambda b, c: (row(b, c), d_inner // gn)),
            pl.BlockSpec((q, gn), lambda b, c: (row(b, c), d_inner // gn + 1)),
            pl.BlockSpec((q, heads), lambda b, c: (row(b, c), 0)),
            pl.BlockSpec((q, d_inner), lambda b, c: (row(b, c), 0)),
            pl.BlockSpec((1, heads), lambda b, c: (0, 0)),
            pl.BlockSpec((1, d_inner), lambda b, c: (0, 0)),
            pl.BlockSpec((1, d_inner), lambda b, c: (0, 0)),
            pl.BlockSpec((2 * heads, d_inner), lambda b, c: (0, 0)),
        ],
        out_specs=[
            pl.BlockSpec((q, d_inner), lambda b, c: (row(b, c), 0)),
            pl.BlockSpec((1, d_inner, ns), lambda b, c: (b, 0, 0)),
        ],
        out_shape=[
            jax.ShapeDtypeStruct((nb * seq, d_inner), BF16),
            jax.ShapeDtypeStruct((nb, d_inner, ns), F32),
        ],
        scratch_shapes=[
            pltpu.VMEM((groups, ns, gw), F32),
            pltpu.VMEM((q, d_inner), F32),
            pltpu.VMEM((q, d_inner), F32),
            pltpu.VMEM((q, d_inner), F32),
            pltpu.VMEM((groups, q, heads), F32),
            pltpu.VMEM((heads, q), F32),
            pltpu.VMEM((q, d_inner), F32),
        ],
        compiler_params=_params(2),
        name="ssd_prompt",
    )(act, act, act, dt, zact, a_log.reshape(1, heads), d_exp, norm_w.reshape(1, d_inner), rep2)


_SAMPLE_SEQS_PER_STEP = 2


def _ssd_sample_kernel(st_ref, xs_ref, b_ref, c_ref, dt_ref, z_ref, alog_ref, dexp_ref, nw_ref, rep2_ref,
                       st_out_ref, ym_ref, *, groups, hpg, hd, ns):
    for s in range(st_ref.shape[0]):
        _ssd_sample_one(st_ref.at[s], xs_ref.at[s], b_ref.at[s], c_ref.at[s], dt_ref.at[s], z_ref.at[s],
                        alog_ref, dexp_ref, nw_ref, rep2_ref, st_out_ref.at[s], ym_ref.at[s],
                        groups=groups, hpg=hpg, hd=hd, ns=ns)


def _ssd_sample_one(st_ref, xs_ref, b_ref, c_ref, dt_ref, z_ref, alog_ref, dexp_ref, nw_ref, rep2_ref,
                    st_out_ref, ym_ref, *, groups, hpg, hd, ns):
    gw = hpg * hd
    d_inner = groups * gw
    xs = xs_ref[...].astype(F32)
    dt = dt_ref[...]
    a = -jnp.exp(alog_ref[...])
    pad = jnp.zeros((6, dt.shape[1]), F32)
    both = _expand_heads(jnp.concatenate([dt, jnp.exp(dt * a), pad], axis=0), rep2_ref)
    dte = both[0:1, :]
    dae = both[1:2, :]
    xdt = xs * dte
    kr = 2 * groups
    grp_of_lane = lax.broadcasted_iota(jnp.int32, (kr, d_inner), 1) // gw
    krow = lax.broadcasted_iota(jnp.int32, (kr, d_inner), 0)
    ltf = (jnp.where(grp_of_lane == krow, jnp.broadcast_to(xdt, (kr, d_inner)), 0.0)
           + jnp.where(krow == groups, jnp.broadcast_to(dae, (kr, d_inner)), 0.0))
    lt_hi, lt_lo = _split_hi_lo(ltf)
    lt = jnp.concatenate([lt_hi, lt_lo], axis=0)
    bmat = b_ref[...].astype(F32)
    rrow = lax.broadcasted_iota(jnp.int32, (groups, 2 * ns), 0)
    rlane = lax.broadcasted_iota(jnp.int32, (groups, 2 * ns), 1)
    r_top = jnp.concatenate([bmat, jnp.zeros_like(bmat)], axis=1)
    r_bot = jnp.where((rrow == 0) & (rlane >= ns), 1.0, 0.0)
    rtf = jnp.concatenate([r_top, r_bot], axis=0)
    rt = jnp.concatenate([rtf, rtf], axis=0).astype(BF16)
    both2 = lax.dot_general(lt, rt, (((0,), (0,)), ((), ())), preferred_element_type=F32)
    new = st_ref[...] * both2[:, ns:] + both2[:, :ns]
    st_out_ref[...] = new
    yg = _dot_nt(c_ref[...], new.astype(BF16))
    own = (lax.broadcasted_iota(jnp.int32, (groups, d_inner), 1) // gw
           == lax.broadcasted_iota(jnp.int32, (groups, d_inner), 0))
    y = jnp.sum(jnp.where(own, yg, 0.0), axis=0, keepdims=True)
    y = y + dexp_ref[...] * xs
    yz = y * z_ref[...].astype(F32)
    ms = jnp.mean(yz * yz, axis=-1, keepdims=True)
    ym_ref[...] = (yz * lax.rsqrt(ms + EPS) * nw_ref[...]).astype(ym_ref.dtype)


def _ssd_sample(state, xs, bmat, cmat, dt, zact, a_log, d_exp, norm_w, rep2, *, heads, hd, groups, ns):
    ts = state.shape[0]
    d_inner = heads * hd
    hpg = heads // groups
    spb = _SAMPLE_SEQS_PER_STEP if ts % _SAMPLE_SEQS_PER_STEP == 0 else 1
    seq_spec = lambda shape: pl.BlockSpec((spb,) + shape, lambda b: (b, 0, 0))
    const = lambda shape: pl.BlockSpec(shape, lambda b: (0, 0))
    return pl.pallas_call(
        functools.partial(_ssd_sample_kernel, groups=groups, hpg=hpg, hd=hd, ns=ns),
        grid=(ts // spb,),
        in_specs=[
            seq_spec((d_inner, ns)), seq_spec((1, d_inner)), seq_spec((groups, ns)), seq_spec((groups, ns)),
            seq_spec((1, heads)), seq_spec((1, d_inner)),
            const((1, heads)), const((1, d_inner)), const((1, d_inner)), const((2 * heads, d_inner)),
        ],
        out_specs=[seq_spec((d_inner, ns)), seq_spec((1, d_inner))],
        out_shape=[
            jax.ShapeDtypeStruct((ts, d_inner, ns), F32),
            jax.ShapeDtypeStruct((ts, 1, d_inner), F32),
        ],
        compiler_params=_params(1),
        name="ssd_sample",
    )(state, xs, bmat, cmat, dt, zact, a_log.reshape(1, heads), d_exp, norm_w.reshape(1, d_inner), rep2)


def _merge_kernel(*refs, n_p, bm, ts, has_xs, has_prev, has_wbf):
    x_ref = refs[0]
    k = 1
    xs_ref = x_ref
    if has_xs:
        xs_ref = refs[k]
        k += 1
    w_ref, ga_ref = refs[k], refs[k + 1]
    k += 2
    prev_ref = None
    if has_prev:
        prev_ref = refs[k]
        k += 1
    o_ref = refs[k]
    i = pl.program_id(1)
    w_ref = _resident_weight(w_ref, refs[k + 1] if has_wbf else None, i)

    def compute(rows, src_ref):
        for r0, r1 in _row_chunks(rows):
            acc = _dot(src_ref[r0:r1, :], w_ref[...]) * ga_ref[r0:r1, :].astype(F32)
            if has_prev:
                acc = acc + prev_ref[r0:r1, :].astype(F32)
            o_ref[r0:r1, :] = acc.astype(o_ref.dtype)

    @pl.when(i < n_p)
    def _():
        compute(bm, x_ref)

    @pl.when(i == n_p)
    def _():
        compute(ts, xs_ref)


def _merge(x, xs, w, gates, gate_col0, prev, *, tp, ts, bm, bn, out_dtype, name):
    kdim = x.shape[1]
    n = w.shape[1]
    n_p = tp // bm
    gcb = gate_col0 // bn
    assert gate_col0 % bn == 0 and n % bn == 0 and tp % bm == 0
    has_xs = xs is not None
    has_prev = prev is not None
    if has_xs:
        in_specs = [pl.BlockSpec((bm, kdim), lambda j, i: (jnp.minimum(i, n_p - 1), 0)),
                    pl.BlockSpec((ts, kdim), lambda j, i: (0, 0))]
        args = [x, xs]
    else:
        in_specs = [pl.BlockSpec((bm, kdim), lambda j, i: (i, 0))]
        args = [x]
    in_specs += [_weight_spec(kdim, bn, 0), pl.BlockSpec((bm, bn), lambda j, i: (i, gcb + j))]
    args += [w, gates]
    if has_prev:
        in_specs.append(pl.BlockSpec((bm, bn), lambda j, i: (i, j)))
        args.append(prev)
    return pl.pallas_call(
        functools.partial(_merge_kernel, n_p=n_p, bm=bm, ts=ts, has_xs=has_xs, has_prev=has_prev,
                          has_wbf=w.dtype != BF16),
        grid=(n // bn, n_p + 1),
        in_specs=in_specs,
        out_specs=pl.BlockSpec((bm, bn), lambda j, i: (i, j)),
        out_shape=jax.ShapeDtypeStruct((tp + ts, n), out_dtype),
        scratch_shapes=_weight_scratch(w, kdim, bn),
        compiler_params=_params(2),
        name=name,
    )(*args)


def _topk_rows(s, k):
    r = s.shape[0]
    rows = lax.broadcasted_iota(jnp.int32, s.shape, 0).astype(F32)
    vals, idxs = [], []
    for _ in range(k):
        m = jnp.max(s, axis=0, keepdims=True)
        idx = jnp.min(jnp.where(s == m, rows, float(r)), axis=0, keepdims=True)
        vals.append(m)
        idxs.append(idx)
        s = jnp.where(rows == idx, NEG_INF, s)
    return jnp.concatenate(vals, axis=0), jnp.concatenate(idxs, axis=0)


def _pick_rows(table, sel):
    k = table.shape[0]
    out = jnp.zeros_like(sel)
    for r in range(k):
        out = jnp.where(sel == float(r), jnp.broadcast_to(table[r:r + 1, :], sel.shape), out)
    return out


def _staircase_counts(k):
    return [k // (i + 1) for i in range(k)]


def _fold_keys_kernel(wq_ref, k1_ref, k2_ref, o_ref, *, qh, nkeys):
    hp = lax.Precision.HIGHEST
    nt = (((1,), (1,)), ((), ()))
    w = wq_ref[...]
    o_ref[:, 0:nkeys] = lax.dot_general(w[:, :qh], k1_ref[0], nt, precision=hp,
                                        preferred_element_type=F32).astype(o_ref.dtype)
    o_ref[:, nkeys:] = lax.dot_general(w[:, qh:], k2_ref[0], nt, precision=hp,
                                       preferred_element_type=F32).astype(o_ref.dtype)


def _fold_keys(wq, keys):
    d = wq.shape[0]
    _, heads, nkeys, qh = keys.shape
    return pl.pallas_call(
        functools.partial(_fold_keys_kernel, qh=qh, nkeys=nkeys),
        grid=(heads,),
        in_specs=[
            pl.BlockSpec((d, 2 * qh), lambda hh: (0, hh)),
            pl.BlockSpec((1, nkeys, qh), lambda hh: (hh, 0, 0)),
            pl.BlockSpec((1, nkeys, qh), lambda hh: (hh, 0, 0)),
        ],
        out_specs=pl.BlockSpec((d, 2 * nkeys), lambda hh: (0, hh)),
        out_shape=jax.ShapeDtypeStruct((d, heads * 2 * nkeys), BF16),
        compiler_params=_params(1),
        name="peer_fold_keys",
    )(wq, keys[0], keys[1])


_ROUTE_CHUNKS_PER_ITER = 3


def _route_kernel(h_ref, wk_ref, ia_ref, ib_ref, gt_ref, s1_ref, s2_ref, *cand_refs, nkeys, lanes):
    k = PEER_TOPK
    s = _dot(h_ref[...], wk_ref[...])
    s1_ref[...] = s[:, :nkeys].T
    s2_ref[...] = s[:, nkeys:].T
    counts = _staircase_counts(k)
    starts = [sum(counts[:i]) for i in range(k)]
    n_cand = sum(counts)

    def one_chunk(ci, cand_ref):
        off = ci * lanes if isinstance(ci, int) else pl.multiple_of(ci * lanes, lanes)
        v1, i1 = _topk_rows(s1_ref[:, pl.ds(off, lanes)], k)
        v2, i2 = _topk_rows(s2_ref[:, pl.ds(off, lanes)], k)
        for i in range(k):
            cand_ref[starts[i]:starts[i] + counts[i], :] = v1[i:i + 1, :] + v2[0:counts[i], :]
        cand_ref[n_cand:, :] = jnp.full((cand_ref.shape[0] - n_cand, lanes), NEG_INF, F32)
        sv, sp = _topk_rows(cand_ref[...], k)
        e = jnp.exp(sv - sv[0:1, :])
        gt_ref[0, :, pl.ds(off, lanes)] = e / jnp.sum(e, axis=0, keepdims=True)
        sel_i = jnp.zeros_like(sp)
        sel_start = jnp.zeros_like(sp)
        for i in range(1, k):
            ge = sp >= float(starts[i])
            sel_i = sel_i + jnp.where(ge, 1.0, 0.0)
            sel_start = sel_start + jnp.where(ge, float(counts[i - 1]), 0.0)
        ia_ref[0, :, pl.ds(off, lanes)] = _pick_rows(i1, sel_i)
        ib_ref[0, :, pl.ds(off, lanes)] = _pick_rows(i2, sp - sel_start)

    n_chunks = s1_ref.shape[1] // lanes
    per_iter = len(cand_refs)

    def chunk_group(gi, carry):
        for c, cand_ref in enumerate(cand_refs):
            one_chunk(per_iter * gi + c, cand_ref)
        return carry

    lax.fori_loop(0, n_chunks // per_iter, chunk_group, 0)
    for c in range(n_chunks % per_iter):
        one_chunk(n_chunks - n_chunks % per_iter + c, cand_refs[c])


def _route(h2, wk, *, heads, nkeys, tq):
    t, d = h2.shape
    k = PEER_TOPK
    out = jax.ShapeDtypeStruct((heads, k, t), F32)
    out_spec = pl.BlockSpec((1, k, tq), lambda i, hh: (hh, 0, i))
    return pl.pallas_call(
        functools.partial(_route_kernel, nkeys=nkeys, lanes=128),
        grid=(t // tq, heads),
        in_specs=[
            pl.BlockSpec((tq, d), lambda i, hh: (i, 0)),
            pl.BlockSpec((d, 2 * nkeys), lambda i, hh: (0, hh)),
        ],
        out_specs=[out_spec, out_spec, out_spec],
        out_shape=[out, out, out],
        scratch_shapes=[pltpu.VMEM((nkeys, tq), F32), pltpu.VMEM((nkeys, tq), F32)]
        + [pltpu.VMEM((-(-sum(_staircase_counts(k)) // 8) * 8, 128), F32)] * _ROUTE_CHUNKS_PER_ITER,
        compiler_params=_params(2),
        name="peer_route",
    )(h2, wk)


_SCATTER_UNROLL = 16
_SCATTER_PITCH = 136
_PACK_ROWS = 16


def _scatter_kernel(ia_ref, ib_ref, gt_ref, *rest, nkeys, n_cast):
    if n_cast:
        u_ref, v_ref, w_ref, ub_ref, vb_ref, ia_s, ib_s, gt_s, wsc = rest

        @pl.when(pl.program_id(0) < n_cast)
        def _():
            ub_ref[...] = u_ref[...].astype(ub_ref.dtype)
            vb_ref[...] = v_ref[...].astype(vb_ref.dtype)
    else:
        w_ref, ia_s, ib_s, gt_s, wsc = rest
    tw = ia_ref.shape[1]
    ia_s[...] = ia_ref[...].T
    ib_s[...] = ib_ref[...].T
    gt_s[...] = gt_ref[...].T
    nsel = ia_ref.shape[0]
    key_id = lax.broadcasted_iota(jnp.int32, (nkeys, nsel), 0).astype(F32)

    def token_group(gi, carry):
        for k in range(_SCATTER_UNROLL):
            tk = gi * _SCATTER_UNROLL + k
            a_row = ia_s[pl.ds(tk, 1), :]
            b_row = ib_s[pl.ds(tk, 1), :]
            g_row = gt_s[pl.ds(tk, 1), :]
            at = jnp.where(key_id == a_row, jnp.broadcast_to(g_row, key_id.shape), 0.0).astype(BF16)
            bt = jnp.where(key_id == b_row, 1.0, 0.0).astype(BF16)
            wsc[pl.ds(pl.multiple_of(tk * _SCATTER_PITCH, 8), nkeys), :] = _dot_nt(at, bt)
        return carry

    lax.fori_loop(0, tw // _SCATTER_UNROLL, token_group, 0)

    def relayout(tg, carry):
        row0 = pl.multiple_of(tg * _PACK_ROWS, _PACK_ROWS)
        base = tg * (_PACK_ROWS * _SCATTER_PITCH)
        for a in range(nkeys):
            rows = wsc[pl.ds(base + a, _PACK_ROWS, stride=_SCATTER_PITCH), :]
            w_ref[pl.ds(row0, _PACK_ROWS), a * nkeys:(a + 1) * nkeys] = rows.astype(w_ref.dtype)
        return carry

    lax.fori_loop(0, tw // _PACK_ROWS, relayout, 0)


_CAST_ROWS = 256


def _scatter(ia, ib, gt, u, v, *, nkeys, tw):
    nsel, t = ia.shape
    assert tw % _SCATTER_UNROLL == 0 and tw % _PACK_ROWS == 0
    steps = t // tw
    ne, d = u.shape
    n_cast = ne // _CAST_ROWS if (ne % _CAST_ROWS == 0 and ne // _CAST_ROWS <= steps) else 0
    spec = pl.BlockSpec((nsel, tw), lambda i: (0, i))
    in_specs = [spec, spec, spec]
    out_specs = [pl.BlockSpec((tw, nkeys * nkeys), lambda i: (i, 0))]
    out_shape = [jax.ShapeDtypeStruct((t, nkeys * nkeys), BF16)]
    args = [ia, ib, gt]
    if n_cast:
        tab_spec = pl.BlockSpec((_CAST_ROWS, d), lambda i: (jnp.minimum(i, n_cast - 1), 0))
        in_specs += [tab_spec, tab_spec]
        out_specs += [tab_spec, tab_spec]
        out_shape += [jax.ShapeDtypeStruct((ne, d), BF16)] * 2
        args += [u, v]
    outs = pl.pallas_call(
        functools.partial(_scatter_kernel, nkeys=nkeys, n_cast=n_cast),
        grid=(steps,),
        in_specs=in_specs,
        out_specs=out_specs,
        out_shape=out_shape,
        scratch_shapes=[pltpu.VMEM((tw, nsel), F32)] * 3 + [pltpu.VMEM((tw * _SCATTER_PITCH, nkeys), F32)],
        compiler_params=_params(1),
        name="peer_scatter",
    )(*args)
    if n_cast:
        return outs
    return outs[0], u.astype(BF16), v.astype(BF16)


def _experts_kernel(h_ref, u_ref, v_ref, w_ref, o_ref):
    e = pl.program_id(1)

    @pl.when(e == 0)
    def _():
        o_ref[...] = jnp.zeros(o_ref.shape, F32)

    tm = h_ref.shape[0]
    rc = EXPERT_ROW_CHUNK if tm % EXPERT_ROW_CHUNK == 0 else tm
    for r0 in range(0, tm, rc):
        s = _dot_nt(h_ref[r0:r0 + rc, :], u_ref[...])
        gelu = 0.5 * s * (1.0 + lax.erf(s * (2.0 ** -0.5)))
        act = gelu * w_ref[r0:r0 + rc, :].astype(F32)
        o_ref[r0:r0 + rc, :] += _dot(act.astype(BF16), v_ref[...])


def _experts(h2, u, v, w, *, tm, te):
    t, d = h2.shape
    ne = u.shape[0]
    return pl.pallas_call(
        _experts_kernel,
        grid=(t // tm, ne // te),
        in_specs=[
            pl.BlockSpec((tm, d), lambda i, e: (i, 0), pipeline_mode=pl.Buffered(1)),
            pl.BlockSpec((te, d), lambda i, e: (e, 0)),
            pl.BlockSpec((te, d), lambda i, e: (e, 0)),
            pl.BlockSpec((tm, te), lambda i, e: (i, e)),
        ],
        out_specs=pl.BlockSpec((tm, d), lambda i, e: (i, 0), pipeline_mode=pl.Buffered(1)),
        out_shape=jax.ShapeDtypeStruct((t, d), F32),
        compiler_params=_params(2),
        name="peer_experts",
    )(h2, u, v, w)


def _largest_divisor(n, cap, multiple):
    best = None
    for cand in range(multiple, min(n, cap) + 1, multiple):
        if n % cand == 0:
            best = cand
    assert best is not None, (n, cap, multiple)
    return best


def kernel(x_prompt, x_sample, state_shortconv, state_mamba_conv, state_ssm, ln1_w, w_in, sc_conv_w,
           sc_out_w, m_conv_w, m_conv_b, m_dt_bias, m_A_log, m_D, m_norm_w, m_out_w, w_o, ln2_w,
           peer_wq, peer_keys, peer_u, peer_v, final_norm_w):
    nb, seq, d = x_prompt.shape
    ts = x_sample.shape[0]
    assert x_sample.shape[1] == 1 and ln1_w.shape[0] == 1, "single layer, one new token per sample"
    tp = nb * seq
    t = tp + ts
    d_sc = state_shortconv.shape[-1]
    conv_dim = state_mamba_conv.shape[-1]
    _, _, heads, hd, ns = state_ssm.shape
    d_inner = heads * hd
    groups = (conv_dim - d_inner) // (2 * ns)
    gn = groups * ns
    nkeys = peer_keys.shape[3]
    c_z = 3 * d_sc
    c_xbc = c_z + d_inner
    c_dt = c_xbc + conv_dim
    c_gate = c_dt + heads

    bm = _largest_divisor(seq, 1024, 128)
    bm_small = _largest_divisor(seq, 512, 128)
    bt = _largest_divisor(seq, 256, 8)

    xp2 = x_prompt.reshape(tp, d)
    xs2 = x_sample.reshape(ts, d)
    w_in0 = w_in[0]

    h = _norm_in(xp2, xs2, ln1_w[0], bt=bt)

    g, sc_tail_p, sc_u_s = _sc_branch(h, w_in0, sc_conv_w[0], state_shortconv[0], d_sc=d_sc, tp=tp, ts=ts,
                                      seq=seq, bm=bm, bn=_largest_divisor(d_sc, 256, 128))
    act, mc_tail_p, mc_raw_s = _xbc_branch(h, w_in0, m_conv_w[0], m_conv_b[0], state_mamba_conv[0],
                                           col0=c_xbc, n=conv_dim, tp=tp, ts=ts, seq=seq, bm=bm,
                                           bn=_largest_divisor(conv_dim, 512, 128))
    zact = _mm_act(h, w_in0, col0=c_z, n=d_inner, tp=tp, ts=ts, bm=bm_small,
                   bn=_largest_divisor(d_inner, 1024, 128), act="silu", out_dtype=BF16, name="proj_z")
    dt = _mm_act(h, w_in0, col0=c_dt, n=heads, tp=tp, ts=ts, bm=bm, bn=heads, act="softplus",
                 out_dtype=F32, bias=m_dt_bias[0], name="proj_dt")
    gates = _mm_act(h, w_in0, col0=c_gate, n=2 * d, tp=tp, ts=ts, bm=bm_small,
                    bn=_largest_divisor(d, 1024, 128), act="sigmoid", out_dtype=BF16, name="proj_gates")

    d_exp = jnp.repeat(m_D[0], hd).reshape(1, d_inner)
    rep = (jnp.arange(d_inner)[None, :] // hd == jnp.arange(heads)[:, None]).astype(BF16)
    rep2 = jnp.concatenate([rep, rep], axis=0)
    ym_p, ssm_p = _ssd_prompt(act, dt, zact, m_A_log[0], d_exp, m_norm_w[0], rep2, nb=nb, seq=seq,
                              heads=heads, hd=hd, groups=groups, ns=ns)
    act_s = act[tp:]
    ssm_s, ym_s = _ssd_sample(
        state_ssm[0].reshape(ts, d_inner, ns),
        act_s[:, :d_inner].astype(F32).reshape(ts, 1, d_inner),
        act_s[:, d_inner:d_inner + gn].reshape(ts, groups, ns),
        act_s[:, d_inner + gn:].reshape(ts, groups, ns),
        dt[tp:].reshape(ts, 1, heads),
        zact[tp:].astype(F32).reshape(ts, 1, d_inner),
        m_A_log[0], d_exp, m_norm_w[0], rep2, heads=heads, hd=hd, groups=groups, ns=ns)
    bn_o = _largest_divisor(d, 512, 128)
    mix_a = _merge(g, None, sc_out_w[0], gates, 0, None, tp=tp, ts=ts, bm=bm, bn=bn_o,
                   out_dtype=F32, name="merge_a")
    mixed = _merge(ym_p, ym_s.reshape(ts, d_inner).astype(BF16), m_out_w[0].astype(BF16), gates, d, mix_a,
                   tp=tp, ts=ts, bm=bm_small, bn=bn_o, out_dtype=BF16, name="merge_b")
    x1 = _mm_act(mixed, w_o[0], col0=0, n=d, tp=tp, ts=ts, bm=bm, bn=bn_o, act="none",
                 out_dtype=F32, resid=(xp2, xs2), name="proj_o")

    tok_tile = _largest_divisor(t, 640, 128)
    h2 = _norm_mid(x1, ln2_w[0], bt=_largest_divisor(t, 256, 8))
    wk = _fold_keys(peer_wq[0], peer_keys[0])
    ia, ib, gt = _route(h2, wk, heads=peer_keys.shape[2], nkeys=nkeys, tq=_largest_divisor(t, 1664, 128))
    nsel = ia.shape[0] * ia.shape[1]
    w, u_b, v_b = _scatter(ia.reshape(nsel, t), ib.reshape(nsel, t), gt.reshape(nsel, t),
                           peer_u[0], peer_v[0], nkeys=nkeys, tw=128)
    peer = _experts(h2, u_b, v_b, w,
                    tm=tok_tile, te=_largest_divisor(nkeys * nkeys, 512, 128))
    y_p, y_s = _norm_out(x1, peer, final_norm_w, tp=tp, ts=ts, bt=bt)

    new_sc_s = jnp.stack([state_shortconv[0][:, 1], sc_u_s], axis=1)
    new_mc_s = jnp.stack([state_mamba_conv[0][:, 1], state_mamba_conv[0][:, 2], mc_raw_s], axis=1)
    return (
        y_p.reshape(nb, seq, d),
        y_s.reshape(ts, 1, d),
        sc_tail_p[None],
        mc_tail_p[None],
        ssm_p.reshape(1, nb, heads, hd, ns),
        new_sc_s[None],
        new_mc_s[None],
        ssm_s.reshape(1, ts, heads, hd, ns),
    )
```

```python
import functools

import jax
import jax.numpy as jnp
from jax import lax
from jax.experimental import pallas as pl
from jax.experimental.pallas import tpu as pltpu

F32 = jnp.float32
BF16 = jnp.bfloat16
EPS = 1e-6
PEER_TOPK = 16
SSD_CHUNK = 128
VMEM_LIMIT_BYTES = 56 * 1024 * 1024
NEG_INF = float("-inf")
LANES = 128


def _params(n_grid_axes):
    return pltpu.CompilerParams(
        dimension_semantics=("arbitrary",) * n_grid_axes,
        vmem_limit_bytes=VMEM_LIMIT_BYTES,
    )


def _dot(a, b):
    return jnp.dot(a, b, preferred_element_type=F32)


def _dot_nt(a, b):
    return lax.dot_general(a, b, (((1,), (1,)), ((), ())), preferred_element_type=F32)


def _silu(x):
    return x * jax.nn.sigmoid(x)


ROW_CHUNK = 256
EXPERT_ROW_CHUNK = 640


def _row_chunks(rows):
    rc = ROW_CHUNK if rows % ROW_CHUNK == 0 else rows
    return [(r0, r0 + rc) for r0 in range(0, rows, rc)]


def _split_hi_lo(x):
    hi = x.astype(BF16)
    lo = (x - hi.astype(F32)).astype(BF16)
    return hi, lo


def _rms(x, w):
    ms = jnp.mean(x * x, axis=-1, keepdims=True)
    return x * lax.rsqrt(ms + EPS) * w


def _norm_in_kernel(xp_ref, xs_ref, w_ref, o_ref, *, n_p, ts):
    i = pl.program_id(0)

    @pl.when(i < n_p)
    def _():
        o_ref[...] = _rms(xp_ref[...], w_ref[...]).astype(o_ref.dtype)

    @pl.when(i == n_p)
    def _():
        o_ref[0:ts, :] = _rms(xs_ref[...], w_ref[...]).astype(o_ref.dtype)


def _norm_in(xp, xs, w, *, bt):
    tp, d = xp.shape
    ts = xs.shape[0]
    n_p = tp // bt
    return pl.pallas_call(
        functools.partial(_norm_in_kernel, n_p=n_p, ts=ts),
        grid=(n_p + 1,),
        in_specs=[
            pl.BlockSpec((bt, d), lambda i: (jnp.minimum(i, n_p - 1), 0)),
            pl.BlockSpec((ts, d), lambda i: (0, 0)),
            pl.BlockSpec((1, d), lambda i: (0, 0)),
        ],
        out_specs=pl.BlockSpec((bt, d), lambda i: (i, 0)),
        out_shape=jax.ShapeDtypeStruct((tp + ts, d), BF16),
        compiler_params=_params(1),
        name="norm_in",
    )(xp, xs, w.reshape(1, d))


def _norm_mid_kernel(x_ref, w_ref, o_ref):
    o_ref[...] = _rms(x_ref[...], w_ref[...]).astype(o_ref.dtype)


def _norm_mid(x, w, *, bt):
    t, d = x.shape
    return pl.pallas_call(
        _norm_mid_kernel,
        grid=(t // bt,),
        in_specs=[pl.BlockSpec((bt, d), lambda i: (i, 0)), pl.BlockSpec((1, d), lambda i: (0, 0))],
        out_specs=pl.BlockSpec((bt, d), lambda i: (i, 0)),
        out_shape=jax.ShapeDtypeStruct((t, d), BF16),
        compiler_params=_params(1),
        name="norm_mid",
    )(x, w.reshape(1, d))


def _norm_out_kernel(x_ref, y_ref, w_ref, op_ref, os_ref, *, n_p, ts):
    i = pl.program_id(0)

    @pl.when(i < n_p)
    def _():
        op_ref[...] = _rms(x_ref[...] + y_ref[...], w_ref[...])

    @pl.when(i == n_p)
    def _():
        os_ref[...] = _rms(x_ref[0:ts, :] + y_ref[0:ts, :], w_ref[...])


def _norm_out(x, y, w, *, tp, ts, bt):
    d = x.shape[1]
    n_p = tp // bt
    return pl.pallas_call(
        functools.partial(_norm_out_kernel, n_p=n_p, ts=ts),
        grid=(n_p + 1,),
        in_specs=[
            pl.BlockSpec((bt, d), lambda i: (i, 0)),
            pl.BlockSpec((bt, d), lambda i: (i, 0)),
            pl.BlockSpec((1, d), lambda i: (0, 0)),
        ],
        out_specs=[
            pl.BlockSpec((bt, d), lambda i: (jnp.minimum(i, n_p - 1), 0)),
            pl.BlockSpec((ts, d), lambda i: (0, 0)),
        ],
        out_shape=[jax.ShapeDtypeStruct((tp, d), F32), jax.ShapeDtypeStruct((ts, d), F32)],
        compiler_params=_params(1),
        name="norm_out",
    )(x, y, w.reshape(1, d))


def _apply_act(acc, act):
    if act == "silu":
        return _silu(acc)
    if act == "sigmoid":
        return jax.nn.sigmoid(acc)
    if act == "softplus":
        return jax.nn.softplus(acc)
    assert act == "none"
    return acc


def _weight_spec(kdim, bn, col0):
    if col0 % bn == 0:
        return pl.BlockSpec((kdim, bn), lambda j, i: (0, col0 // bn + j))
    assert col0 % LANES == 0 and bn % LANES == 0
    return pl.BlockSpec((pl.Element(kdim), pl.Element(bn)),
                        lambda j, i: (0, (col0 // LANES + j * (bn // LANES)) * LANES))


def _resident_weight(w_ref, wbf_ref, i):
    if wbf_ref is None:
        return w_ref

    @pl.when(i == 0)
    def _():
        wbf_ref[...] = w_ref[...].astype(BF16)

    return wbf_ref


def _weight_scratch(w, kdim, bn):
    return [] if w.dtype == BF16 else [pltpu.VMEM((kdim, bn), BF16)]


def _mm_act_kernel(*refs, n_p, bm, ts, act, has_bias, has_resid, has_wbf):
    x_ref, w_ref = refs[0], refs[1]
    k = 2
    b_ref = None
    if has_bias:
        b_ref = refs[k]
        k += 1
    rp_ref = rs_ref = None
    if has_resid:
        rp_ref, rs_ref = refs[k], refs[k + 1]
        k += 2
    o_ref = refs[k]
    i = pl.program_id(1)
    w_ref = _resident_weight(w_ref, refs[k + 1] if has_wbf else None, i)

    def compute(rows, r_ref):
        for r0, r1 in _row_chunks(rows):
            acc = _dot(x_ref[r0:r1, :], w_ref[...])
            if has_bias:
                acc = acc + b_ref[...]
            acc = _apply_act(acc, act)
            if has_resid:
                acc = acc + r_ref[r0:r1, :]
            o_ref[r0:r1, :] = acc.astype(o_ref.dtype)

    @pl.when(i < n_p)
    def _():
        compute(bm, rp_ref)

    @pl.when(i == n_p)
    def _():
        compute(ts, rs_ref)


def _mm_act(x, w, *, col0, n, tp, ts, bm, bn, act, out_dtype, bias=None, resid=None, name):
    t, kdim = x.shape
    n_p = tp // bm
    assert tp % bm == 0 and ts <= bm and n % bn == 0
    in_specs = [
        pl.BlockSpec((bm, kdim), lambda j, i: (i, 0)),
        _weight_spec(kdim, bn, col0),
    ]
    args = [x, w]
    if bias is not None:
        in_specs.append(pl.BlockSpec((1, bn), lambda j, i: (0, j)))
        args.append(bias.reshape(1, n))
    if resid is not None:
        in_specs.append(pl.BlockSpec((bm, bn), lambda j, i: (jnp.minimum(i, n_p - 1), j)))
        in_specs.append(pl.BlockSpec((ts, bn), lambda j, i: (0, j)))
        args.extend(resid)
    return pl.pallas_call(
        functools.partial(_mm_act_kernel, n_p=n_p, bm=bm, ts=ts, act=act,
                          has_bias=bias is not None, has_resid=resid is not None,
                          has_wbf=w.dtype != BF16),
        grid=(n // bn, n_p + 1),
        in_specs=in_specs,
        out_specs=pl.BlockSpec((bm, bn), lambda j, i: (i, j)),
        out_shape=jax.ShapeDtypeStruct((t, n), out_dtype),
        scratch_shapes=_weight_scratch(w, kdim, bn),
        compiler_params=_params(2),
        name=name,
    )(*args)


_CARRY = 8


def _conv_reset(ubuf, first_in_seq):
    @pl.when(first_in_seq)
    def _():
        ubuf[0:_CARRY, :] = jnp.zeros((_CARRY, ubuf.shape[1]), F32)


def _conv_rows(ubuf, cw_ref, u, r0, *, taps):
    rc = u.shape[0]
    ubuf[_CARRY + r0:_CARRY + r0 + rc, :] = u
    return _conv_taps(ubuf, cw_ref, u, r0, taps=taps)


def _conv_taps(ubuf, cw_ref, u, r0, *, taps):
    rc = u.shape[0]
    ext = ubuf[r0:r0 + _CARRY + rc, :]
    y = None
    for k in range(taps - 1):
        shifted = pltpu.roll(ext, shift=taps - 1 - k, axis=0)[_CARRY:_CARRY + rc, :]
        term = shifted * cw_ref[k:k + 1, :]
        y = term if y is None else y + term
    y = y + u * cw_ref[taps - 1:taps, :]
    return y


def _sc_kernel(x_ref, wb_ref, wc_ref, wx_ref, cw_ref, s0_ref, s1_ref,
               g_ref, tail_ref, us_ref, ubuf, wcat, *, n_p, bm, ts, tps):
    i = pl.program_id(1)
    bn = wb_ref.shape[1]

    @pl.when(i == 0)
    def _():
        wcat[:, 0:bn] = wb_ref[...].astype(BF16)
        wcat[:, bn:2 * bn] = wc_ref[...].astype(BF16)
        wcat[:, 2 * bn:3 * bn] = wx_ref[...].astype(BF16)

    @pl.when(i < n_p)
    def _():
        _conv_reset(ubuf, (i % tps) == 0)
        for r0, r1 in _row_chunks(bm):
            p = _dot(x_ref[r0:r1, :], wcat[...])
            u = p[:, bn:2 * bn] * p[:, 2 * bn:3 * bn]
            y = _conv_rows(ubuf, cw_ref, u, r0, taps=3)
            g_ref[r0:r1, :] = (p[:, 0:bn] * y).astype(g_ref.dtype)
        tail_ref[0] = ubuf[_CARRY + bm - 2:_CARRY + bm, :]
        ubuf[0:_CARRY, :] = ubuf[bm:bm + _CARRY, :]

    @pl.when(i == n_p)
    def _():
        p = _dot(x_ref[0:ts, :], wcat[...])
        u = p[:, bn:2 * bn] * p[:, 2 * bn:3 * bn]
        us_ref[...] = u
        y = s0_ref[...] * cw_ref[0:1, :]
        y = y + s1_ref[...] * cw_ref[1:2, :]
        y = y + u * cw_ref[2:3, :]
        g_ref[0:ts, :] = (p[:, 0:bn] * y).astype(g_ref.dtype)


def _sc_branch(h, w_in, conv_w, state, *, d_sc, tp, ts, seq, bm, bn):
    t, kdim = h.shape
    n_p = tp // bm
    tps = seq // bm
    nb = tp // seq
    nj = d_sc // bn
    st = state.reshape(ts, 2 * d_sc)
    w_spec = lambda off: pl.BlockSpec((kdim, bn), lambda j, i: (0, off + j))
    return pl.pallas_call(
        functools.partial(_sc_kernel, n_p=n_p, bm=bm, ts=ts, tps=tps),
        grid=(nj, n_p + 1),
        in_specs=[
            pl.BlockSpec((bm, kdim), lambda j, i: (i, 0)),
            w_spec(0), w_spec(nj), w_spec(2 * nj),
            pl.BlockSpec((3, bn), lambda j, i: (0, j)),
            pl.BlockSpec((ts, bn), lambda j, i: (0, j)),
            pl.BlockSpec((ts, bn), lambda j, i: (0, nj + j)),
        ],
        out_specs=[
            pl.BlockSpec((bm, bn), lambda j, i: (i, j)),
            pl.BlockSpec((1, 2, bn), lambda j, i: (jnp.minimum(i, n_p - 1) // tps, 0, j)),
            pl.BlockSpec((ts, bn), lambda j, i: (0, j)),
        ],
        out_shape=[
            jax.ShapeDtypeStruct((t, d_sc), BF16),
            jax.ShapeDtypeStruct((nb, 2, d_sc), F32),
            jax.ShapeDtypeStruct((ts, d_sc), F32),
        ],
        scratch_shapes=[pltpu.VMEM((_CARRY + bm, bn), F32), pltpu.VMEM((kdim, 3 * bn), BF16)],
        compiler_params=_params(2),
        name="sc_branch",
    )(h, w_in, w_in, w_in, conv_w, st, st)


def _xbc_kernel(x_ref, w_ref, cw_ref, cb_ref, s0_ref, s1_ref, s2_ref,
                a_ref, tail_ref, rs_ref, ubuf, *wbf, n_p, bm, ts, tps):
    i = pl.program_id(1)
    w_ref = _resident_weight(w_ref, wbf[0] if wbf else None, i)

    @pl.when(i < n_p)
    def _():
        _conv_reset(ubuf, (i % tps) == 0)

        def epilogue(r0, r1):
            raw = ubuf[_CARRY + r0:_CARRY + r1, :]
            y = _conv_taps(ubuf, cw_ref, raw, r0, taps=4)
            a_ref[r0:r1, :] = _silu(y + cb_ref[...]).astype(a_ref.dtype)

        chunks = _row_chunks(bm)
        for ci, (r0, r1) in enumerate(chunks):
            ubuf[_CARRY + r0:_CARRY + r1, :] = _dot(x_ref[r0:r1, :], w_ref[...])
            if ci > 0:
                epilogue(*chunks[ci - 1])
        epilogue(*chunks[-1])
        tail_ref[0] = ubuf[_CARRY + bm - 3:_CARRY + bm, :]
        ubuf[0:_CARRY, :] = ubuf[bm:bm + _CARRY, :]

    @pl.when(i == n_p)
    def _():
        r = _dot(x_ref[0:ts, :], w_ref[...])
        rs_ref[...] = r
        y = s0_ref[...] * cw_ref[0:1, :]
        y = y + s1_ref[...] * cw_ref[1:2, :]
        y = y + s2_ref[...] * cw_ref[2:3, :]
        y = y + r * cw_ref[3:4, :]
        a_ref[0:ts, :] = _silu(y + cb_ref[...]).astype(a_ref.dtype)


def _xbc_branch(h, w_in, conv_w, conv_b, state, *, col0, n, tp, ts, seq, bm, bn):
    t, kdim = h.shape
    n_p = tp // bm
    tps = seq // bm
    nb = tp // seq
    nj = n // bn
    cb = col0 // bn
    assert col0 % bn == 0
    st = state.reshape(ts, 3 * n)
    return pl.pallas_call(
        functools.partial(_xbc_kernel, n_p=n_p, bm=bm, ts=ts, tps=tps),
        grid=(nj, n_p + 1),
        in_specs=[
            pl.BlockSpec((bm, kdim), lambda j, i: (i, 0)),
            pl.BlockSpec((kdim, bn), lambda j, i: (0, cb + j)),
            pl.BlockSpec((4, bn), lambda j, i: (0, j)),
            pl.BlockSpec((1, bn), lambda j, i: (0, j)),
            pl.BlockSpec((ts, bn), lambda j, i: (0, j)),
            pl.BlockSpec((ts, bn), lambda j, i: (0, nj + j)),
            pl.BlockSpec((ts, bn), lambda j, i: (0, 2 * nj + j)),
        ],
        out_specs=[
            pl.BlockSpec((bm, bn), lambda j, i: (i, j)),
            pl.BlockSpec((1, 3, bn), lambda j, i: (jnp.minimum(i, n_p - 1) // tps, 0, j)),
            pl.BlockSpec((ts, bn), lambda j, i: (0, j)),
        ],
        out_shape=[
            jax.ShapeDtypeStruct((t, n), BF16),
            jax.ShapeDtypeStruct((nb, 3, n), F32),
            jax.ShapeDtypeStruct((ts, n), F32),
        ],
        scratch_shapes=[pltpu.VMEM((_CARRY + bm, bn), F32)] + _weight_scratch(w_in, kdim, bn),
        compiler_params=_params(2),
        name="xbc_branch",
    )(h, w_in, conv_w, conv_b.reshape(1, n), st, st, st)


def _expand_heads(vals, rep2_ref):
    hi, lo = _split_hi_lo(vals)
    return _dot(jnp.concatenate([hi, lo], axis=1), rep2_ref[...])


def _ssd_prompt_kernel(xs_ref, b_ref, c_ref, dt_ref, z_ref, alog_ref, dexp_ref, nw_ref, rep2_ref,
                       ym_ref, st_out_ref,
                       st_ref, dte_ref, eae_ref, dee_ref, acg_ref, act_ref, y_ref,
                       *, n_chunks, groups, hpg, hd, ns):
    q = SSD_CHUNK
    c = pl.program_id(1)
    gw = hpg * hd

    @pl.when(c == 0)
    def _():
        st_ref[...] = jnp.zeros(st_ref.shape, F32)

    dt = dt_ref[...]
    a = -jnp.exp(alog_ref[...])
    da = dt * a
    row = lax.broadcasted_iota(jnp.int32, (q, q), 0)
    col = lax.broadcasted_iota(jnp.int32, (q, q), 1)
    tril = (row >= col)
    hi, lo = _split_hi_lo(da)
    trilb = jnp.where(tril, 1.0, 0.0).astype(BF16)
    mid = (da - hi.astype(F32) - lo.astype(F32)).astype(BF16)
    acum = _dot(trilb, hi) + _dot(trilb, lo) + _dot(trilb, mid)
    a_last = acum[q - 1:q, :]
    dte_ref[...] = _expand_heads(dt, rep2_ref)
    eae_ref[...] = _expand_heads(jnp.exp(acum), rep2_ref)
    dee_ref[...] = _expand_heads(jnp.exp(a_last - acum), rep2_ref)
    act_ref[...] = acum.T
    acg_ref[0] = acum
    for g in range(1, groups):
        acg_ref[g] = pltpu.roll(acum, shift=acum.shape[1] - g * hpg, axis=1)

    lane = lax.broadcasted_iota(jnp.int32, (q, 2 * hd), 1)
    lo_half = lane < hd

    def group_body(g, carry):
        off = pl.multiple_of(g * gw, gw)
        noff = pl.multiple_of(g * ns, ns)
        xs_g = xs_ref[:, pl.ds(off, gw)].astype(F32)
        b_g = b_ref[:, pl.ds(noff, ns)]
        c_g = c_ref[:, pl.ds(noff, ns)]
        bt = b_g.astype(F32).T.astype(BF16)
        cb = _dot(c_g, bt)
        xdt = xs_g * dte_ref[:, pl.ds(off, gw)]
        xdt_b = xdt.astype(BF16)
        xd_b = (xdt * dee_ref[:, pl.ds(off, gw)]).astype(BF16)
        eae_g = eae_ref[:, pl.ds(off, gw)]
        st_g = st_ref[g]
        y_off = _dot(c_g, st_g.astype(BF16)) * eae_g
        st_ref[g] = st_g * eae_g[q - 1:q, :] + _dot(bt, xd_b)
        ac = acg_ref[g]
        hoff = pl.multiple_of(g * hpg, hpg)
        ac_t = act_ref[pl.ds(hoff, hpg), :]
        for pr in range(hpg // 2):
            ms = []
            for r in (2 * pr, 2 * pr + 1):
                seg = ac[:, r:r + 1] - ac_t[r:r + 1, :]
                ms.append((jnp.where(tril, jnp.exp(seg), 0.0) * cb).astype(BF16))
            lhs = jnp.concatenate(ms, axis=1)
            xp = xdt_b[:, pr * 2 * hd:(pr + 1) * 2 * hd]
            zero = jnp.zeros_like(xp)
            rhs = jnp.concatenate([jnp.where(lo_half, xp, zero), jnp.where(lo_half, zero, xp)], axis=0)
            sl = slice(pr * 2 * hd, (pr + 1) * 2 * hd)
            lanes_pr = pl.ds(pl.multiple_of(off + pr * 2 * hd, 2 * hd), 2 * hd)
            y_pair = _dot(lhs, rhs) + y_off[:, sl] + dexp_ref[:, lanes_pr] * xs_g[:, sl]
            yz = y_pair * z_ref[:, lanes_pr].astype(F32)
            y_ref[:, lanes_pr] = yz
            carry = carry + yz * yz
        return carry

    ssq = lax.fori_loop(0, groups, group_body, jnp.zeros((q, 2 * hd), F32))
    ms = jnp.sum(ssq, axis=-1, keepdims=True) * (1.0 / (groups * gw))
    ym_ref[...] = (y_ref[...] * lax.rsqrt(ms + EPS) * nw_ref[...]).astype(ym_ref.dtype)

    @pl.when(c == n_chunks - 1)
    def _():
        for g in range(groups):
            st_out_ref[0, g * gw:(g + 1) * gw, :] = st_ref[g].T


def _ssd_prompt(act, dt, zact, a_log, d_exp, norm_w, rep2, *, nb, seq, heads, hd, groups, ns):
    q = SSD_CHUNK
    d_inner = heads * hd
    hpg = heads // groups
    gw = hpg * hd
    n_chunks = seq // q
    gn = groups * ns
    assert heads == q and hd * 2 == q and ns == q and d_inner % gn == 0
    row = lambda b, c: b * n_chunks + c
    return pl.pallas_call(
        functools.partial(_ssd_prompt_kernel, n_chunks=n_chunks, groups=groups, hpg=hpg, hd=hd, ns=ns),
        grid=(nb, n_chunks),
        in_specs=[
            pl.BlockSpec((q, d_inner), lambda b, c: (row(b, c), 0)),
            pl.BlockSpec((q, gn), lambda b, c: (row(b, c), d_inner // gn)),
            pl.BlockSpec((q, gn), lambda b, c: (row(b, c), d_inner // gn + 1)),
            pl.BlockSpec((q, heads), lambda b, c: (row(b, c), 0)),
            pl.BlockSpec((q, d_inner), lambda b, c: (row(b, c), 0)),
            pl.BlockSpec((1, heads), lambda b, c: (0, 0)),
            pl.BlockSpec((1, d_inner), lambda b, c: (0, 0)),
            pl.BlockSpec((1, d_inner), lambda b, c: (0, 0)),
            pl.BlockSpec((2 * heads, d_inner), lambda b, c: (0, 0)),
        ],
        out_specs=[
            pl.BlockSpec((q, d_inner), lambda b, c: (row(b, c), 0)),
            pl.BlockSpec((1, d_inner, ns), lambda b, c: (b, 0, 0)),
        ],
        out_shape=[
            jax.ShapeDtypeStruct((nb * seq, d_inner), BF16),
            jax.ShapeDtypeStruct((nb, d_inner, ns), F32),
        ],
        scratch_shapes=[
            pltpu.VMEM((groups, ns, gw), F32),
            pltpu.VMEM((q, d_inner), F32),
            pltpu.VMEM((q, d_inner), F32),
            pltpu.VMEM((q, d_inner), F32),
            pltpu.VMEM((groups, q, heads), F32),
            pltpu.VMEM((heads, q), F32),
            pltpu.VMEM((q, d_inner), F32),
        ],
        compiler_params=_params(2),
        name="ssd_prompt",
    )(act, act, act, dt, zact, a_log.reshape(1, heads), d_exp, norm_w.reshape(1, d_inner), rep2)


_SAMPLE_SEQS_PER_STEP = 2


def _ssd_sample_kernel(st_ref, xs_ref, b_ref, c_ref, dt_ref, z_ref, alog_ref, dexp_ref, nw_ref, rep2_ref,
                       st_out_ref, ym_ref, *, groups, hpg, hd, ns):
    for s in range(st_ref.shape[0]):
        _ssd_sample_one(st_ref.at[s], xs_ref.at[s], b_ref.at[s], c_ref.at[s], dt_ref.at[s], z_ref.at[s],
                        alog_ref, dexp_ref, nw_ref, rep2_ref, st_out_ref.at[s], ym_ref.at[s],
                        groups=groups, hpg=hpg, hd=hd, ns=ns)


def _ssd_sample_one(st_ref, xs_ref, b_ref, c_ref, dt_ref, z_ref, alog_ref, dexp_ref, nw_ref, rep2_ref,
                    st_out_ref, ym_ref, *, groups, hpg, hd, ns):
    gw = hpg * hd
    d_inner = groups * gw
    xs = xs_ref[...].astype(F32)
    dt = dt_ref[...]
    a = -jnp.exp(alog_ref[...])
    pad = jnp.zeros((6, dt.shape[1]), F32)
    both = _expand_heads(jnp.concatenate([dt, jnp.exp(dt * a), pad], axis=0), rep2_ref)
    dte = both[0:1, :]
    dae = both[1:2, :]
    xdt = xs * dte
    kr = 2 * groups
    grp_of_lane = lax.broadcasted_iota(jnp.int32, (kr, d_inner), 1) // gw
    krow = lax.broadcasted_iota(jnp.int32, (kr, d_inner), 0)
    ltf = (jnp.where(grp_of_lane == krow, jnp.broadcast_to(xdt, (kr, d_inner)), 0.0)
           + jnp.where(krow == groups, jnp.broadcast_to(dae, (kr, d_inner)), 0.0))
    lt_hi, lt_lo = _split_hi_lo(ltf)
    lt = jnp.concatenate([lt_hi, lt_lo], axis=0)
    bmat = b_ref[...].astype(F32)
    rrow = lax.broadcasted_iota(jnp.int32, (groups, 2 * ns), 0)
    rlane = lax.broadcasted_iota(jnp.int32, (groups, 2 * ns), 1)
    r_top = jnp.concatenate([bmat, jnp.zeros_like(bmat)], axis=1)
    r_bot = jnp.where((rrow == 0) & (rlane >= ns), 1.0, 0.0)
    rtf = jnp.concatenate([r_top, r_bot], axis=0)
    rt = jnp.concatenate([rtf, rtf], axis=0).astype(BF16)
    both2 = lax.dot_general(lt, rt, (((0,), (0,)), ((), ())), preferred_element_type=F32)
    new = st_ref[...] * both2[:, ns:] + both2[:, :ns]
    st_out_ref[...] = new
    yg = _dot_nt(c_ref[...], new.astype(BF16))
    own = (lax.broadcasted_iota(jnp.int32, (groups, d_inner), 1) // gw
           == lax.broadcasted_iota(jnp.int32, (groups, d_inner), 0))
    y = jnp.sum(jnp.where(own, yg, 0.0), axis=0, keepdims=True)
    y = y + dexp_ref[...] * xs
    yz = y * z_ref[...].astype(F32)
    ms = jnp.mean(yz * yz, axis=-1, keepdims=True)
    ym_ref[...] = (yz * lax.rsqrt(ms + EPS) * nw_ref[...]).astype(ym_ref.dtype)


def _ssd_sample(state, xs, bmat, cmat, dt, zact, a_log, d_exp, norm_w, rep2, *, heads, hd, groups, ns):
    ts = state.shape[0]
    d_inner = heads * hd
    hpg = heads // groups
    spb = _SAMPLE_SEQS_PER_STEP if ts % _SAMPLE_SEQS_PER_STEP == 0 else 1
    seq_spec = lambda shape: pl.BlockSpec((spb,) + shape, lambda b: (b, 0, 0))
    const = lambda shape: pl.BlockSpec(shape, lambda b: (0, 0))
    return pl.pallas_call(
        functools.partial(_ssd_sample_kernel, groups=groups, hpg=hpg, hd=hd, ns=ns),
        grid=(ts // spb,),
        in_specs=[
            seq_spec((d_inner, ns)), seq_spec((1, d_inner)), seq_spec((groups, ns)), seq_spec((groups, ns)),
            seq_spec((1, heads)), seq_spec((1, d_inner)),
            const((1, heads)), const((1, d_inner)), const((1, d_inner)), const((2 * heads, d_inner)),
        ],
        out_specs=[seq_spec((d_inner, ns)), seq_spec((1, d_inner))],
        out_shape=[
            jax.ShapeDtypeStruct((ts, d_inner, ns), F32),
            jax.ShapeDtypeStruct((ts, 1, d_inner), F32),
        ],
        compiler_params=_params(1),
        name="ssd_sample",
    )(state, xs, bmat, cmat, dt, zact, a_log.reshape(1, heads), d_exp, norm_w.reshape(1, d_inner), rep2)


def _merge_kernel(*refs, n_p, bm, ts, has_xs, has_prev, has_wbf, n_cast):
    x_ref = refs[0]
    k = 1
    xs_ref = x_ref
    if has_xs:
        xs_ref = refs[k]
        k += 1
    w_ref, ga_ref = refs[k], refs[k + 1]
    k += 2
    prev_ref = None
    if has_prev:
        prev_ref = refs[k]
        k += 1
    cast_in = None
    if n_cast:
        cast_in = refs[k]
        k += 1
    o_ref = refs[k]
    k += 1
    i = pl.program_id(1)
    if n_cast:
        cast_out = refs[k]
        k += 1

        @pl.when(pl.program_id(0) * (n_p + 1) + i < n_cast)
        def _():
            cast_out[...] = cast_in[...].astype(cast_out.dtype)

    w_ref = _resident_weight(w_ref, refs[k] if has_wbf else None, i)

    def compute(rows, src_ref):
        for r0, r1 in _row_chunks(rows):
            acc = _dot(src_ref[r0:r1, :], w_ref[...]) * ga_ref[r0:r1, :].astype(F32)
            if has_prev:
                acc = acc + prev_ref[r0:r1, :].astype(F32)
            o_ref[r0:r1, :] = acc.astype(o_ref.dtype)

    @pl.when(i < n_p)
    def _():
        compute(bm, x_ref)

    @pl.when(i == n_p)
    def _():
        compute(ts, xs_ref)


_MERGE_CAST_ROWS = 128


def _merge(x, xs, w, gates, gate_col0, prev, *, tp, ts, bm, bn, out_dtype, name, also_cast=None):
    kdim = x.shape[1]
    n = w.shape[1]
    n_p = tp // bm
    gcb = gate_col0 // bn
    assert gate_col0 % bn == 0 and n % bn == 0 and tp % bm == 0
    has_xs = xs is not None
    has_prev = prev is not None
    n_cast = 0
    if also_cast is not None:
        crow, ccol = also_cast.shape
        blocks = crow // _MERGE_CAST_ROWS
        if crow % _MERGE_CAST_ROWS == 0 and blocks <= (n // bn) * (n_p + 1):
            n_cast = blocks
    if has_xs:
        in_specs = [pl.BlockSpec((bm, kdim), lambda j, i: (jnp.minimum(i, n_p - 1), 0)),
                    pl.BlockSpec((ts, kdim), lambda j, i: (0, 0))]
        args = [x, xs]
    else:
        in_specs = [pl.BlockSpec((bm, kdim), lambda j, i: (i, 0))]
        args = [x]
    in_specs += [_weight_spec(kdim, bn, 0), pl.BlockSpec((bm, bn), lambda j, i: (i, gcb + j))]
    args += [w, gates]
    if has_prev:
        in_specs.append(pl.BlockSpec((bm, bn), lambda j, i: (i, j)))
        args.append(prev)
    out_specs = [pl.BlockSpec((bm, bn), lambda j, i: (i, j))]
    out_shape = [jax.ShapeDtypeStruct((tp + ts, n), out_dtype)]
    if n_cast:
        cast_spec = pl.BlockSpec((_MERGE_CAST_ROWS, ccol),
                                 lambda j, i: (jnp.minimum(j * (n_p + 1) + i, n_cast - 1), 0))
        in_specs.append(cast_spec)
        args.append(also_cast)
        out_specs.append(cast_spec)
        out_shape.append(jax.ShapeDtypeStruct((crow, ccol), BF16))
    outs = pl.pallas_call(
        functools.partial(_merge_kernel, n_p=n_p, bm=bm, ts=ts, has_xs=has_xs, has_prev=has_prev,
                          has_wbf=w.dtype != BF16, n_cast=n_cast),
        grid=(n // bn, n_p + 1),
        in_specs=in_specs,
        out_specs=out_specs,
        out_shape=out_shape,
        scratch_shapes=_weight_scratch(w, kdim, bn),
        compiler_params=_params(2),
        name=name,
    )(*args)
    if also_cast is None:
        return outs[0]
    return outs[0], (outs[1] if n_cast else also_cast.astype(BF16))


def _topk_rows(s, k):
    r = s.shape[0]
    rows = lax.broadcasted_iota(jnp.int32, s.shape, 0).astype(F32)
    vals, idxs = [], []
    for _ in range(k):
        m = jnp.max(s, axis=0, keepdims=True)
        idx = jnp.min(jnp.where(s == m, rows, float(r)), axis=0, keepdims=True)
        vals.append(m)
        idxs.append(idx)
        s = jnp.where(rows == idx, NEG_INF, s)
    return jnp.concatenate(vals, axis=0), jnp.concatenate(idxs, axis=0)


def _pick_rows(table, sel):
    k = table.shape[0]
    out = jnp.zeros_like(sel)
    for r in range(k):
        out = jnp.where(sel == float(r), jnp.broadcast_to(table[r:r + 1, :], sel.shape), out)
    return out


def _staircase_counts(k):
    return [k // (i + 1) for i in range(k)]


def _fold_keys_kernel(wq_ref, k1_ref, k2_ref, o_ref, *, qh, nkeys):
    hp = lax.Precision.HIGHEST
    nt = (((1,), (1,)), ((), ()))
    w = wq_ref[...]
    o_ref[:, 0:nkeys] = lax.dot_general(w[:, :qh], k1_ref[0], nt, precision=hp,
                                        preferred_element_type=F32).astype(o_ref.dtype)
    o_ref[:, nkeys:] = lax.dot_general(w[:, qh:], k2_ref[0], nt, precision=hp,
                                       preferred_element_type=F32).astype(o_ref.dtype)


def _fold_keys(wq, keys):
    d = wq.shape[0]
    _, heads, nkeys, qh = keys.shape
    return pl.pallas_call(
        functools.partial(_fold_keys_kernel, qh=qh, nkeys=nkeys),
        grid=(heads,),
        in_specs=[
            pl.BlockSpec((d, 2 * qh), lambda hh: (0, hh)),
            pl.BlockSpec((1, nkeys, qh), lambda hh: (hh, 0, 0)),
            pl.BlockSpec((1, nkeys, qh), lambda hh: (hh, 0, 0)),
        ],
        out_specs=pl.BlockSpec((d, 2 * nkeys), lambda hh: (0, hh)),
        out_shape=jax.ShapeDtypeStruct((d, heads * 2 * nkeys), BF16),
        compiler_params=_params(1),
        name="peer_fold_keys",
    )(wq, keys[0], keys[1])


_ROUTE_CHUNKS_PER_ITER = 3


def _route_kernel(h_ref, wk_ref, ia_ref, ib_ref, gt_ref, s1_ref, s2_ref, *cand_refs, nkeys, lanes):
    k = PEER_TOPK
    s = _dot(h_ref[...], wk_ref[...])
    s1_ref[...] = s[:, :nkeys].T
    s2_ref[...] = s[:, nkeys:].T
    counts = _staircase_counts(k)
    starts = [sum(counts[:i]) for i in range(k)]
    n_cand = sum(counts)

    def one_chunk(ci, cand_ref):
        off = ci * lanes if isinstance(ci, int) else pl.multiple_of(ci * lanes, lanes)
        v1, i1 = _topk_rows(s1_ref[:, pl.ds(off, lanes)], k)
        v2, i2 = _topk_rows(s2_ref[:, pl.ds(off, lanes)], k)
        for i in range(k):
            cand_ref[starts[i]:starts[i] + counts[i], :] = v1[i:i + 1, :] + v2[0:counts[i], :]
        cand_ref[n_cand:, :] = jnp.full((cand_ref.shape[0] - n_cand, lanes), NEG_INF, F32)
        sv, sp = _topk_rows(cand_ref[...], k)
        e = jnp.exp(sv - sv[0:1, :])
        gt_ref[0, :, pl.ds(off, lanes)] = e / jnp.sum(e, axis=0, keepdims=True)
        sel_i = jnp.zeros_like(sp)
        sel_start = jnp.zeros_like(sp)
        for i in range(1, k):
            ge = sp >= float(starts[i])
            sel_i = sel_i + jnp.where(ge, 1.0, 0.0)
            sel_start = sel_start + jnp.where(ge, float(counts[i - 1]), 0.0)
        ia_ref[0, :, pl.ds(off, lanes)] = _pick_rows(i1, sel_i)
        ib_ref[0, :, pl.ds(off, lanes)] = _pick_rows(i2, sp - sel_start)

    n_chunks = s1_ref.shape[1] // lanes
    per_iter = len(cand_refs)

    def chunk_group(gi, carry):
        for c, cand_ref in enumerate(cand_refs):
            one_chunk(per_iter * gi + c, cand_ref)
        return carry

    lax.fori_loop(0, n_chunks // per_iter, chunk_group, 0)
    for c in range(n_chunks % per_iter):
        one_chunk(n_chunks - n_chunks % per_iter + c, cand_refs[c])


def _route(h2, wk, *, heads, nkeys, tq):
    t, d = h2.shape
    k = PEER_TOPK
    out = jax.ShapeDtypeStruct((heads, k, t), F32)
    out_spec = pl.BlockSpec((1, k, tq), lambda i, hh: (hh, 0, i))
    return pl.pallas_call(
        functools.partial(_route_kernel, nkeys=nkeys, lanes=128),
        grid=(t // tq, heads),
        in_specs=[
            pl.BlockSpec((tq, d), lambda i, hh: (i, 0)),
            pl.BlockSpec((d, 2 * nkeys), lambda i, hh: (0, hh)),
        ],
        out_specs=[out_spec, out_spec, out_spec],
        out_shape=[out, out, out],
        scratch_shapes=[pltpu.VMEM((nkeys, tq), F32), pltpu.VMEM((nkeys, tq), F32)]
        + [pltpu.VMEM((-(-sum(_staircase_counts(k)) // 8) * 8, 128), F32)] * _ROUTE_CHUNKS_PER_ITER,
        compiler_params=_params(2),
        name="peer_route",
    )(h2, wk)


_SCATTER_UNROLL = 16
_SCATTER_PITCH = 136
_PACK_ROWS = 16


def _scatter_kernel(ia_ref, ib_ref, gt_ref, *rest, nkeys, n_cast):
    if n_cast:
        u_ref, v_ref, w_ref, ub_ref, vb_ref, ia_s, ib_s, gt_s, wsc = rest

        @pl.when(pl.program_id(0) < n_cast)
        def _():
            ub_ref[...] = u_ref[...].astype(ub_ref.dtype)
            vb_ref[...] = v_ref[...].astype(vb_ref.dtype)
    else:
        w_ref, ia_s, ib_s, gt_s, wsc = rest
    tw = ia_ref.shape[1]
    ia_s[...] = ia_ref[...].T
    ib_s[...] = ib_ref[...].T
    gt_s[...] = gt_ref[...].T
    nsel = ia_ref.shape[0]
    key_id = lax.broadcasted_iota(jnp.int32, (nkeys, nsel), 0).astype(F32)

    def token_group(gi, carry):
        for k in range(_SCATTER_UNROLL):
            tk = gi * _SCATTER_UNROLL + k
            a_row = ia_s[pl.ds(tk, 1), :]
            b_row = ib_s[pl.ds(tk, 1), :]
            g_row = gt_s[pl.ds(tk, 1), :]
            at = jnp.where(key_id == a_row, jnp.broadcast_to(g_row, key_id.shape), 0.0).astype(BF16)
            bt = jnp.where(key_id == b_row, 1.0, 0.0).astype(BF16)
            wsc[pl.ds(pl.multiple_of(tk * _SCATTER_PITCH, 8), nkeys), :] = _dot_nt(at, bt)
        return carry

    lax.fori_loop(0, tw // _SCATTER_UNROLL, token_group, 0)

    def relayout(tg, carry):
        row0 = pl.multiple_of(tg * _PACK_ROWS, _PACK_ROWS)
        base = tg * (_PACK_ROWS * _SCATTER_PITCH)
        for a in range(nkeys):
            rows = wsc[pl.ds(base + a, _PACK_ROWS, stride=_SCATTER_PITCH), :]
            w_ref[pl.ds(row0, _PACK_ROWS), a * nkeys:(a + 1) * nkeys] = rows.astype(w_ref.dtype)
        return carry

    lax.fori_loop(0, tw // _PACK_ROWS, relayout, 0)


_CAST_ROWS = 256


def _scatter(ia, ib, gt, u, v, *, nkeys, tw):
    nsel, t = ia.shape
    assert tw % _SCATTER_UNROLL == 0 and tw % _PACK_ROWS == 0
    steps = t // tw
    ne, d = u.shape
    n_cast = ne // _CAST_ROWS if (ne % _CAST_ROWS == 0 and ne // _CAST_ROWS <= steps) else 0
    spec = pl.BlockSpec((nsel, tw), lambda i: (0, i))
    in_specs = [spec, spec, spec]
    out_specs = [pl.BlockSpec((tw, nkeys * nkeys), lambda i: (i, 0))]
    out_shape = [jax.ShapeDtypeStruct((t, nkeys * nkeys), BF16)]
    args = [ia, ib, gt]
    if n_cast:
        tab_spec = pl.BlockSpec((_CAST_ROWS, d), lambda i: (jnp.minimum(i, n_cast - 1), 0))
        in_specs += [tab_spec, tab_spec]
        out_specs += [tab_spec, tab_spec]
        out_shape += [jax.ShapeDtypeStruct((ne, d), BF16)] * 2
        args += [u, v]
    outs = pl.pallas_call(
        functools.partial(_scatter_kernel, nkeys=nkeys, n_cast=n_cast),
        grid=(steps,),
        in_specs=in_specs,
        out_specs=out_specs,
        out_shape=out_shape,
        scratch_shapes=[pltpu.VMEM((tw, nsel), F32)] * 3 + [pltpu.VMEM((tw * _SCATTER_PITCH, nkeys), F32)],
        compiler_params=_params(1),
        name="peer_scatter",
    )(*args)
    if n_cast:
        return outs
    return outs[0], u.astype(BF16), v.astype(BF16)


def _experts_kernel(h_ref, u_ref, v_ref, w_ref, o_ref):
    e = pl.program_id(1)

    @pl.when(e == 0)
    def _():
        o_ref[...] = jnp.zeros(o_ref.shape, F32)

    tm = h_ref.shape[0]
    rc = EXPERT_ROW_CHUNK if tm % EXPERT_ROW_CHUNK == 0 else tm
    for r0 in range(0, tm, rc):
        s = _dot_nt(h_ref[r0:r0 + rc, :], u_ref[...])
        gelu = 0.5 * s * (1.0 + lax.erf(s * (2.0 ** -0.5)))
        act = gelu * w_ref[r0:r0 + rc, :].astype(F32)
        o_ref[r0:r0 + rc, :] += _dot(act.astype(BF16), v_ref[...])


def _experts(h2, u, v, w, *, tm, te):
    t, d = h2.shape
    ne = u.shape[0]
    return pl.pallas_call(
        _experts_kernel,
        grid=(t // tm, ne // te),
        in_specs=[
            pl.BlockSpec((tm, d), lambda i, e: (i, 0), pipeline_mode=pl.Buffered(1)),
            pl.BlockSpec((te, d), lambda i, e: (e, 0)),
            pl.BlockSpec((te, d), lambda i, e: (e, 0)),
            pl.BlockSpec((tm, te), lambda i, e: (i, e)),
        ],
        out_specs=pl.BlockSpec((tm, d), lambda i, e: (i, 0), pipeline_mode=pl.Buffered(1)),
        out_shape=jax.ShapeDtypeStruct((t, d), F32),
        compiler_params=_params(2),
        name="peer_experts",
    )(h2, u, v, w)


def _largest_divisor(n, cap, multiple):
    best = None
    for cand in range(multiple, min(n, cap) + 1, multiple):
        if n % cand == 0:
            best = cand
    assert best is not None, (n, cap, multiple)
    return best


def kernel(x_prompt, x_sample, state_shortconv, state_mamba_conv, state_ssm, ln1_w, w_in, sc_conv_w,
           sc_out_w, m_conv_w, m_conv_b, m_dt_bias, m_A_log, m_D, m_norm_w, m_out_w, w_o, ln2_w,
           peer_wq, peer_keys, peer_u, peer_v, final_norm_w):
    nb, seq, d = x_prompt.shape
    ts = x_sample.shape[0]
    assert x_sample.shape[1] == 1 and ln1_w.shape[0] == 1, "single layer, one new token per sample"
    tp = nb * seq
    t = tp + ts
    d_sc = state_shortconv.shape[-1]
    conv_dim = state_mamba_conv.shape[-1]
    _, _, heads, hd, ns = state_ssm.shape
    d_inner = heads * hd
    groups = (conv_dim - d_inner) // (2 * ns)
    gn = groups * ns
    nkeys = peer_keys.shape[3]
    c_z = 3 * d_sc
    c_xbc = c_z + d_inner
    c_dt = c_xbc + conv_dim
    c_gate = c_dt + heads

    bm = _largest_divisor(seq, 1024, 128)
    bm_small = _largest_divisor(seq, 512, 128)
    bt = _largest_divisor(seq, 256, 8)

    xp2 = x_prompt.reshape(tp, d)
    xs2 = x_sample.reshape(ts, d)
    w_in0 = w_in[0]

    h = _norm_in(xp2, xs2, ln1_w[0], bt=bt)

    g, sc_tail_p, sc_u_s = _sc_branch(h, w_in0, sc_conv_w[0], state_shortconv[0], d_sc=d_sc, tp=tp, ts=ts,
                                      seq=seq, bm=bm, bn=_largest_divisor(d_sc, 256, 128))
    act, mc_tail_p, mc_raw_s = _xbc_branch(h, w_in0, m_conv_w[0], m_conv_b[0], state_mamba_conv[0],
                                           col0=c_xbc, n=conv_dim, tp=tp, ts=ts, seq=seq, bm=bm,
                                           bn=_largest_divisor(conv_dim, 512, 128))
    zact = _mm_act(h, w_in0, col0=c_z, n=d_inner, tp=tp, ts=ts, bm=bm_small,
                   bn=_largest_divisor(d_inner, 1024, 128), act="silu", out_dtype=BF16, name="proj_z")
    dt = _mm_act(h, w_in0, col0=c_dt, n=heads, tp=tp, ts=ts, bm=bm, bn=heads, act="softplus",
                 out_dtype=F32, bias=m_dt_bias[0], name="proj_dt")
    gates = _mm_act(h, w_in0, col0=c_gate, n=2 * d, tp=tp, ts=ts, bm=bm_small,
                    bn=_largest_divisor(d, 1024, 128), act="sigmoid", out_dtype=BF16, name="proj_gates")

    d_exp = jnp.repeat(m_D[0], hd).reshape(1, d_inner)
    rep = (jnp.arange(d_inner)[None, :] // hd == jnp.arange(heads)[:, None]).astype(BF16)
    rep2 = jnp.concatenate([rep, rep], axis=0)
    ym_p, ssm_p = _ssd_prompt(act, dt, zact, m_A_log[0], d_exp, m_norm_w[0], rep2, nb=nb, seq=seq,
                              heads=heads, hd=hd, groups=groups, ns=ns)
    act_s = act[tp:]
    ssm_s, ym_s = _ssd_sample(
        state_ssm[0].reshape(ts, d_inner, ns),
        act_s[:, :d_inner].astype(F32).reshape(ts, 1, d_inner),
        act_s[:, d_inner:d_inner + gn].reshape(ts, groups, ns),
        act_s[:, d_inner + gn:].reshape(ts, groups, ns),
        dt[tp:].reshape(ts, 1, heads),
        zact[tp:].astype(F32).reshape(ts, 1, d_inner),
        m_A_log[0], d_exp, m_norm_w[0], rep2, heads=heads, hd=hd, groups=groups, ns=ns)
    bn_o = _largest_divisor(d, 512, 128)
    mix_a, m_out_b = _merge(g, None, sc_out_w[0], gates, 0, None, tp=tp, ts=ts, bm=bm, bn=bn_o,
                            out_dtype=F32, name="merge_a", also_cast=m_out_w[0])
    mixed = _merge(ym_p, ym_s.reshape(ts, d_inner).astype(BF16), m_out_b, gates, d, mix_a,
                   tp=tp, ts=ts, bm=bm_small, bn=bn_o, out_dtype=BF16, name="merge_b")
    x1 = _mm_act(mixed, w_o[0], col0=0, n=d, tp=tp, ts=ts, bm=bm, bn=bn_o, act="none",
                 out_dtype=F32, resid=(xp2, xs2), name="proj_o")

    tok_tile = _largest_divisor(t, 640, 128)
    h2 = _norm_mid(x1, ln2_w[0], bt=_largest_divisor(t, 256, 8))
    wk = _fold_keys(peer_wq[0], peer_keys[0])
    ia, ib, gt = _route(h2, wk, heads=peer_keys.shape[2], nkeys=nkeys, tq=_largest_divisor(t, 1664, 128))
    nsel = ia.shape[0] * ia.shape[1]
    w, u_b, v_b = _scatter(ia.reshape(nsel, t), ib.reshape(nsel, t), gt.reshape(nsel, t),
                           peer_u[0], peer_v[0], nkeys=nkeys, tw=128)
    peer = _experts(h2, u_b, v_b, w,
                    tm=tok_tile, te=_largest_divisor(nkeys * nkeys, 512, 128))
    y_p, y_s = _norm_out(x1, peer, final_norm_w, tp=tp, ts=ts, bt=bt)

    new_sc_s = jnp.stack([state_shortconv[0][:, 1], sc_u_s], axis=1)
    new_mc_s = jnp.stack([state_mamba_conv[0][:, 1], state_mamba_conv[0][:, 2], mc_raw_s], axis=1)
    return (
        y_p.reshape(nb, seq, d),
        y_s.reshape(ts, 1, d),
        sc_tail_p[None],
        mc_tail_p[None],
        ssm_p.reshape(1, nb, heads, hd, ns),
        new_sc_s[None],
        new_mc_s[None],
        ssm_s.reshape(1, ts, heads, hd, ns),
    )
```

```python
import functools

import jax
import jax.numpy as jnp
from jax import lax
from jax.experimental import pallas as pl
from jax.experimental.pallas import tpu as pltpu

F32 = jnp.float32
BF16 = jnp.bfloat16
EPS = 1e-6
PEER_TOPK = 16
SSD_CHUNK = 128
VMEM_LIMIT_BYTES = 56 * 1024 * 1024
NEG_INF = float("-inf")
LANES = 128


def _params(n_grid_axes):
    return pltpu.CompilerParams(
        dimension_semantics=("arbitrary",) * n_grid_axes,
        vmem_limit_bytes=VMEM_LIMIT_BYTES,
    )


def _dot(a, b):
    return jnp.dot(a, b, preferred_element_type=F32)


def _dot_nt(a, b):
    return lax.dot_general(a, b, (((1,), (1,)), ((), ())), preferred_element_type=F32)


def _silu(x):
    return x * jax.nn.sigmoid(x)


ROW_CHUNK = 256
EXPERT_ROW_CHUNK = 640


def _row_chunks(rows):
    rc = ROW_CHUNK if rows % ROW_CHUNK == 0 else rows
    return [(r0, r0 + rc) for r0 in range(0, rows, rc)]


def _split_hi_lo(x):
    hi = x.astype(BF16)
    lo = (x - hi.astype(F32)).astype(BF16)
    return hi, lo


def _rms(x, w):
    ms = jnp.mean(x * x, axis=-1, keepdims=True)
    return x * lax.rsqrt(ms + EPS) * w


def _norm_in_kernel(xp_ref, xs_ref, w_ref, o_ref, *, n_p, ts):
    i = pl.program_id(0)

    @pl.when(i < n_p)
    def _():
        o_ref[...] = _rms(xp_ref[...], w_ref[...]).astype(o_ref.dtype)

    @pl.when(i == n_p)
    def _():
        o_ref[0:ts, :] = _rms(xs_ref[...], w_ref[...]).astype(o_ref.dtype)


def _norm_in(xp, xs, w, *, bt):
    tp, d = xp.shape
    ts = xs.shape[0]
    n_p = tp // bt
    return pl.pallas_call(
        functools.partial(_norm_in_kernel, n_p=n_p, ts=ts),
        grid=(n_p + 1,),
        in_specs=[
            pl.BlockSpec((bt, d), lambda i: (jnp.minimum(i, n_p - 1), 0)),
            pl.BlockSpec((ts, d), lambda i: (0, 0)),
            pl.BlockSpec((1, d), lambda i: (0, 0)),
        ],
        out_specs=pl.BlockSpec((bt, d), lambda i: (i, 0)),
        out_shape=jax.ShapeDtypeStruct((tp + ts, d), BF16),
        compiler_params=_params(1),
        name="norm_in",
    )(xp, xs, w.reshape(1, d))


def _norm_mid_kernel(x_ref, w_ref, o_ref):
    o_ref[...] = _rms(x_ref[...], w_ref[...]).astype(o_ref.dtype)


def _norm_mid(x, w, *, bt):
    t, d = x.shape
    return pl.pallas_call(
        _norm_mid_kernel,
        grid=(t // bt,),
        in_specs=[pl.BlockSpec((bt, d), lambda i: (i, 0)), pl.BlockSpec((1, d), lambda i: (0, 0))],
        out_specs=pl.BlockSpec((bt, d), lambda i: (i, 0)),
        out_shape=jax.ShapeDtypeStruct((t, d), BF16),
        compiler_params=_params(1),
        name="norm_mid",
    )(x, w.reshape(1, d))


def _norm_out_kernel(x_ref, y_ref, w_ref, op_ref, os_ref, *, n_p, ts):
    i = pl.program_id(0)

    @pl.when(i < n_p)
    def _():
        op_ref[...] = _rms(x_ref[...] + y_ref[...], w_ref[...])

    @pl.when(i == n_p)
    def _():
        os_ref[...] = _rms(x_ref[0:ts, :] + y_ref[0:ts, :], w_ref[...])


def _norm_out(x, y, w, *, tp, ts, bt):
    d = x.shape[1]
    n_p = tp // bt
    return pl.pallas_call(
        functools.partial(_norm_out_kernel, n_p=n_p, ts=ts),
        grid=(n_p + 1,),
        in_specs=[
            pl.BlockSpec((bt, d), lambda i: (i, 0)),
            pl.BlockSpec((bt, d), lambda i: (i, 0)),
            pl.BlockSpec((1, d), lambda i: (0, 0)),
        ],
        out_specs=[
            pl.BlockSpec((bt, d), lambda i: (jnp.minimum(i, n_p - 1), 0)),
            pl.BlockSpec((ts, d), lambda i: (0, 0)),
        ],
        out_shape=[jax.ShapeDtypeStruct((tp, d), F32), jax.ShapeDtypeStruct((ts, d), F32)],
        compiler_params=_params(1),
        name="norm_out",
    )(x, y, w.reshape(1, d))


def _apply_act(acc, act):
    if act == "silu":
        return _silu(acc)
    if act == "sigmoid":
        return jax.nn.sigmoid(acc)
    if act == "softplus":
        return jax.nn.softplus(acc)
    assert act == "none"
    return acc


def _weight_spec(kdim, bn, col0):
    if col0 % bn == 0:
        return pl.BlockSpec((kdim, bn), lambda j, i: (0, col0 // bn + j))
    assert col0 % LANES == 0 and bn % LANES == 0
    return pl.BlockSpec((pl.Element(kdim), pl.Element(bn)),
                        lambda j, i: (0, (col0 // LANES + j * (bn // LANES)) * LANES))


def _resident_weight(w_ref, wbf_ref, i):
    if wbf_ref is None:
        return w_ref

    @pl.when(i == 0)
    def _():
        wbf_ref[...] = w_ref[...].astype(BF16)

    return wbf_ref


def _weight_scratch(w, kdim, bn):
    return [] if w.dtype == BF16 else [pltpu.VMEM((kdim, bn), BF16)]


def _mm_act_kernel(*refs, n_p, bm, ts, act, has_bias, has_resid, has_wbf):
    x_ref, w_ref = refs[0], refs[1]
    k = 2
    b_ref = None
    if has_bias:
        b_ref = refs[k]
        k += 1
    rp_ref = rs_ref = None
    if has_resid:
        rp_ref, rs_ref = refs[k], refs[k + 1]
        k += 2
    o_ref = refs[k]
    i = pl.program_id(1)
    w_ref = _resident_weight(w_ref, refs[k + 1] if has_wbf else None, i)

    def compute(rows, r_ref):
        for r0, r1 in _row_chunks(rows):
            acc = _dot(x_ref[r0:r1, :], w_ref[...])
            if has_bias:
                acc = acc + b_ref[...]
            acc = _apply_act(acc, act)
            if has_resid:
                acc = acc + r_ref[r0:r1, :]
            o_ref[r0:r1, :] = acc.astype(o_ref.dtype)

    @pl.when(i < n_p)
    def _():
        compute(bm, rp_ref)

    @pl.when(i == n_p)
    def _():
        compute(ts, rs_ref)


def _mm_act(x, w, *, col0, n, tp, ts, bm, bn, act, out_dtype, bias=None, resid=None, name):
    t, kdim = x.shape
    n_p = tp // bm
    assert tp % bm == 0 and ts <= bm and n % bn == 0
    in_specs = [
        pl.BlockSpec((bm, kdim), lambda j, i: (i, 0)),
        _weight_spec(kdim, bn, col0),
    ]
    args = [x, w]
    if bias is not None:
        in_specs.append(pl.BlockSpec((1, bn), lambda j, i: (0, j)))
        args.append(bias.reshape(1, n))
    if resid is not None:
        in_specs.append(pl.BlockSpec((bm, bn), lambda j, i: (jnp.minimum(i, n_p - 1), j)))
        in_specs.append(pl.BlockSpec((ts, bn), lambda j, i: (0, j)))
        args.extend(resid)
    return pl.pallas_call(
        functools.partial(_mm_act_kernel, n_p=n_p, bm=bm, ts=ts, act=act,
                          has_bias=bias is not None, has_resid=resid is not None,
                          has_wbf=w.dtype != BF16),
        grid=(n // bn, n_p + 1),
        in_specs=in_specs,
        out_specs=pl.BlockSpec((bm, bn), lambda j, i: (i, j)),
        out_shape=jax.ShapeDtypeStruct((t, n), out_dtype),
        scratch_shapes=_weight_scratch(w, kdim, bn),
        compiler_params=_params(2),
        name=name,
    )(*args)


_CARRY = 8


def _conv_reset(ubuf, first_in_seq):
    @pl.when(first_in_seq)
    def _():
        ubuf[0:_CARRY, :] = jnp.zeros((_CARRY, ubuf.shape[1]), F32)


def _conv_rows(ubuf, cw_ref, u, r0, *, taps):
    rc = u.shape[0]
    ubuf[_CARRY + r0:_CARRY + r0 + rc, :] = u
    return _conv_taps(ubuf, cw_ref, u, r0, taps=taps)


def _conv_taps(ubuf, cw_ref, u, r0, *, taps):
    rc = u.shape[0]
    ext = ubuf[r0:r0 + _CARRY + rc, :]
    y = None
    for k in range(taps - 1):
        shifted = pltpu.roll(ext, shift=taps - 1 - k, axis=0)[_CARRY:_CARRY + rc, :]
        term = shifted * cw_ref[k:k + 1, :]
        y = term if y is None else y + term
    y = y + u * cw_ref[taps - 1:taps, :]
    return y


def _sc_kernel(x_ref, wb_ref, wc_ref, wx_ref, cw_ref, s0_ref, s1_ref,
               g_ref, tail_ref, us_ref, ubuf, wcat, *, n_p, bm, ts, tps):
    i = pl.program_id(1)
    bn = wb_ref.shape[1]

    @pl.when(i == 0)
    def _():
        wcat[:, 0:bn] = wb_ref[...].astype(BF16)
        wcat[:, bn:2 * bn] = wc_ref[...].astype(BF16)
        wcat[:, 2 * bn:3 * bn] = wx_ref[...].astype(BF16)

    @pl.when(i < n_p)
    def _():
        _conv_reset(ubuf, (i % tps) == 0)
        for r0, r1 in _row_chunks(bm):
            p = _dot(x_ref[r0:r1, :], wcat[...])
            u = p[:, bn:2 * bn] * p[:, 2 * bn:3 * bn]
            y = _conv_rows(ubuf, cw_ref, u, r0, taps=3)
            g_ref[r0:r1, :] = (p[:, 0:bn] * y).astype(g_ref.dtype)
        tail_ref[0] = ubuf[_CARRY + bm - 2:_CARRY + bm, :]
        ubuf[0:_CARRY, :] = ubuf[bm:bm + _CARRY, :]

    @pl.when(i == n_p)
    def _():
        p = _dot(x_ref[0:ts, :], wcat[...])
        u = p[:, bn:2 * bn] * p[:, 2 * bn:3 * bn]
        us_ref[...] = u
        y = s0_ref[...] * cw_ref[0:1, :]
        y = y + s1_ref[...] * cw_ref[1:2, :]
        y = y + u * cw_ref[2:3, :]
        g_ref[0:ts, :] = (p[:, 0:bn] * y).astype(g_ref.dtype)


def _sc_branch(h, w_in, conv_w, state, *, d_sc, tp, ts, seq, bm, bn):
    t, kdim = h.shape
    n_p = tp // bm
    tps = seq // bm
    nb = tp // seq
    nj = d_sc // bn
    st = state.reshape(ts, 2 * d_sc)
    w_spec = lambda off: pl.BlockSpec((kdim, bn), lambda j, i: (0, off + j))
    return pl.pallas_call(
        functools.partial(_sc_kernel, n_p=n_p, bm=bm, ts=ts, tps=tps),
        grid=(nj, n_p + 1),
        in_specs=[
            pl.BlockSpec((bm, kdim), lambda j, i: (i, 0)),
            w_spec(0), w_spec(nj), w_spec(2 * nj),
            pl.BlockSpec((3, bn), lambda j, i: (0, j)),
            pl.BlockSpec((ts, bn), lambda j, i: (0, j)),
            pl.BlockSpec((ts, bn), lambda j, i: (0, nj + j)),
        ],
        out_specs=[
            pl.BlockSpec((bm, bn), lambda j, i: (i, j)),
            pl.BlockSpec((1, 2, bn), lambda j, i: (jnp.minimum(i, n_p - 1) // tps, 0, j)),
            pl.BlockSpec((ts, bn), lambda j, i: (0, j)),
        ],
        out_shape=[
            jax.ShapeDtypeStruct((t, d_sc), BF16),
            jax.ShapeDtypeStruct((nb, 2, d_sc), F32),
            jax.ShapeDtypeStruct((ts, d_sc), F32),
        ],
        scratch_shapes=[pltpu.VMEM((_CARRY + bm, bn), F32), pltpu.VMEM((kdim, 3 * bn), BF16)],
        compiler_params=_params(2),
        name="sc_branch",
    )(h, w_in, w_in, w_in, conv_w, st, st)


def _xbc_kernel(x_ref, w_ref, cw_ref, cb_ref, s0_ref, s1_ref, s2_ref,
                a_ref, tail_ref, rs_ref, ubuf, *wbf, n_p, bm, ts, tps):
    i = pl.program_id(1)
    w_ref = _resident_weight(w_ref, wbf[0] if wbf else None, i)

    @pl.when(i < n_p)
    def _():
        _conv_reset(ubuf, (i % tps) == 0)

        def epilogue(r0, r1):
            raw = ubuf[_CARRY + r0:_CARRY + r1, :]
            y = _conv_taps(ubuf, cw_ref, raw, r0, taps=4)
            a_ref[r0:r1, :] = _silu(y + cb_ref[...]).astype(a_ref.dtype)

        chunks = _row_chunks(bm)
        for ci, (r0, r1) in enumerate(chunks):
            ubuf[_CARRY + r0:_CARRY + r1, :] = _dot(x_ref[r0:r1, :], w_ref[...])
            if ci > 0:
                epilogue(*chunks[ci - 1])
        epilogue(*chunks[-1])
        tail_ref[0] = ubuf[_CARRY + bm - 3:_CARRY + bm, :]
        ubuf[0:_CARRY, :] = ubuf[bm:bm + _CARRY, :]

    @pl.when(i == n_p)
    def _():
        r = _dot(x_ref[0:ts, :], w_ref[...])
        rs_ref[...] = r
        y = s0_ref[...] * cw_ref[0:1, :]
        y = y + s1_ref[...] * cw_ref[1:2, :]
        y = y + s2_ref[...] * cw_ref[2:3, :]
        y = y + r * cw_ref[3:4, :]
        a_ref[0:ts, :] = _silu(y + cb_ref[...]).astype(a_ref.dtype)


def _xbc_branch(h, w_in, conv_w, conv_b, state, *, col0, n, tp, ts, seq, bm, bn):
    t, kdim = h.shape
    n_p = tp // bm
    tps = seq // bm
    nb = tp // seq
    nj = n // bn
    cb = col0 // bn
    assert col0 % bn == 0
    st = state.reshape(ts, 3 * n)
    return pl.pallas_call(
        functools.partial(_xbc_kernel, n_p=n_p, bm=bm, ts=ts, tps=tps),
        grid=(nj, n_p + 1),
        in_specs=[
            pl.BlockSpec((bm, kdim), lambda j, i: (i, 0)),
            pl.BlockSpec((kdim, bn), lambda j, i: (0, cb + j)),
            pl.BlockSpec((4, bn), lambda j, i: (0, j)),
            pl.BlockSpec((1, bn), lambda j, i: (0, j)),
            pl.BlockSpec((ts, bn), lambda j, i: (0, j)),
            pl.BlockSpec((ts, bn), lambda j, i: (0, nj + j)),
            pl.BlockSpec((ts, bn), lambda j, i: (0, 2 * nj + j)),
        ],
        out_specs=[
            pl.BlockSpec((bm, bn), lambda j, i: (i, j)),
            pl.BlockSpec((1, 3, bn), lambda j, i: (jnp.minimum(i, n_p - 1) // tps, 0, j)),
            pl.BlockSpec((ts, bn), lambda j, i: (0, j)),
        ],
        out_shape=[
            jax.ShapeDtypeStruct((t, n), BF16),
            jax.ShapeDtypeStruct((nb, 3, n), F32),
            jax.ShapeDtypeStruct((ts, n), F32),
        ],
        scratch_shapes=[pltpu.VMEM((_CARRY + bm, bn), F32)] + _weight_scratch(w_in, kdim, bn),
        compiler_params=_params(2),
        name="xbc_branch",
    )(h, w_in, conv_w, conv_b.reshape(1, n), st, st, st)


def _expand_heads(vals, rep2_ref):
    hi, lo = _split_hi_lo(vals)
    return _dot(jnp.concatenate([hi, lo], axis=1), rep2_ref[...])


def _ssd_prompt_kernel(xs_ref, b_ref, c_ref, dt_ref, z_ref, alog_ref, dexp_ref, nw_ref, rep2_ref,
                       ym_ref, st_out_ref,
                       st_ref, dte_ref, eae_ref, dee_ref, acg_ref, act_ref, y_ref,
                       *, n_chunks, groups, hpg, hd, ns):
    q = SSD_CHUNK
    c = pl.program_id(1)
    gw = hpg * hd

    @pl.when(c == 0)
    def _():
        st_ref[...] = jnp.zeros(st_ref.shape, F32)

    dt = dt_ref[...]
    a = -jnp.exp(alog_ref[...])
    da = dt * a
    row = lax.broadcasted_iota(jnp.int32, (q, q), 0)
    col = lax.broadcasted_iota(jnp.int32, (q, q), 1)
    tril = (row >= col)
    hi, lo = _split_hi_lo(da)
    trilb = jnp.where(tril, 1.0, 0.0).astype(BF16)
    mid = (da - hi.astype(F32) - lo.astype(F32)).astype(BF16)
    acum = _dot(trilb, hi) + _dot(trilb, lo) + _dot(trilb, mid)
    a_last = acum[q - 1:q, :]
    dte_ref[...] = _expand_heads(dt, rep2_ref)
    eae_ref[...] = _expand_heads(jnp.exp(acum), rep2_ref)
    dee_ref[...] = _expand_heads(jnp.exp(a_last - acum), rep2_ref)
    act_ref[...] = acum.T
    acg_ref[0] = acum
    for g in range(1, groups):
        acg_ref[g] = pltpu.roll(acum, shift=acum.shape[1] - g * hpg, axis=1)

    lane = lax.broadcasted_iota(jnp.int32, (q, 2 * hd), 1)
    lo_half = lane < hd

    def group_body(g, carry):
        off = pl.multiple_of(g * gw, gw)
        noff = pl.multiple_of(g * ns, ns)
        xs_g = xs_ref[:, pl.ds(off, gw)].astype(F32)
        b_g = b_ref[:, pl.ds(noff, ns)]
        c_g = c_ref[:, pl.ds(noff, ns)]
        bt = b_g.astype(F32).T.astype(BF16)
        cb = _dot(c_g, bt)
        xdt = xs_g * dte_ref[:, pl.ds(off, gw)]
        xdt_b = xdt.astype(BF16)
        xd_b = (xdt * dee_ref[:, pl.ds(off, gw)]).astype(BF16)
        eae_g = eae_ref[:, pl.ds(off, gw)]
        st_g = st_ref[g]
        y_off = _dot(c_g, st_g.astype(BF16)) * eae_g
        st_ref[g] = st_g * eae_g[q - 1:q, :] + _dot(bt, xd_b)
        ac = acg_ref[g]
        hoff = pl.multiple_of(g * hpg, hpg)
        ac_t = act_ref[pl.ds(hoff, hpg), :]
        for pr in range(hpg // 2):
            ms = []
            for r in (2 * pr, 2 * pr + 1):
                seg = ac[:, r:r + 1] - ac_t[r:r + 1, :]
                ms.append((jnp.where(tril, jnp.exp(seg), 0.0) * cb).astype(BF16))
            lhs = jnp.concatenate(ms, axis=1)
            xp = xdt_b[:, pr * 2 * hd:(pr + 1) * 2 * hd]
            zero = jnp.zeros_like(xp)
            rhs = jnp.concatenate([jnp.where(lo_half, xp, zero), jnp.where(lo_half, zero, xp)], axis=0)
            sl = slice(pr * 2 * hd, (pr + 1) * 2 * hd)
            lanes_pr = pl.ds(pl.multiple_of(off + pr * 2 * hd, 2 * hd), 2 * hd)
            y_pair = _dot(lhs, rhs) + y_off[:, sl] + dexp_ref[:, lanes_pr] * xs_g[:, sl]
            yz = y_pair * z_ref[:, lanes_pr].astype(F32)
            y_ref[:, lanes_pr] = yz
            carry = carry + yz * yz
        return carry

    ssq = lax.fori_loop(0, groups, group_body, jnp.zeros((q, 2 * hd), F32))
    ms = jnp.sum(ssq, axis=-1, keepdims=True) * (1.0 / (groups * gw))
    ym_ref[...] = (y_ref[...] * lax.rsqrt(ms + EPS) * nw_ref[...]).astype(ym_ref.dtype)

    @pl.when(c == n_chunks - 1)
    def _():
        for g in range(groups):
            st_out_ref[0, g * gw:(g + 1) * gw, :] = st_ref[g].T


def _ssd_prompt(act, dt, zact, a_log, d_exp, norm_w, rep2, *, nb, seq, heads, hd, groups, ns):
    q = SSD_CHUNK
    d_inner = heads * hd
    hpg = heads // groups
    gw = hpg * hd
    n_chunks = seq // q
    gn = groups * ns
    assert heads == q and hd * 2 == q and ns == q and d_inner % gn == 0
    row = lambda b, c: b * n_chunks + c
    return pl.pallas_call(
        functools.partial(_ssd_prompt_kernel, n_chunks=n_chunks, groups=groups, hpg=hpg, hd=hd, ns=ns),
        grid=(nb, n_chunks),
        in_specs=[
            pl.BlockSpec((q, d_inner), lambda b, c: (row(b, c), 0)),
            pl.BlockSpec((q, gn), lambda b, c: (row(b, c), d_inner // gn)),
            pl.BlockSpec((q, gn), lambda b, c: (row(b, c), d_inner // gn + 1)),
            pl.BlockSpec((q, heads), lambda b, c: (row(b, c), 0)),
            pl.BlockSpec((q, d_inner), lambda b, c: (row(b, c), 0)),
            pl.BlockSpec((1, heads), lambda b, c: (0, 0)),
            pl.BlockSpec((1, d_inner), lambda b, c: (0, 0)),
            pl.BlockSpec((1, d_inner), lambda b, c: (0, 0)),
            pl.BlockSpec((2 * heads, d_inner), lambda b, c: (0, 0)),
        ],
        out_specs=[
            pl.BlockSpec((q, d_inner), lambda b, c: (row(b, c), 0)),
            pl.BlockSpec((1, d_inner, ns), lambda b, c: (b, 0, 0)),
        ],
        out_shape=[
            jax.ShapeDtypeStruct((nb * seq, d_inner), BF16),
            jax.ShapeDtypeStruct((nb, d_inner, ns), F32),
        ],
        scratch_shapes=[
            pltpu.VMEM((groups, ns, gw), F32),
            pltpu.VMEM((q, d_inner), F32),
            pltpu.VMEM((q, d_inner), F32),
            pltpu.VMEM((q, d_inner), F32),
            pltpu.VMEM((groups, q, heads), F32),
            pltpu.VMEM((heads, q), F32),
            pltpu.VMEM((q, d_inner), F32),
        ],
        compiler_params=_params(2),
        name="ssd_prompt",
    )(act, act, act, dt, zact, a_log.reshape(1, heads), d_exp, norm_w.reshape(1, d_inner), rep2)


_SAMPLE_SEQS_PER_STEP = 2


def _ssd_sample_kernel(st_ref, xs_ref, b_ref, c_ref, dt_ref, z_ref, alog_ref, dexp_ref, nw_ref, rep2_ref,
                       st_out_ref, ym_ref, *, groups, hpg, hd, ns):
    for s in range(st_ref.shape[0]):
        _ssd_sample_one(st_ref.at[s], xs_ref.at[s], b_ref.at[s], c_ref.at[s], dt_ref.at[s], z_ref.at[s],
                        alog_ref, dexp_ref, nw_ref, rep2_ref, st_out_ref.at[s], ym_ref.at[s],
                        groups=groups, hpg=hpg, hd=hd, ns=ns)


def _ssd_sample_one(st_ref, xs_ref, b_ref, c_ref, dt_ref, z_ref, alog_ref, dexp_ref, nw_ref, rep2_ref,
                    st_out_ref, ym_ref, *, groups, hpg, hd, ns):
    gw = hpg * hd
    d_inner = groups * gw
    xs = xs_ref[...].astype(F32)
    dt = dt_ref[...]
    a = -jnp.exp(alog_ref[...])
    pad = jnp.zeros((6, dt.shape[1]), F32)
    both = _expand_heads(jnp.concatenate([dt, jnp.exp(dt * a), pad], axis=0), rep2_ref)
    dte = both[0:1, :]
    dae = both[1:2, :]
    xdt = xs * dte
    kr = 2 * groups
    grp_of_lane = lax.broadcasted_iota(jnp.int32, (kr, d_inner), 1) // gw
    krow = lax.broadcasted_iota(jnp.int32, (kr, d_inner), 0)
    ltf = (jnp.where(grp_of_lane == krow, jnp.broadcast_to(xdt, (kr, d_inner)), 0.0)
           + jnp.where(krow == groups, jnp.broadcast_to(dae, (kr, d_inner)), 0.0))
    lt_hi, lt_lo = _split_hi_lo(ltf)
    lt = jnp.concatenate([lt_hi, lt_lo], axis=0)
    bmat = b_ref[...].astype(F32)
    rrow = lax.broadcasted_iota(jnp.int32, (groups, 2 * ns), 0)
    rlane = lax.broadcasted_iota(jnp.int32, (groups, 2 * ns), 1)
    r_top = jnp.concatenate([bmat, jnp.zeros_like(bmat)], axis=1)
    r_bot = jnp.where((rrow == 0) & (rlane >= ns), 1.0, 0.0)
    rtf = jnp.concatenate([r_top, r_bot], axis=0)
    rt = jnp.concatenate([rtf, rtf], axis=0).astype(BF16)
    both2 = lax.dot_general(lt, rt, (((0,), (0,)), ((), ())), preferred_element_type=F32)
    new = st_ref[...] * both2[:, ns:] + both2[:, :ns]
    st_out_ref[...] = new
    yg = _dot_nt(c_ref[...], new.astype(BF16))
    own = (lax.broadcasted_iota(jnp.int32, (groups, d_inner), 1) // gw
           == lax.broadcasted_iota(jnp.int32, (groups, d_inner), 0))
    y = jnp.sum(jnp.where(own, yg, 0.0), axis=0, keepdims=True)
    y = y + dexp_ref[...] * xs
    yz = y * z_ref[...].astype(F32)
    ms = jnp.mean(yz * yz, axis=-1, keepdims=True)
    ym_ref[...] = (yz * lax.rsqrt(ms + EPS) * nw_ref[...]).astype(ym_ref.dtype)


def _ssd_sample(state, xs, bmat, cmat, dt, zact, a_log, d_exp, norm_w, rep2, *, heads, hd, groups, ns):
    ts = state.shape[0]
    d_inner = heads * hd
    hpg = heads // groups
    spb = _SAMPLE_SEQS_PER_STEP if ts % _SAMPLE_SEQS_PER_STEP == 0 else 1
    seq_spec = lambda shape: pl.BlockSpec((spb,) + shape, lambda b: (b, 0, 0))
    const = lambda shape: pl.BlockSpec(shape, lambda b: (0, 0))
    return pl.pallas_call(
        functools.partial(_ssd_sample_kernel, groups=groups, hpg=hpg, hd=hd, ns=ns),
        grid=(ts // spb,),
        in_specs=[
            seq_spec((d_inner, ns)), seq_spec((1, d_inner)), seq_spec((groups, ns)), seq_spec((groups, ns)),
            seq_spec((1, heads)), seq_spec((1, d_inner)),
            const((1, heads)), const((1, d_inner)), const((1, d_inner)), const((2 * heads, d_inner)),
        ],
        out_specs=[seq_spec((d_inner, ns)), seq_spec((1, d_inner))],
        out_shape=[
            jax.ShapeDtypeStruct((ts, d_inner, ns), F32),
            jax.ShapeDtypeStruct((ts, 1, d_inner), F32),
        ],
        compiler_params=_params(1),
        name="ssd_sample",
    )(state, xs, bmat, cmat, dt, zact, a_log.reshape(1, heads), d_exp, norm_w.reshape(1, d_inner), rep2)


def _merge_kernel(*refs, n_p, bm, ts, has_xs, has_prev, has_wbf, n_cast):
    x_ref = refs[0]
    k = 1
    xs_ref = x_ref
    if has_xs:
        xs_ref = refs[k]
        k += 1
    w_ref, ga_ref = refs[k], refs[k + 1]
    k += 2
    prev_ref = None
    if has_prev:
        prev_ref = refs[k]
        k += 1
    cast_in = None
    if n_cast:
        cast_in = refs[k]
        k += 1
    o_ref = refs[k]
    k += 1
    i = pl.program_id(1)
    if n_cast:
        cast_out = refs[k]
        k += 1

        @pl.when(pl.program_id(0) * (n_p + 1) + i < n_cast)
        def _():
            cast_out[...] = cast_in[...].astype(cast_out.dtype)

    w_ref = _resident_weight(w_ref, refs[k] if has_wbf else None, i)

    def compute(rows, src_ref):
        for r0, r1 in _row_chunks(rows):
            acc = _dot(src_ref[r0:r1, :], w_ref[...]) * ga_ref[r0:r1, :].astype(F32)
            if has_prev:
                acc = acc + prev_ref[r0:r1, :].astype(F32)
            o_ref[r0:r1, :] = acc.astype(o_ref.dtype)

    @pl.when(i < n_p)
    def _():
        compute(bm, x_ref)

    @pl.when(i == n_p)
    def _():
        compute(ts, xs_ref)


_MERGE_CAST_ROWS = 128


def _merge(x, xs, w, gates, gate_col0, prev, *, tp, ts, bm, bn, out_dtype, name, also_cast=None):
    kdim = x.shape[1]
    n = w.shape[1]
    n_p = tp // bm
    gcb = gate_col0 // bn
    assert gate_col0 % bn == 0 and n % bn == 0 and tp % bm == 0
    has_xs = xs is not None
    has_prev = prev is not None
    n_cast = 0
    if also_cast is not None:
        crow, ccol = also_cast.shape
        blocks = crow // _MERGE_CAST_ROWS
        if crow % _MERGE_CAST_ROWS == 0 and blocks <= (n // bn) * (n_p + 1):
            n_cast = blocks
    if has_xs:
        in_specs = [pl.BlockSpec((bm, kdim), lambda j, i: (jnp.minimum(i, n_p - 1), 0)),
                    pl.BlockSpec((ts, kdim), lambda j, i: (0, 0))]
        args = [x, xs]
    else:
        in_specs = [pl.BlockSpec((bm, kdim), lambda j, i: (i, 0))]
        args = [x]
    in_specs += [_weight_spec(kdim, bn, 0), pl.BlockSpec((bm, bn), lambda j, i: (i, gcb + j))]
    args += [w, gates]
    if has_prev:
        in_specs.append(pl.BlockSpec((bm, bn), lambda j, i: (i, j)))
        args.append(prev)
    out_specs = [pl.BlockSpec((bm, bn), lambda j, i: (i, j))]
    out_shape = [jax.ShapeDtypeStruct((tp + ts, n), out_dtype)]
    if n_cast:
        cast_spec = pl.BlockSpec((_MERGE_CAST_ROWS, ccol),
                                 lambda j, i: (jnp.minimum(j * (n_p + 1) + i, n_cast - 1), 0))
        in_specs.append(cast_spec)
        args.append(also_cast)
        out_specs.append(cast_spec)
        out_shape.append(jax.ShapeDtypeStruct((crow, ccol), BF16))
    outs = pl.pallas_call(
        functools.partial(_merge_kernel, n_p=n_p, bm=bm, ts=ts, has_xs=has_xs, has_prev=has_prev,
                          has_wbf=w.dtype != BF16, n_cast=n_cast),
        grid=(n // bn, n_p + 1),
        in_specs=in_specs,
        out_specs=out_specs,
        out_shape=out_shape,
        scratch_shapes=_weight_scratch(w, kdim, bn),
        compiler_params=_params(2),
        name=name,
    )(*args)
    if also_cast is None:
        return outs[0]
    return outs[0], (outs[1] if n_cast else also_cast.astype(BF16))


def _topk_rows(s, k):
    r = s.shape[0]
    rows = lax.broadcasted_iota(jnp.int32, s.shape, 0).astype(F32)
    vals, idxs = [], []
    for _ in range(k):
        m = jnp.max(s, axis=0, keepdims=True)
        idx = jnp.min(jnp.where(s == m, rows, float(r)), axis=0, keepdims=True)
        vals.append(m)
        idxs.append(idx)
        s = jnp.where(rows == idx, NEG_INF, s)
    return jnp.concatenate(vals, axis=0), jnp.concatenate(idxs, axis=0)


def _pick_rows(table, sel):
    k = table.shape[0]
    out = jnp.zeros_like(sel)
    for r in range(k):
        out = jnp.where(sel == float(r), jnp.broadcast_to(table[r:r + 1, :], sel.shape), out)
    return out


def _staircase_counts(k):
    return [k // (i + 1) for i in range(k)]


def _fold_keys_kernel(wq_ref, k1_ref, k2_ref, o_ref, *, qh, nkeys):
    hp = lax.Precision.HIGHEST
    nt = (((1,), (1,)), ((), ()))
    w = wq_ref[...]
    o_ref[:, 0:nkeys] = lax.dot_general(w[:, :qh], k1_ref[0], nt, precision=hp,
                                        preferred_element_type=F32).astype(o_ref.dtype)
    o_ref[:, nkeys:] = lax.dot_general(w[:, qh:], k2_ref[0], nt, precision=hp,
                                       preferred_element_type=F32).astype(o_ref.dtype)


def _fold_keys(wq, keys):
    d = wq.shape[0]
    _, heads, nkeys, qh = keys.shape
    return pl.pallas_call(
        functools.partial(_fold_keys_kernel, qh=qh, nkeys=nkeys),
        grid=(heads,),
        in_specs=[
            pl.BlockSpec((d, 2 * qh), lambda hh: (0, hh)),
            pl.BlockSpec((1, nkeys, qh), lambda hh: (hh, 0, 0)),
            pl.BlockSpec((1, nkeys, qh), lambda hh: (hh, 0, 0)),
        ],
        out_specs=pl.BlockSpec((d, 2 * nkeys), lambda hh: (0, hh)),
        out_shape=jax.ShapeDtypeStruct((d, heads * 2 * nkeys), BF16),
        compiler_params=_params(1),
        name="peer_fold_keys",
    )(wq, keys[0], keys[1])


_ROUTE_CHUNKS_PER_ITER = 3


def _route_kernel(h_ref, wk_ref, ia_ref, ib_ref, gt_ref, s1_ref, s2_ref, *cand_refs, nkeys, lanes):
    k = PEER_TOPK
    s = _dot(h_ref[...], wk_ref[...])
    s1_ref[...] = s[:, :nkeys].T
    s2_ref[...] = s[:, nkeys:].T
    counts = _staircase_counts(k)
    starts = [sum(counts[:i]) for i in range(k)]
    n_cand = sum(counts)

    def one_chunk(ci, cand_ref):
        off = ci * lanes if isinstance(ci, int) else pl.multiple_of(ci * lanes, lanes)
        v1, i1 = _topk_rows(s1_ref[:, pl.ds(off, lanes)], k)
        v2, i2 = _topk_rows(s2_ref[:, pl.ds(off, lanes)], k)
        for i in range(k):
            cand_ref[starts[i]:starts[i] + counts[i], :] = v1[i:i + 1, :] + v2[0:counts[i], :]
        cand_ref[n_cand:, :] = jnp.full((cand_ref.shape[0] - n_cand, lanes), NEG_INF, F32)
        sv, sp = _topk_rows(cand_ref[...], k)
        e = jnp.exp(sv - sv[0:1, :])
        gt_ref[0, :, pl.ds(off, lanes)] = e / jnp.sum(e, axis=0, keepdims=True)
        sel_i = jnp.zeros_like(sp)
        sel_start = jnp.zeros_like(sp)
        for i in range(1, k):
            ge = sp >= float(starts[i])
            sel_i = sel_i + jnp.where(ge, 1.0, 0.0)
            sel_start = sel_start + jnp.where(ge, float(counts[i - 1]), 0.0)
        ia_ref[0, :, pl.ds(off, lanes)] = _pick_rows(i1, sel_i)
        ib_ref[0, :, pl.ds(off, lanes)] = _pick_rows(i2, sp - sel_start)

    n_chunks = s1_ref.shape[1] // lanes
    per_iter = len(cand_refs)

    def chunk_group(gi, carry):
        for c, cand_ref in enumerate(cand_refs):
            one_chunk(per_iter * gi + c, cand_ref)
        return carry

    lax.fori_loop(0, n_chunks // per_iter, chunk_group, 0)
    for c in range(n_chunks % per_iter):
        one_chunk(n_chunks - n_chunks % per_iter + c, cand_refs[c])


def _route(h2, wk, *, heads, nkeys, tq):
    t, d = h2.shape
    k = PEER_TOPK
    out = jax.ShapeDtypeStruct((heads, k, t), F32)
    out_spec = pl.BlockSpec((1, k, tq), lambda i, hh: (hh, 0, i))
    return pl.pallas_call(
        functools.partial(_route_kernel, nkeys=nkeys, lanes=128),
        grid=(t // tq, heads),
        in_specs=[
            pl.BlockSpec((tq, d), lambda i, hh: (i, 0)),
            pl.BlockSpec((d, 2 * nkeys), lambda i, hh: (0, hh)),
        ],
        out_specs=[out_spec, out_spec, out_spec],
        out_shape=[out, out, out],
        scratch_shapes=[pltpu.VMEM((nkeys, tq), F32), pltpu.VMEM((nkeys, tq), F32)]
        + [pltpu.VMEM((-(-sum(_staircase_counts(k)) // 8) * 8, 128), F32)] * _ROUTE_CHUNKS_PER_ITER,
        compiler_params=_params(2),
        name="peer_route",
    )(h2, wk)


_SCATTER_UNROLL = 16
_SCATTER_PITCH = 136
_PACK_ROWS = 16


def _scatter_kernel(ia_ref, ib_ref, gt_ref, *rest, nkeys, n_cast):
    if n_cast:
        u_ref, v_ref, w_ref, ub_ref, vb_ref, ia_s, ib_s, gt_s, wsc = rest

        @pl.when(pl.program_id(0) < n_cast)
        def _():
            ub_ref[...] = u_ref[...].astype(ub_ref.dtype)
            vb_ref[...] = v_ref[...].astype(vb_ref.dtype)
    else:
        w_ref, ia_s, ib_s, gt_s, wsc = rest
    tw = ia_ref.shape[1]
    ia_s[...] = ia_ref[...].T
    ib_s[...] = ib_ref[...].T
    gt_s[...] = gt_ref[...].T
    nsel = ia_ref.shape[0]
    key_id = lax.broadcasted_iota(jnp.int32, (nkeys, nsel), 0).astype(F32)

    def token_group(gi, carry):
        for k in range(_SCATTER_UNROLL):
            tk = gi * _SCATTER_UNROLL + k
            a_row = ia_s[pl.ds(tk, 1), :]
            b_row = ib_s[pl.ds(tk, 1), :]
            g_row = gt_s[pl.ds(tk, 1), :]
            at = jnp.where(key_id == a_row, jnp.broadcast_to(g_row, key_id.shape), 0.0).astype(BF16)
            bt = jnp.where(key_id == b_row, 1.0, 0.0).astype(BF16)
            wsc[pl.ds(pl.multiple_of(tk * _SCATTER_PITCH, 8), nkeys), :] = _dot_nt(at, bt)
        return carry

    lax.fori_loop(0, tw // _SCATTER_UNROLL, token_group, 0)

    def relayout(tg, carry):
        row0 = pl.multiple_of(tg * _PACK_ROWS, _PACK_ROWS)
        base = tg * (_PACK_ROWS * _SCATTER_PITCH)
        for a in range(nkeys):
            rows = wsc[pl.ds(base + a, _PACK_ROWS, stride=_SCATTER_PITCH), :]
            w_ref[pl.ds(row0, _PACK_ROWS), a * nkeys:(a + 1) * nkeys] = rows.astype(w_ref.dtype)
        return carry

    lax.fori_loop(0, tw // _PACK_ROWS, relayout, 0)


_CAST_ROWS = 256


def _scatter(ia, ib, gt, u, v, *, nkeys, tw):
    nsel, t = ia.shape
    assert tw % _SCATTER_UNROLL == 0 and tw % _PACK_ROWS == 0
    steps = t // tw
    ne, d = u.shape
    n_cast = ne // _CAST_ROWS if (ne % _CAST_ROWS == 0 and ne // _CAST_ROWS <= steps) else 0
    spec = pl.BlockSpec((nsel, tw), lambda i: (0, i))
    in_specs = [spec, spec, spec]
    out_specs = [pl.BlockSpec((tw, nkeys * nkeys), lambda i: (i, 0))]
    out_shape = [jax.ShapeDtypeStruct((t, nkeys * nkeys), BF16)]
    args = [ia, ib, gt]
    if n_cast:
        tab_spec = pl.BlockSpec((_CAST_ROWS, d), lambda i: (jnp.minimum(i, n_cast - 1), 0))
        in_specs += [tab_spec, tab_spec]
        out_specs += [tab_spec, tab_spec]
        out_shape += [jax.ShapeDtypeStruct((ne, d), BF16)] * 2
        args += [u, v]
    outs = pl.pallas_call(
        functools.partial(_scatter_kernel, nkeys=nkeys, n_cast=n_cast),
        grid=(steps,),
        in_specs=in_specs,
        out_specs=out_specs,
        out_shape=out_shape,
        scratch_shapes=[pltpu.VMEM((tw, nsel), F32)] * 3 + [pltpu.VMEM((tw * _SCATTER_PITCH, nkeys), F32)],
        compiler_params=_params(1),
        name="peer_scatter",
    )(*args)
    if n_cast:
        return outs
    return outs[0], u.astype(BF16), v.astype(BF16)


def _experts_kernel(h_ref, u_ref, v_ref, w_ref, o_ref):
    e = pl.program_id(1)

    @pl.when(e == 0)
    def _():
        o_ref[...] = jnp.zeros(o_ref.shape, F32)

    tm = h_ref.shape[0]
    rc = EXPERT_ROW_CHUNK if tm % EXPERT_ROW_CHUNK == 0 else tm
    for r0 in range(0, tm, rc):
        s = _dot_nt(h_ref[r0:r0 + rc, :], u_ref[...])
        gelu = 0.5 * s * (1.0 + lax.erf(s * (2.0 ** -0.5)))
        act = gelu * w_ref[r0:r0 + rc, :].astype(F32)
        o_ref[r0:r0 + rc, :] += _dot(act.astype(BF16), v_ref[...])


def _experts(h2, u, v, w, *, tm, te):
    t, d = h2.shape
    ne = u.shape[0]
    return pl.pallas_call(
        _experts_kernel,
        grid=(t // tm, ne // te),
        in_specs=[
            pl.BlockSpec((tm, d), lambda i, e: (i, 0), pipeline_mode=pl.Buffered(1)),
            pl.BlockSpec((te, d), lambda i, e: (e, 0)),
            pl.BlockSpec((te, d), lambda i, e: (e, 0)),
            pl.BlockSpec((tm, te), lambda i, e: (i, e)),
        ],
        out_specs=pl.BlockSpec((tm, d), lambda i, e: (i, 0), pipeline_mode=pl.Buffered(1)),
        out_shape=jax.ShapeDtypeStruct((t, d), F32),
        compiler_params=_params(2),
        name="peer_experts",
    )(h2, u, v, w)


def _largest_divisor(n, cap, multiple):
    best = None
    for cand in range(multiple, min(n, cap) + 1, multiple):
        if n % cand == 0:
            best = cand
    assert best is not None, (n, cap, multiple)
    return best


def kernel(x_prompt, x_sample, state_shortconv, state_mamba_conv, state_ssm, ln1_w, w_in, sc_conv_w,
           sc_out_w, m_conv_w, m_conv_b, m_dt_bias, m_A_log, m_D, m_norm_w, m_out_w, w_o, ln2_w,
           peer_wq, peer_keys, peer_u, peer_v, final_norm_w):
    nb, seq, d = x_prompt.shape
    ts = x_sample.shape[0]
    assert x_sample.shape[1] == 1 and ln1_w.shape[0] == 1, "single layer, one new token per sample"
    tp = nb * seq
    t = tp + ts
    d_sc = state_shortconv.shape[-1]
    conv_dim = state_mamba_conv.shape[-1]
    _, _, heads, hd, ns = state_ssm.shape
    d_inner = heads * hd
    groups = (conv_dim - d_inner) // (2 * ns)
    gn = groups * ns
    nkeys = peer_keys.shape[3]
    c_z = 3 * d_sc
    c_xbc = c_z + d_inner
    c_dt = c_xbc + conv_dim
    c_gate = c_dt + heads

    bm = _largest_divisor(seq, 1024, 128)
    bm_small = _largest_divisor(seq, 512, 128)
    bt = _largest_divisor(seq, 256, 8)

    xp2 = x_prompt.reshape(tp, d)
    xs2 = x_sample.reshape(ts, d)
    w_in0 = w_in[0]

    h = _norm_in(xp2, xs2, ln1_w[0], bt=bt)

    g, sc_tail_p, sc_u_s = _sc_branch(h, w_in0, sc_conv_w[0], state_shortconv[0], d_sc=d_sc, tp=tp, ts=ts,
                                      seq=seq, bm=bm, bn=_largest_divisor(d_sc, 256, 128))
    act, mc_tail_p, mc_raw_s = _xbc_branch(h, w_in0, m_conv_w[0], m_conv_b[0], state_mamba_conv[0],
                                           col0=c_xbc, n=conv_dim, tp=tp, ts=ts, seq=seq, bm=bm,
                                           bn=_largest_divisor(conv_dim, 512, 128))
    zact = _mm_act(h, w_in0, col0=c_z, n=d_inner, tp=tp, ts=ts, bm=bm_small,
                   bn=_largest_divisor(d_inner, 1024, 128), act="silu", out_dtype=BF16, name="proj_z")
    dt = _mm_act(h, w_in0, col0=c_dt, n=heads, tp=tp, ts=ts, bm=bm, bn=heads, act="softplus",
                 out_dtype=F32, bias=m_dt_bias[0], name="proj_dt")
    gates = _mm_act(h, w_in0, col0=c_gate, n=2 * d, tp=tp, ts=ts, bm=bm_small,
                    bn=_largest_divisor(d, 1024, 128), act="sigmoid", out_dtype=BF16, name="proj_gates")

    d_exp = jnp.repeat(m_D[0], hd).reshape(1, d_inner)
    rep = (jnp.arange(d_inner)[None, :] // hd == jnp.arange(heads)[:, None]).astype(BF16)
    rep2 = jnp.concatenate([rep, rep], axis=0)
    ym_p, ssm_p = _ssd_prompt(act, dt, zact, m_A_log[0], d_exp, m_norm_w[0], rep2, nb=nb, seq=seq,
                              heads=heads, hd=hd, groups=groups, ns=ns)
    act_s = act[tp:]
    ssm_s, ym_s = _ssd_sample(
        state_ssm[0].reshape(ts, d_inner, ns),
        act_s[:, :d_inner].astype(F32).reshape(ts, 1, d_inner),
        act_s[:, d_inner:d_inner + gn].reshape(ts, groups, ns),
        act_s[:, d_inner + gn:].reshape(ts, groups, ns),
        dt[tp:].reshape(ts, 1, heads),
        zact[tp:].astype(F32).reshape(ts, 1, d_inner),
        m_A_log[0], d_exp, m_norm_w[0], rep2, heads=heads, hd=hd, groups=groups, ns=ns)
    bn_o = _largest_divisor(d, 512, 128)
    mix_a, m_out_b = _merge(g, None, sc_out_w[0], gates, 0, None, tp=tp, ts=ts, bm=bm, bn=bn_o,
                            out_dtype=F32, name="merge_a", also_cast=m_out_w[0])
    mixed = _merge(ym_p, ym_s.reshape(ts, d_inner).astype(BF16), m_out_b, gates, d, mix_a,
                   tp=tp, ts=ts, bm=bm_small, bn=bn_o, out_dtype=BF16, name="merge_b")
    x1 = _mm_act(mixed, w_o[0], col0=0, n=d, tp=tp, ts=ts, bm=bm, bn=bn_o, act="none",
                 out_dtype=F32, resid=(xp2, xs2), name="proj_o")

    tok_tile = _largest_divisor(t, 640, 128)
    h2 = _norm_mid(x1, ln2_w[0], bt=_largest_divisor(t, 256, 8))
    wk = _fold_keys(peer_wq[0], peer_keys[0])
    ia, ib, gt = _route(h2, wk, heads=peer_keys.shape[2], nkeys=nkeys, tq=_largest_divisor(t, 1664, 128))
    nsel = ia.shape[0] * ia.shape[1]
    w, u_b, v_b = _scatter(ia.reshape(nsel, t), ib.reshape(nsel, t), gt.reshape(nsel, t),
                           peer_u[0], peer_v[0], nkeys=nkeys, tw=128)
    peer = _experts(h2, u_b, v_b, w,
                    tm=tok_tile, te=_largest_divisor(nkeys * nkeys, 1024, 128))
    y_p, y_s = _norm_out(x1, peer, final_norm_w, tp=tp, ts=ts, bt=bt)

    new_sc_s = jnp.stack([state_shortconv[0][:, 1], sc_u_s], axis=1)
    new_mc_s = jnp.stack([state_mamba_conv[0][:, 1], state_mamba_conv[0][:, 2], mc_raw_s], axis=1)
    return (
        y_p.reshape(nb, seq, d),
        y_s.reshape(ts, 1, d),
        sc_tail_p[None],
        mc_tail_p[None],
        ssm_p.reshape(1, nb, heads, hd, ns),
        new_sc_s[None],
        new_mc_s[None],
        ssm_s.reshape(1, ts, heads, hd, ns),
    )
```
